```python
import jax, jax.numpy as jnp
from jax import lax
import numpy as np

D_MODEL = 1024
BATCH = 8
SEQ = 2048
DEPTH = 2
DEC_BATCH = 32
DEC_SEQ = 4
PAST_LEN = 8192
PAGE_SIZE = 128

N_BRANCH = 4
BRANCH_W = D_MODEL // 4
NORM_EPS = 1e-6
A_BLOCKS = 4
A_BLOCK_W = BRANCH_W // A_BLOCKS
A_CONV_W = 4
A_C = 8.0
B_HEAD = 64
B_HEADS = BRANCH_W // B_HEAD
B_DECAY_LORA = 64
B_AAA_LORA = 64
B_GATE_LORA = 128
B_GN_EPS = 64e-5
B_SPLITS = (BRANCH_W, BRANCH_W, BRANCH_W, B_DECAY_LORA, B_AAA_LORA, B_GATE_LORA)
B_COLS = BRANCH_W * 3 + B_DECAY_LORA + B_AAA_LORA + B_GATE_LORA
C_HEADS = 4
C_EXPAND = 128
C_VDIM = BRANCH_W // C_HEADS
C_FDIM = C_HEADS * C_EXPAND
C_CHUNK = 64
D_HEADS = 4
D_HEAD = BRANCH_W // D_HEADS
D_PATTERNS = ((128, 1), (512, 4), (2048, 16))
D_WIN_MAX = 2048
D_QBLOCK = 128
ROPE_THETA = 10000.0
NEG_INF = -1e30
D_FF = 3 * D_MODEL
FFN_CONV_W = 3
IN_SPLITS = (BRANCH_W, BRANCH_W,
             B_COLS,
             C_FDIM, C_FDIM, BRANCH_W, BRANCH_W,
             BRANCH_W, BRANCH_W, BRANCH_W,
             N_BRANCH * D_MODEL)
N_IN = 2 * BRANCH_W + B_COLS + 2 * C_FDIM + 5 * BRANCH_W + N_BRANCH * D_MODEL

kernel_name = 'hybrid_rglru_rwkv7_hgrn2_dilated_step'


def split_cols(c, sizes):
    out, start = [], 0
    for s in sizes:
        out.append(c[..., start:start + s])
        start += s
    return out


def rms_norm(x, g):
    x32 = x.astype(jnp.float32)
    y = x32 * lax.rsqrt(jnp.mean(x32 * x32, axis=-1, keepdims=True) + NORM_EPS)
    return (y * g.astype(jnp.float32)).astype(x.dtype)


def causal_dwconv(x, buf, w, b):
    t_ = x.shape[1]
    xp = jnp.concatenate([buf.astype(jnp.float32), x.astype(jnp.float32)], axis=1)
    w32 = w.astype(jnp.float32)
    y = b.astype(jnp.float32)
    for j in range(w.shape[0]):
        y = y + w32[j] * xp[:, j:j + t_]
    return y, xp[:, t_:]


def rope(t, pos):
    half = t.shape[-1] // 2
    inv = ROPE_THETA ** (-jnp.arange(half, dtype=jnp.float32) / half)
    ang = pos.astype(jnp.float32)[:, None] * inv[None, :]
    cos = jnp.cos(ang)[None, :, None, :]
    sin = jnp.sin(ang)[None, :, None, :]
    t = t.astype(jnp.float32)
    t1, t2 = t[..., :half], t[..., half:]
    return jnp.concatenate([t1 * cos - t2 * sin, t2 * cos + t1 * sin], axis=-1)


def _lin_combine(e1, e2):
    a1, b1 = e1
    a2, b2 = e2
    return a1 * a2, a2 * b1 + b2


def rglru_branch(ax, ag, h0, conv_buf, conv_w, conv_b, gx_w, gx_b, ga_w, ga_b, lam):
    f32 = jnp.float32
    xc, new_buf = causal_dwconv(ax, conv_buf, conv_w, conv_b)
    b_, t_ = xc.shape[:2]
    xb = xc.reshape(b_, t_, A_BLOCKS, A_BLOCK_W)
    gate_x = jax.nn.sigmoid(jnp.einsum('bthi,hij->bthj', xb, gx_w.astype(f32)).reshape(b_, t_, BRANCH_W) + gx_b)
    gate_a = jax.nn.sigmoid(jnp.einsum('bthi,hij->bthj', xb, ga_w.astype(f32)).reshape(b_, t_, BRANCH_W) + ga_b)
    log_a = -A_C * gate_a * jax.nn.softplus(-lam.astype(f32))
    a = jnp.exp(log_a)
    b_in = jnp.sqrt(-jnp.expm1(2.0 * log_a)) * (gate_x * xc)
    a_cum, h_zero = lax.associative_scan(_lin_combine, (a, b_in), axis=1)
    h = a_cum * h0.astype(f32)[:, None] + h_zero
    y = h * jax.nn.gelu(ag.astype(f32))
    return y, h[:, -1], new_buf


def rwkv7_scan(r, w, k, v, a, b, s0):
    def step(s, inp):
        r_t, w_t, k_t, v_t, a_t, b_t = inp
        sa = jnp.einsum('bhvk,bhk->bhv', s, a_t)
        s = s * w_t[:, :, None, :] + sa[..., None] * b_t[:, :, None, :] + v_t[..., None] * k_t[:, :, None, :]
        return s, jnp.einsum('bhvk,bhk->bhv', s, r_t)
    xs = tuple(jnp.moveaxis(z, 1, 0) for z in (r, w, k, v, a, b))
    s_fin, y = lax.scan(step, s0, xs)
    return jnp.moveaxis(y, 0, 1), s_fin


def rwkv7_branch(cb, prev, s0, mu, w0, w2, a0, a2, g2, k_k, k_a, r_k, ln_w, ln_b):
    f32 = jnp.float32
    cb = cb.astype(f32)
    b_, t_ = cb.shape[:2]
    shifted = jnp.concatenate([prev.astype(f32)[:, None], cb[:, :-1]], axis=1)
    cm = cb + (shifted - cb) * mu.astype(f32)
    r, k, v, wl, al, gl = split_cols(cm, B_SPLITS)
    w = -jax.nn.softplus(-(w0 + jnp.tanh(wl) @ w2.astype(f32))) - 0.5
    decay = jnp.exp(-jnp.exp(w))
    a = jax.nn.sigmoid(a0 + al @ a2.astype(f32))
    g = jax.nn.sigmoid(gl) @ g2.astype(f32)

    def heads(z):
        return z.reshape(b_, t_, B_HEADS, B_HEAD)
    kk = heads(k * k_k)
    kk = kk / jnp.maximum(jnp.sqrt(jnp.sum(kk * kk, axis=-1, keepdims=True)), 1e-12)
    k = k * (1.0 + (a - 1.0) * k_a)
    rh, kh, vh = heads(r), heads(k), heads(v)
    y, s_new = rwkv7_scan(rh, heads(decay), kh, vh, -kk, kk * heads(a), s0.astype(f32))
    mean = jnp.mean(y, axis=-1, keepdims=True)
    var = jnp.mean(jnp.square(y - mean), axis=-1, keepdims=True)
    y = ((y - mean) * lax.rsqrt(var + B_GN_EPS)).reshape(b_, t_, BRANCH_W) * ln_w + ln_b
    bonus = jnp.sum(rh * kh * r_k, axis=-1, keepdims=True) * vh
    out = (y + bonus.reshape(b_, t_, BRANCH_W)) * g
    return out, cb[:, -1], s_new


def gla_chunked(q, k, v, logf, s0):
    b_, t_, h_, kd = q.shape
    vd = v.shape[-1]
    c_ = C_CHUNK if t_ % C_CHUNK == 0 else t_
    n_ = t_ // c_

    def chunks(z):
        return jnp.moveaxis(z.reshape(b_, n_, c_, h_, z.shape[-1]), 1, 0)
    causal = jnp.tril(jnp.ones((c_, c_), dtype=bool))[None, :, :, None, None]

    def step(s, inp):
        qc, kc, vc, gc = inp
        bcum = jnp.cumsum(gc, axis=1)
        o_inter = jnp.einsum('bthk,bhkv->bthv', qc * jnp.exp(bcum), s)
        diff = bcum[:, :, None] - bcum[:, None, :]
        dec = jnp.exp(jnp.where(causal, diff, -jnp.inf))
        att = jnp.einsum('bthk,bshk,btshk->btsh', qc, kc, dec)
        o_intra = jnp.einsum('btsh,bshv->bthv', att, vc)
        b_last = bcum[:, -1]
        s = s * jnp.exp(b_last)[..., None] + jnp.einsum('bshk,bshv->bhkv', kc * jnp.exp(b_last[:, None] - bcum), vc)
        return s, o_inter + o_intra
    s_fin, o = lax.scan(step, s0, (chunks(q), chunks(k), chunks(v), chunks(logf)))
    return jnp.moveaxis(o, 0, 1).reshape(b_, t_, h_, vd), s_fin


def hgrn2_branch(cq, cf, ci, cg, s0, lb, norm_g):
    f32 = jnp.float32
    b_, t_ = cq.shape[:2]
    q = cq.astype(f32).reshape(b_, t_, C_HEADS, C_EXPAND)
    f = cf.astype(f32).reshape(b_, t_, C_HEADS, C_EXPAND)
    v = ci.astype(f32).reshape(b_, t_, C_HEADS, C_VDIM)
    lbh = lb.reshape(C_HEADS, C_EXPAND)
    fg = lbh + (1.0 - lbh) * jax.nn.sigmoid(f)
    o, s_new = gla_chunked(q, 1.0 - fg, v, jnp.log(fg), s0.astype(f32))
    o = o * lax.rsqrt(jnp.mean(o * o, axis=-1, keepdims=True) + NORM_EPS) * norm_g.astype(f32)
    o = o.reshape(b_, t_, BRANCH_W) * jax.nn.silu(cg.astype(f32))
    return o, s_new


def dilated_block(q_blk, s0, k_all, v_all):
    f32 = jnp.float32
    qb = q_blk.shape[1]
    qi = s0 + jnp.arange(qb)
    q32 = q_blk.astype(f32) * (D_HEAD ** -0.5)
    stats = []
    for win, dil in D_PATTERNS:
        nk = win // dil + 1
        idx = qi[:, None] - dil * jnp.arange(nk)[None, :]
        valid = idx >= 0
        idx = jnp.maximum(idx, 0)
        kg = jnp.take(k_all, idx, axis=1).astype(f32)
        vg = jnp.take(v_all, idx, axis=1).astype(f32)
        s = jnp.einsum('bqhd,bqjhd->bhqj', q32, kg)
        s = jnp.where(valid[None, None], s, NEG_INF)
        m = jnp.max(s, axis=-1)
        e = jnp.exp(s - m[..., None])
        stats.append((m, jnp.sum(e, axis=-1), jnp.einsum('bhqj,bqjhd->bhqd', e, vg)))
    m_all = stats[0][0]
    for m, _, _ in stats[1:]:
        m_all = jnp.maximum(m_all, m)
    num = jnp.zeros_like(stats[0][2])
    den = jnp.zeros_like(stats[0][1])
    for m, l, n in stats:
        c = jnp.exp(m - m_all)
        num = num + c[..., None] * n
        den = den + c * l
    return jnp.transpose(num / den[..., None], (0, 2, 1, 3))


def dilated_branch(dq, dk, dv, pos, k_cache, v_cache):
    b_, t_ = dq.shape[:2]

    def hs(z):
        return z.reshape(b_, t_, D_HEADS, D_HEAD)
    q = rope(hs(dq), pos)
    k = rope(hs(dk), pos)
    v = hs(dv).astype(jnp.float32)
    if k_cache is None:
        k_all, v_all, off = k, v, 0
    else:
        k_all = jnp.concatenate([k_cache.astype(jnp.float32), k], axis=1)
        v_all = jnp.concatenate([v_cache.astype(jnp.float32), v], axis=1)
        off = k_cache.shape[1]
    qb = D_QBLOCK if t_ % D_QBLOCK == 0 else t_
    nb = t_ // qb
    q_blocks = jnp.moveaxis(q.reshape(b_, nb, qb, D_HEADS, D_HEAD), 1, 0)
    starts = off + qb * jnp.arange(nb)
    out = lax.map(lambda args: dilated_block(args[0], args[1], k_all, v_all), (q_blocks, starts))
    return jnp.moveaxis(out, 0, 1).reshape(b_, t_, BRANCH_W), k, v


def run_trunk(x, pos, h_a, conv_a, wkv_b, shift_b, s_c, k_d, v_d, conv_ffn, W):
    f32 = jnp.float32
    b_, t_, _ = x.shape
    lb_all = jax.nn.softmax(W['c_lb'].astype(f32), axis=0)
    lb_all = jnp.cumsum(lb_all, axis=0) - lb_all[0]
    keep = min(D_WIN_MAX, t_) if k_d is None else t_
    o_ha, o_ca, o_wkv, o_sh, o_sc, o_k, o_v, o_cf = [], [], [], [], [], [], [], []
    for l in range(DEPTH):
        u = rms_norm(x, W['norm1_g'][l])
        cols = jnp.einsum('btd,dc->btc', u, W['w_in'][l])
        ax, ag, cb, cq, cf, ci, cg, dq, dk, dv, gt = split_cols(cols, IN_SPLITS)
        y_a, h_new, ca_new = rglru_branch(ax, ag, h_a[l], conv_a[l], W['a_conv_w'][l], W['a_conv_b'][l],
                                          W['a_gx_w'][l], W['a_gx_b'][l], W['a_ga_w'][l], W['a_ga_b'][l],
                                          W['a_lambda'][l])
        y_b, sh_new, wkv_new = rwkv7_branch(cb, shift_b[l], wkv_b[l], W['b_mu'][l], W['b_w0'][l], W['b_w2'][l],
                                            W['b_a0'][l], W['b_a2'][l], W['b_g2'][l], W['b_k_k'][l],
                                            W['b_k_a'][l], W['b_r_k'][l], W['b_ln_w'][l], W['b_ln_b'][l])
        y_c, sc_new = hgrn2_branch(cq, cf, ci, cg, s_c[l], lb_all[l], W['c_norm_g'][l])
        kc = None if k_d is None else k_d[l]
        vc = None if v_d is None else v_d[l]
        y_d, k_rows, v_rows = dilated_branch(dq, dk, dv, pos, kc, vc)
        branches = jnp.stack([y_a, y_b, y_c, y_d], axis=2).astype(x.dtype)
        z = jnp.einsum('btnc,ncd->btnd', branches, W['w_branch'][l]).astype(f32)
        gates = jax.nn.sigmoid(gt.astype(f32).reshape(b_, t_, N_BRANCH, D_MODEL))
        merged = jnp.sum(gates * z, axis=2).astype(x.dtype)
        x = x + jnp.einsum('btd,de->bte', merged, W['w_out'][l])
        v_in = rms_norm(x, W['norm2_g'][l])
        hg = jnp.einsum('btd,df->btf', v_in, W['ffn_w_gate'][l])
        hu = jnp.einsum('btd,df->btf', v_in, W['ffn_w_up'][l])
        hg_c, cf_new = causal_dwconv(hg, conv_ffn[l], W['ffn_conv_w'][l], W['ffn_conv_b'][l])
        hmid = (jax.nn.gelu(hg_c) * hu.astype(f32)).astype(x.dtype)
        x = x + jnp.einsum('btf,fd->btd', hmid, W['ffn_w_down'][l])
        o_ha.append(h_new)
        o_ca.append(ca_new)
        o_wkv.append(wkv_new)
        o_sh.append(sh_new)
        o_sc.append(sc_new)
        o_k.append(k_rows[:, t_ - keep:])
        o_v.append(v_rows[:, t_ - keep:])
        o_cf.append(cf_new)
    y = rms_norm(x, W['final_norm_g'])
    return (y, jnp.stack(o_ha), jnp.stack(o_ca), jnp.stack(o_wkv), jnp.stack(o_sh), jnp.stack(o_sc),
            jnp.stack(o_k), jnp.stack(o_v), jnp.stack(o_cf))


def setup_inputs(seed: int = 0) -> dict:
    key = jax.random.key(seed)
    ks = iter(jax.random.split(key, 64))
    f32 = jnp.float32

    def nrm(shape, s):
        return jax.random.normal(next(ks), shape, f32) * s
    cache_win = min(D_WIN_MAX, PAST_LEN)
    u = jax.random.uniform(next(ks), (DEPTH, BRANCH_W), f32, minval=0.9, maxval=0.999)
    p = u ** (1.0 / A_C)
    a_lambda = jnp.log(p) - jnp.log1p(-p)
    return {
        'x_prompt': nrm((BATCH, SEQ, D_MODEL), 1.0),
        'x_sample': nrm((DEC_BATCH, DEC_SEQ, D_MODEL), 1.0),
        'state_a_h': nrm((DEPTH, DEC_BATCH, BRANCH_W), 0.5),
        'state_a_conv': nrm((DEPTH, DEC_BATCH, A_CONV_W - 1, BRANCH_W), 1.0),
        'state_b_wkv': nrm((DEPTH, DEC_BATCH, B_HEADS, B_HEAD, B_HEAD), 0.5),
        'state_b_shift': nrm((DEPTH, DEC_BATCH, B_COLS), 1.0),
        'state_c_s': nrm((DEPTH, DEC_BATCH, C_HEADS, C_EXPAND, C_VDIM), 0.5),
        'cache_d_k': nrm((DEPTH, DEC_BATCH, cache_win, D_HEADS, D_HEAD), 1.0),
        'cache_d_v': nrm((DEPTH, DEC_BATCH, cache_win, D_HEADS, D_HEAD), 1.0),
        'state_ffn_conv': nrm((DEPTH, DEC_BATCH, FFN_CONV_W - 1, D_FF), 1.0),
        'norm1_g': 1.0 + nrm((DEPTH, D_MODEL), 0.05),
        'w_in': nrm((DEPTH, D_MODEL, N_IN), D_MODEL ** -0.5),
        'a_conv_w': nrm((DEPTH, A_CONV_W, BRANCH_W), 0.5),
        'a_conv_b': nrm((DEPTH, BRANCH_W), 0.02),
        'a_gx_w': nrm((DEPTH, A_BLOCKS, A_BLOCK_W, A_BLOCK_W), A_BLOCK_W ** -0.5),
        'a_gx_b': nrm((DEPTH, BRANCH_W), 0.1),
        'a_ga_w': nrm((DEPTH, A_BLOCKS, A_BLOCK_W, A_BLOCK_W), A_BLOCK_W ** -0.5),
        'a_ga_b': nrm((DEPTH, BRANCH_W), 0.1),
        'a_lambda': a_lambda,
        'b_mu': jax.random.uniform(next(ks), (DEPTH, B_COLS), f32),
        'b_w0': nrm((DEPTH, BRANCH_W), 0.5) - 1.0,
        'b_w2': nrm((DEPTH, B_DECAY_LORA, BRANCH_W), 0.1),
        'b_a0': nrm((DEPTH, BRANCH_W), 0.1),
        'b_a2': nrm((DEPTH, B_AAA_LORA, BRANCH_W), 0.1),
        'b_g2': nrm((DEPTH, B_GATE_LORA, BRANCH_W), B_GATE_LORA ** -0.5),
        'b_k_k': 0.85 + nrm((DEPTH, BRANCH_W), 0.05),
        'b_k_a': 1.0 + nrm((DEPTH, BRANCH_W), 0.05),
        'b_r_k': nrm((DEPTH, B_HEADS, B_HEAD), 0.1),
        'b_ln_w': 1.0 + nrm((DEPTH, BRANCH_W), 0.05),
        'b_ln_b': nrm((DEPTH, BRANCH_W), 0.02),
        'c_lb': nrm((DEPTH, C_FDIM), 0.5),
        'c_norm_g': 1.0 + nrm((DEPTH, C_VDIM), 0.05),
        'w_branch': nrm((DEPTH, N_BRANCH, BRANCH_W, D_MODEL), BRANCH_W ** -0.5),
        'w_out': nrm((DEPTH, D_MODEL, D_MODEL), D_MODEL ** -0.5),
        'norm2_g': 1.0 + nrm((DEPTH, D_MODEL), 0.05),
        'ffn_w_gate': nrm((DEPTH, D_MODEL, D_FF), D_MODEL ** -0.5),
        'ffn_w_up': nrm((DEPTH, D_MODEL, D_FF), D_MODEL ** -0.5),
        'ffn_conv_w': nrm((DEPTH, FFN_CONV_W, D_FF), FFN_CONV_W ** -0.5),
        'ffn_conv_b': nrm((DEPTH, D_FF), 0.02),
        'ffn_w_down': nrm((DEPTH, D_FF, D_MODEL), D_FF ** -0.5),
        'final_norm_g': 1.0 + nrm((D_MODEL,), 0.05),
    }


def reference(x_prompt, x_sample, state_a_h, state_a_conv, state_b_wkv, state_b_shift, state_c_s,
              cache_d_k, cache_d_v, state_ffn_conv, norm1_g, w_in, a_conv_w, a_conv_b, a_gx_w, a_gx_b,
              a_ga_w, a_ga_b, a_lambda, b_mu, b_w0, b_w2, b_a0, b_a2, b_g2, b_k_k, b_k_a, b_r_k, b_ln_w,
              b_ln_b, c_lb, c_norm_g, w_branch, w_out, norm2_g, ffn_w_gate, ffn_w_up, ffn_conv_w,
              ffn_conv_b, ffn_w_down, final_norm_g):
    W = {'norm1_g': norm1_g, 'w_in': w_in, 'a_conv_w': a_conv_w, 'a_conv_b': a_conv_b, 'a_gx_w': a_gx_w,
         'a_gx_b': a_gx_b, 'a_ga_w': a_ga_w, 'a_ga_b': a_ga_b, 'a_lambda': a_lambda, 'b_mu': b_mu,
         'b_w0': b_w0, 'b_w2': b_w2, 'b_a0': b_a0, 'b_a2': b_a2, 'b_g2': b_g2, 'b_k_k': b_k_k,
         'b_k_a': b_k_a, 'b_r_k': b_r_k, 'b_ln_w': b_ln_w, 'b_ln_b': b_ln_b, 'c_lb': c_lb,
         'c_norm_g': c_norm_g, 'w_branch': w_branch, 'w_out': w_out, 'norm2_g': norm2_g,
         'ffn_w_gate': ffn_w_gate, 'ffn_w_up': ffn_w_up, 'ffn_conv_w': ffn_conv_w,
         'ffn_conv_b': ffn_conv_b, 'ffn_w_down': ffn_w_down, 'final_norm_g': final_norm_g}
    f32 = jnp.float32
    b_p, t_p = x_prompt.shape[:2]
    t_s = x_sample.shape[1]

    def zeros(*s):
        return jnp.zeros((DEPTH, b_p) + s, f32)
    (y_prompt, p_a_h, p_a_conv, p_b_wkv, p_b_shift, p_c_s, p_d_k, p_d_v, p_ffn_conv) = run_trunk(
        x_prompt, jnp.arange(t_p), zeros(BRANCH_W), zeros(A_CONV_W - 1, BRANCH_W),
        zeros(B_HEADS, B_HEAD, B_HEAD), zeros(B_COLS), zeros(C_HEADS, C_EXPAND, C_VDIM),
        None, None, zeros(FFN_CONV_W - 1, D_FF), W)
    (y_sample, s_a_h, s_a_conv, s_b_wkv, s_b_shift, s_c_s, s_d_k, s_d_v, s_ffn_conv) = run_trunk(
        x_sample, PAST_LEN + jnp.arange(t_s), state_a_h, state_a_conv, state_b_wkv, state_b_shift,
        state_c_s, cache_d_k, cache_d_v, state_ffn_conv, W)
    return (y_prompt, y_sample, p_a_h, p_a_conv, p_b_wkv, p_b_shift, p_c_s, p_d_k, p_d_v, p_ffn_conv,
            s_a_h, s_a_conv, s_b_wkv, s_b_shift, s_c_s, s_d_k, s_d_v, s_ffn_conv)
```

```python
import functools

import jax
import jax.numpy as jnp
from jax import lax
from jax.experimental import pallas as pl
from jax.experimental.pallas import tpu as pltpu

F32 = jnp.float32
BF16 = jnp.bfloat16
HI = lax.Precision.HIGHEST

D_MODEL = 1024
BRANCH_W = 256
N_HEADS = 4
HEAD_W = 64
HEAD_SHIFT = 6
C_EXPAND = 128
C_FDIM = 512
B_COLS = 1024
D_FF = 3072
NORM_EPS = 1e-6
A_C = 8.0
B_GN_EPS = 64e-5
ROPE_THETA = 10000.0
PAST_LEN = 8192
D_WIN_MAX = 2048
NEG_BIG = -1e30
N_MIX_COLS = 3840
ATT_BLK = 128

VMEM_LIMIT = 56 * 1024 * 1024


def _dot(a, b, prec=None):
    return jnp.dot(a, b, preferred_element_type=F32, precision=prec)


def _dot_nt(a, b, prec=None):
    return lax.dot_general(a, b, (((1,), (1,)), ((), ())), preferred_element_type=F32, precision=prec)


def _dot_tn(a, b, prec=None):
    return lax.dot_general(a, b, (((0,), (0,)), ((), ())), preferred_element_type=F32, precision=prec)


def _rms_rows(x, g):
    ms = jnp.mean(x * x, axis=-1, keepdims=True)
    return x * lax.rsqrt(ms + NORM_EPS) * g


def _round_up(n, m):
    return (n + m - 1) // m * m


def _params(sem):
    return pltpu.CompilerParams(dimension_semantics=sem, vmem_limit_bytes=VMEM_LIMIT)


def _resident(shape):
    nd = len(shape)
    return pl.BlockSpec(shape, lambda *_: (0,) * nd, pipeline_mode=pl.Buffered(1))


def _head_sum_matrix():
    r = lax.broadcasted_iota(jnp.int32, (BRANCH_W, BRANCH_W), 0) >> HEAD_SHIFT
    c = lax.broadcasted_iota(jnp.int32, (BRANCH_W, BRANCH_W), 1) >> HEAD_SHIFT
    return (r == c).astype(F32)


def _inproj_body(x_ref, g_ref, w_ref, o_ref, u_scr):
    @pl.when(pl.program_id(1) == 0)
    def _():
        u_scr[...] = _rms_rows(x_ref[...], g_ref[...]).astype(BF16)

    o_ref[...] = _dot(u_scr[...], w_ref[...])


def _inproj(x2d, g, w):
    n, d = x2d.shape
    nc = w.shape[1]
    tm = min(n, 1024)
    tn = 768
    return pl.pallas_call(
        _inproj_body,
        out_shape=jax.ShapeDtypeStruct((n, nc), F32),
        grid=(n // tm, nc // tn),
        in_specs=[pl.BlockSpec((tm, d), lambda i, j: (i, 0)),
                  pl.BlockSpec((1, d), lambda i, j: (0, 0)),
                  pl.BlockSpec((d, tn), lambda i, j: (0, j))],
        out_specs=pl.BlockSpec((tm, tn), lambda i, j: (i, j)),
        scratch_shapes=[pltpu.VMEM((tm, d), BF16)],
        compiler_params=_params(("parallel", "arbitrary")),
        name="inproj",
    )(x2d, g, w)


def _rglru_body(t_len, tp, axg_ref, h0_ref, cbuf_ref, cw_ref, cb_ref, wgx_ref, bgx_ref, wga_ref, bga_ref,
                lam_ref, y_ref, hl_ref, nb_ref, xs_scr, a_scr, b_scr):
    if tp != t_len:
        xs_scr[...] = jnp.zeros(xs_scr.shape, F32)
    xs_scr[pl.ds(5, 3), :] = jnp.concatenate([cbuf_ref[0], jnp.zeros((3, BRANCH_W), F32)], axis=1)
    xs_scr[pl.ds(8, t_len), :] = axg_ref[0]
    x0 = xs_scr[pl.ds(8, tp), 0:BRANCH_W]
    x1 = xs_scr[pl.ds(7, tp), 0:BRANCH_W]
    x2 = xs_scr[pl.ds(6, tp), 0:BRANCH_W]
    x3 = xs_scr[pl.ds(5, tp), 0:BRANCH_W]
    cw = cw_ref[...]
    xc = cb_ref[...] + cw[3:4] * x0 + cw[2:3] * x1 + cw[1:2] * x2 + cw[0:1] * x3
    gate_x = jax.nn.sigmoid(_dot(xc, wgx_ref[...], HI) + bgx_ref[...])
    gate_a = jax.nn.sigmoid(_dot(xc, wga_ref[...], HI) + bga_ref[...])
    log_a = -A_C * gate_a * jax.nn.softplus(-lam_ref[...])
    a = jnp.exp(log_a)
    th = jnp.tanh(log_a)
    b_in = jnp.sqrt(-2.0 * th / (1.0 - th)) * (gate_x * xc)
    if tp != t_len:
        row = lax.broadcasted_iota(jnp.int32, (tp, 1), 0)
        a = jnp.where(row < t_len, a, 1.0)
        b_in = jnp.where(row < t_len, b_in, 0.0)
    a_scr[...] = a
    b_scr[...] = b_in

    row8 = lax.broadcasted_iota(jnp.int32, (8, 1), 0)

    def group(g, carry):
        r0 = pl.multiple_of(g * 8, 8)
        ag = a_scr[pl.ds(r0, 8), :]
        bg = b_scr[pl.ds(r0, 8), :]
        for s in (1, 2, 4):
            a_sh = jnp.where(row8 >= s, pltpu.roll(ag, s, 0), 1.0)
            b_sh = jnp.where(row8 >= s, pltpu.roll(bg, s, 0), 0.0)
            bg = ag * b_sh + bg
            ag = ag * a_sh
        h = ag * carry + bg
        b_scr[pl.ds(r0, 8), :] = h
        return h[7:8, :]

    h_last = lax.fori_loop(0, tp // 8, group, h0_ref[0])
    h = b_scr[...]
    gate = xs_scr[pl.ds(8, tp), BRANCH_W:2 * BRANCH_W]
    y = h * jax.nn.gelu(gate)
    y_ref[0] = y[0:t_len]
    hl_ref[0] = h_last
    nb_ref[0] = xs_scr[pl.ds(8 + t_len - 3, 3), 0:BRANCH_W]


def _rglru(cols3, h0, cbuf, p):
    b, t, _ = cols3.shape
    tp = _round_up(t, 8)
    body = functools.partial(_rglru_body, t, tp)
    vec = pl.BlockSpec((1, BRANCH_W), lambda i: (0, 0))
    mat = pl.BlockSpec((BRANCH_W, BRANCH_W), lambda i: (0, 0))
    return pl.pallas_call(
        body,
        out_shape=(jax.ShapeDtypeStruct((b, t, BRANCH_W), F32),
                   jax.ShapeDtypeStruct((b, 1, BRANCH_W), F32),
                   jax.ShapeDtypeStruct((b, 3, BRANCH_W), F32)),
        grid=(b,),
        in_specs=[pl.BlockSpec((1, t, 2 * BRANCH_W), lambda i: (i, 0, 2)),
                  pl.BlockSpec((1, 1, BRANCH_W), lambda i: (i, 0, 0)),
                  pl.BlockSpec((1, 3, BRANCH_W), lambda i: (i, 0, 0)),
                  pl.BlockSpec((4, BRANCH_W), lambda i: (0, 0)),
                  vec, mat, vec, mat, vec, vec],
        out_specs=(pl.BlockSpec((1, t, BRANCH_W), lambda i: (i, 0, 0)),
                   pl.BlockSpec((1, 1, BRANCH_W), lambda i: (i, 0, 0)),
                   pl.BlockSpec((1, 3, BRANCH_W), lambda i: (i, 0, 0))),
        scratch_shapes=[pltpu.VMEM((tp + 8, 2 * BRANCH_W), F32),
                        pltpu.VMEM((tp, BRANCH_W), F32),
                        pltpu.VMEM((tp, BRANCH_W), F32)],
        compiler_params=_params(("parallel",)),
        name="rglru",
    )(cols3, h0, cbuf, p["a_conv_w"], p["a_conv_b"], p["a_gx_bd"], p["a_gx_b"], p["a_ga_bd"], p["a_ga_b"],
      p["a_lambda"])


def _rwkv_body(t_len, tc, n_t, ch, cb_ref, shift_ref, s0_ref, mu_ref, w0_ref, w2_ref, a0_ref, a2_ref, g2_ref,
               kk_ref, ka_ref, rk_ref, lnw_ref, lnb_ref, y_ref, last_ref, sn_ref,
               xs_scr, h_scr, r_scr, lw_scr, k_scr, v_scr, as_scr, bs_scr, y_scr):
    ti = pl.program_id(1)
    t_blk = min(t_len, tc)

    @pl.when(ti == 0)
    def _():
        if t_blk != tc:
            xs_scr[...] = jnp.zeros(xs_scr.shape, F32)
        xs_scr[pl.ds(7, 1), :] = shift_ref[0]
        for h in range(N_HEADS):
            h_scr[h] = s0_ref[0, h].T

    xs_scr[pl.ds(8, t_blk), :] = cb_ref[0]
    cb = xs_scr[pl.ds(8, tc), :]
    shifted = xs_scr[pl.ds(7, tc), :]
    cm = cb + (shifted - cb) * mu_ref[...]
    r = cm[:, 0:256]
    k = cm[:, 256:512]
    v = cm[:, 512:768]
    lora = cm[:, 768:896]
    w = -jax.nn.softplus(-(w0_ref[...] + _dot(jnp.tanh(lora), w2_ref[...], HI))) - 0.5
    log_w = -jnp.exp(w)
    a = jax.nn.sigmoid(a0_ref[...] + _dot(lora, a2_ref[...], HI))
    g = _dot(jax.nn.sigmoid(cm[:, 896:1024]), g2_ref[...], HI)
    e_mat = _head_sum_matrix()
    kk = k * kk_ref[...]
    kk = kk / jnp.maximum(jnp.sqrt(_dot(kk * kk, e_mat, HI)), 1e-12)
    k2 = k * (1.0 + (a - 1.0) * ka_ref[...])
    a_s = -kk
    b_s = kk * a
    if t_blk != tc:
        row = lax.broadcasted_iota(jnp.int32, (tc, 1), 0)
        live = row < t_blk
        log_w = jnp.where(live, log_w, 0.0)
        a_s = jnp.where(live, a_s, 0.0)
        b_s = jnp.where(live, b_s, 0.0)
        k2 = jnp.where(live, k2, 0.0)
    r_scr[...] = r
    lw_scr[...] = log_w
    k_scr[...] = k2
    v_scr[...] = v
    as_scr[...] = a_s
    bs_scr[...] = b_s

    ri = lax.broadcasted_iota(jnp.int32, (ch, ch), 0)
    ci = lax.broadcasted_iota(jnp.int32, (ch, ch), 1)
    tril_incl = (ci <= ri).astype(F32)
    strict = ci < ri
    incl = ci <= ri
    eye_c = (ci == ri).astype(F32)
    r64 = lax.broadcasted_iota(jnp.int32, (HEAD_W, HEAD_W), 0)
    c64 = lax.broadcasted_iota(jnp.int32, (HEAD_W, HEAD_W), 1)
    eye_k = r64 == c64
    n_dbl = ch.bit_length() - 1

    def chunk(c, _):
        r0 = pl.multiple_of(c * ch, ch)
        for h in range(N_HEADS):
            ls = slice(h * HEAD_W, (h + 1) * HEAD_W)
            rr = r_scr[pl.ds(r0, ch), ls]
            lw = lw_scr[pl.ds(r0, ch), ls]
            kc = k_scr[pl.ds(r0, ch), ls]
            vc = v_scr[pl.ds(r0, ch), ls]
            ac = as_scr[pl.ds(r0, ch), ls]
            bc = bs_scr[pl.ds(r0, ch), ls]
            cum = _dot(tril_incl, lw, HI)
            cum_last = cum[ch - 1:ch, :]
            gam = jnp.exp(cum)
            inv = jnp.exp(-cum)
            a_t = ac * jnp.exp(cum - lw)
            r_t = rr * gam
            b_t = bc * inv
            k_t = kc * inv
            tail = jnp.exp(cum_last - cum)
            b_h = bc * tail
            k_h = kc * tail
            l_ab = jnp.where(strict, _dot_nt(a_t, b_t, HI), 0.0)
            l_ak = jnp.where(strict, _dot_nt(a_t, k_t, HI), 0.0)
            m_rb = jnp.where(incl, _dot_nt(r_t, b_t, HI), 0.0)
            m_rk = jnp.where(incl, _dot_nt(r_t, k_t, HI), 0.0)
            tinv = eye_c + l_ab
            pw = l_ab
            for _i in range(1, n_dbl):
                pw = _dot(pw, pw, HI)
                tinv = tinv + _dot(tinv, pw, HI)
            a_hat = _dot(tinv, a_t, HI)
            u_hat = _dot(tinv, _dot(l_ak, vc, HI), HI)
            r_hat = r_t + _dot(m_rb, a_hat, HI)
            y_hat = _dot(m_rb, u_hat, HI) + _dot(m_rk, vc, HI)
            g_mat = jnp.where(eye_k, jnp.exp(cum_last), 0.0) + _dot_tn(b_h, a_hat, HI)
            h_hat = _dot_tn(b_h, u_hat, HI) + _dot_tn(k_h, vc, HI)
            hs = h_scr[h]
            y_scr[pl.ds(r0, ch), ls] = _dot(r_hat, hs, HI) + y_hat
            h_scr[h] = _dot(g_mat, hs, HI) + h_hat
        return 0

    lax.fori_loop(0, tc // ch, chunk, 0)

    y = y_scr[...]
    r = r_scr[...]
    k2 = k_scr[...]
    v = v_scr[...]
    mean = _dot(y, e_mat, HI) * (1.0 / HEAD_W)
    dev = y - mean
    var = _dot(dev * dev, e_mat, HI) * (1.0 / HEAD_W)
    yn = dev * lax.rsqrt(var + B_GN_EPS) * lnw_ref[...] + lnb_ref[...]
    bonus = _dot(r * k2 * rk_ref[...], e_mat, HI) * v
    out = (yn + bonus) * g
    y_ref[0] = out[0:t_blk]
    xs_scr[pl.ds(7, 1), :] = xs_scr[pl.ds(8 + t_blk - 1, 1), :]

    @pl.when(ti == n_t - 1)
    def _():
        last_ref[0] = xs_scr[pl.ds(7, 1), :]
        for h in range(N_HEADS):
            sn_ref[0, h] = h_scr[h].T


def _rwkv(cols3, shift, s0, p):
    b, t, _ = cols3.shape
    if t >= 64:
        ch, tc = 64, min(t, 512)
        assert t % tc == 0
    else:
        ch = 16
        tc = _round_up(t, ch)
    n_t = max(1, t // tc)
    t_blk = min(t, tc)
    body = functools.partial(_rwkv_body, t, tc, n_t, ch)
    vec = lambda w: pl.BlockSpec((1, w), lambda i, j: (0, 0))
    sc = lambda: pltpu.VMEM((tc, BRANCH_W), F32)
    return pl.pallas_call(
        body,
        out_shape=(jax.ShapeDtypeStruct((b, t, BRANCH_W), F32),
                   jax.ShapeDtypeStruct((b, 1, B_COLS), F32),
                   jax.ShapeDtypeStruct((b, N_HEADS, HEAD_W, HEAD_W), F32)),
        grid=(b, n_t),
        in_specs=[pl.BlockSpec((1, t_blk, B_COLS), lambda i, j: (i, j, 0)),
                  pl.BlockSpec((1, 1, B_COLS), lambda i, j: (i, 0, 0)),
                  pl.BlockSpec((1, N_HEADS, HEAD_W, HEAD_W), lambda i, j: (i, 0, 0, 0)),
                  vec(B_COLS), vec(BRANCH_W),
                  pl.BlockSpec((128, BRANCH_W), lambda i, j: (0, 0)),
                  vec(BRANCH_W),
                  pl.BlockSpec((128, BRANCH_W), lambda i, j: (0, 0)),
                  pl.BlockSpec((128, BRANCH_W), lambda i, j: (0, 0)),
                  vec(BRANCH_W), vec(BRANCH_W), vec(BRANCH_W), vec(BRANCH_W), vec(BRANCH_W)],
        out_specs=(pl.BlockSpec((1, t_blk, BRANCH_W), lambda i, j: (i, j, 0)),
                   pl.BlockSpec((1, 1, B_COLS), lambda i, j: (i, 0, 0)),
                   pl.BlockSpec((1, N_HEADS, HEAD_W, HEAD_W), lambda i, j: (i, 0, 0, 0))),
        scratch_shapes=[pltpu.VMEM((tc + 8, B_COLS), F32),
                        pltpu.VMEM((N_HEADS, HEAD_W, HEAD_W), F32),
                        sc(), sc(), sc(), sc(), sc(), sc(), sc()],
        compiler_params=_params(("parallel", "arbitrary")),
        name="rwkv7",
    )(cols3, shift, s0, p["b_mu"], p["b_w0"], p["b_w2p"], p["b_a0"], p["b_a2p"], p["b_g2"],
      p["b_k_k"], p["b_k_a"], p["b_r_k"], p["b_ln_w"], p["b_ln_b"])


GLA_CH = 16
GLA_SHIFT = 4


def _hgrn_body(t_len, tc, n_t, cq_ref, cf_ref, cig_ref, s0_ref, lb_ref, ng_ref, y_ref, sn_ref,
               in_scr, st_scr, cum_scr, kk_scr, y_scr):
    ti = pl.program_id(1)
    t_blk = min(t_len, tc)

    @pl.when(ti == 0)
    def _():
        if t_blk != tc:
            in_scr[...] = jnp.zeros(in_scr.shape, F32)
        for h in range(N_HEADS):
            st_scr[h] = s0_ref[0, h].T

    in_scr[0, pl.ds(0, t_blk), :] = cq_ref[0]
    in_scr[1, pl.ds(0, t_blk), :] = cf_ref[0]
    in_scr[2, pl.ds(0, t_blk), :] = cig_ref[0]
    lb = lb_ref[...]
    fg = lb + (1.0 - lb) * jax.nn.sigmoid(in_scr[1])
    log_f = jnp.log(fg)
    kk = 1.0 - fg
    if t_blk != tc:
        row = lax.broadcasted_iota(jnp.int32, (tc, 1), 0)
        live = row < t_blk
        log_f = jnp.where(live, log_f, 0.0)
        kk = jnp.where(live, kk, 0.0)
    ri = lax.broadcasted_iota(jnp.int32, (tc, tc), 0)
    ci = lax.broadcasted_iota(jnp.int32, (tc, tc), 1)
    blk_tril = (((ri >> GLA_SHIFT) == (ci >> GLA_SHIFT)) & (ci <= ri)).astype(F32)
    cum_scr[...] = _dot(blk_tril, log_f, HI)
    kk_scr[...] = kk
    srow = lax.broadcasted_iota(jnp.int32, (GLA_CH, 1), 0)

    def chunk(c, _):
        r0 = pl.multiple_of(c * GLA_CH, GLA_CH)
        for h in range(N_HEADS):
            kl = slice(h * C_EXPAND, (h + 1) * C_EXPAND)
            vl = slice(h * HEAD_W, (h + 1) * HEAD_W)
            qh = in_scr[0, pl.ds(r0, GLA_CH), kl]
            kh = kk_scr[pl.ds(r0, GLA_CH), kl]
            cc = cum_scr[pl.ds(r0, GLA_CH), kl]
            vh = in_scr[2, pl.ds(r0, GLA_CH), vl]
            st = st_scr[h]
            o = _dot_nt(qh * jnp.exp(cc), st, HI)
            for t in range(GLA_CH):
                diff = cc[t:t + 1, :] - cc
                e = jnp.exp(jnp.where(srow <= t, diff, NEG_BIG))
                att = jnp.sum(e * kh * qh[t:t + 1, :], axis=-1, keepdims=True)
                o_t = jnp.sum(att * vh, axis=0, keepdims=True)
                o = o + jnp.where(srow == t, o_t, 0.0)
            o = o * lax.rsqrt(jnp.mean(o * o, axis=-1, keepdims=True) + NORM_EPS) * ng_ref[...]
            y_scr[pl.ds(r0, GLA_CH), vl] = o
            c_last = cc[GLA_CH - 1:GLA_CH, :]
            st_scr[h] = st * jnp.exp(c_last) + _dot_tn(vh, kh * jnp.exp(c_last - cc), HI)
        return 0

    lax.fori_loop(0, tc // GLA_CH, chunk, 0)
    y = y_scr[...] * jax.nn.silu(in_scr[2, :, BRANCH_W:2 * BRANCH_W])
    y_ref[0] = y[0:t_blk]

    @pl.when(ti == n_t - 1)
    def _():
        for h in range(N_HEADS):
            sn_ref[0, h] = st_scr[h].T


def _hgrn(cols3, s0, lb, ng):
    b, t, _ = cols3.shape
    tc = min(_round_up(t, GLA_CH), 256)
    n_t = max(1, t // tc)
    t_blk = min(t, tc)
    assert t_blk * n_t == t
    body = functools.partial(_hgrn_body, t, tc, n_t)
    blk = lambda idx: pl.BlockSpec((1, t_blk, C_FDIM), lambda i, j: (i, j, idx))
    return pl.pallas_call(
        body,
        out_shape=(jax.ShapeDtypeStruct((b, t, BRANCH_W), F32),
                   jax.ShapeDtypeStruct((b, N_HEADS, C_EXPAND, HEAD_W), F32)),
        grid=(b, n_t),
        in_specs=[blk(3), blk(4), blk(5),
                  pl.BlockSpec((1, N_HEADS, C_EXPAND, HEAD_W), lambda i, j: (i, 0, 0, 0)),
                  pl.BlockSpec((1, C_FDIM), lambda i, j: (0, 0)),
                  pl.BlockSpec((1, HEAD_W), lambda i, j: (0, 0))],
        out_specs=(pl.BlockSpec((1, t_blk, BRANCH_W), lambda i, j: (i, j, 0)),
                   pl.BlockSpec((1, N_HEADS, C_EXPAND, HEAD_W), lambda i, j: (i, 0, 0, 0))),
        scratch_shapes=[pltpu.VMEM((3, tc, C_FDIM), F32),
                        pltpu.VMEM((N_HEADS, HEAD_W, C_EXPAND), F32),
                        pltpu.VMEM((tc, C_FDIM), F32),
                        pltpu.VMEM((tc, C_FDIM), F32),
                        pltpu.VMEM((tc, BRANCH_W), F32)],
        compiler_params=_params(("parallel", "arbitrary")),
        name="hgrn2",
    )(cols3, cols3, cols3, s0, lb, ng)


def _rope_rows(x, cos, sin_signed):
    lane = lax.broadcasted_iota(jnp.int32, x.shape, 1)
    swapped = jnp.where((lane & (HEAD_W - 1)) < HEAD_W // 2,
                        pltpu.roll(x, BRANCH_W - HEAD_W // 2, 1), pltpu.roll(x, HEAD_W // 2, 1))
    return x * cos + swapped * sin_signed


def _pattern_weight(d):
    ok = d >= 0
    w = ((d <= 128).astype(F32) + (((d & 3) == 0) & (d <= 512)).astype(F32)
         + (((d & 15) == 0) & (d <= 2048)).astype(F32))
    return jnp.where(ok, w, 0.0)


def _attn_prompt_body(n_blk, qkv_ref, cos_ref, sin_ref, y_ref, k_ref, q_scr, kb_scr, vh_scr, s_scr):
    t = n_blk * ATT_BLK
    qkv = qkv_ref[0]
    cos = cos_ref[...]
    sin = sin_ref[...]
    q = _rope_rows(qkv[:, 0:BRANCH_W], cos, sin) * (HEAD_W ** -0.5)
    k = _rope_rows(qkv[:, BRANCH_W:2 * BRANCH_W], cos, sin)
    k_ref[0] = k
    q_scr[...] = q
    kb_scr[...] = k.astype(BF16)
    v = qkv[:, 2 * BRANCH_W:3 * BRANCH_W]
    lane_head = lax.broadcasted_iota(jnp.int32, (ATT_BLK, BRANCH_W), 1) >> HEAD_SHIFT
    for j in range(n_blk):
        vj = v[j * ATT_BLK:(j + 1) * ATT_BLK, :]
        for h in range(N_HEADS):
            vh_scr[j, pl.ds(h * ATT_BLK, ATT_BLK), :] = jnp.where(lane_head == h, vj, 0.0).astype(BF16)

    rr = lax.broadcasted_iota(jnp.int32, (ATT_BLK, ATT_BLK), 0)
    cc = lax.broadcasted_iota(jnp.int32, (ATT_BLK, ATT_BLK), 1)
    rel = rr - cc

    def q_block(i, _):
        q0 = pl.multiple_of(i * ATT_BLK, ATT_BLK)
        qi = q_scr[pl.ds(q0, ATT_BLK), :]
        q_bd = jnp.concatenate([jnp.where(lane_head == h, qi, 0.0) for h in range(N_HEADS)], axis=0).astype(BF16)

        def weights(j):
            w1 = _pattern_weight((i - j) * ATT_BLK + rel)
            return jnp.concatenate([w1] * N_HEADS, axis=0)

        def pass1(j, m_run):
            k0 = pl.multiple_of(j * ATT_BLK, ATT_BLK)
            s = _dot_nt(q_bd, kb_scr[pl.ds(k0, ATT_BLK), :])
            s = jnp.where(weights(j) > 0.0, s, NEG_BIG)
            s_scr[j] = s
            return jnp.maximum(m_run, s)

        m_run = lax.fori_loop(0, i + 1, pass1, jnp.full((N_HEADS * ATT_BLK, ATT_BLK), NEG_BIG, F32))
        m = jnp.max(m_run, axis=-1, keepdims=True)

        def pass2(j, carry):
            l_run, acc = carry
            p = weights(j) * jnp.exp(s_scr[j] - m)
            p_cat = jnp.concatenate([p[h * ATT_BLK:(h + 1) * ATT_BLK, :] for h in range(N_HEADS)], axis=1)
            acc = acc + _dot(p_cat.astype(BF16), vh_scr[j])
            return l_run + p, acc

        l_run, acc = lax.fori_loop(0, i + 1, pass2, (jnp.zeros((N_HEADS * ATT_BLK, ATT_BLK), F32),
                                                     jnp.zeros((ATT_BLK, BRANCH_W), F32)))
        l = jnp.sum(l_run, axis=-1, keepdims=True)
        den = jnp.zeros((ATT_BLK, BRANCH_W), F32)
        for h in range(N_HEADS):
            den = jnp.where(lane_head == h, l[h * ATT_BLK:(h + 1) * ATT_BLK, :], den)
        y_ref[0, pl.ds(q0, ATT_BLK), :] = acc / den
        return 0

    lax.fori_loop(0, n_blk, q_block, 0)


def _attn_prompt(cols3, cos, sin):
    b, t, _ = cols3.shape
    n_blk = t // ATT_BLK
    assert n_blk * ATT_BLK == t
    body = functools.partial(_attn_prompt_body, n_blk)
    return pl.pallas_call(
        body,
        out_shape=(jax.ShapeDtypeStruct((b, t, BRANCH_W), F32), jax.ShapeDtypeStruct((b, t, BRANCH_W), F32)),
        grid=(b,),
        in_specs=[pl.BlockSpec((1, t, 3 * BRANCH_W), lambda i: (i, 0, 4)),
                  _resident((t, BRANCH_W)), _resident((t, BRANCH_W))],
        out_specs=(pl.BlockSpec((1, t, BRANCH_W), lambda i: (i, 0, 0)),
                   pl.BlockSpec((1, t, BRANCH_W), lambda i: (i, 0, 0))),
        scratch_shapes=[pltpu.VMEM((t, BRANCH_W), F32),
                        pltpu.VMEM((t, BRANCH_W), BF16),
                        pltpu.VMEM((n_blk, N_HEADS * ATT_BLK, BRANCH_W), BF16),
                        pltpu.VMEM((n_blk, N_HEADS * ATT_BLK, ATT_BLK), F32)],
        compiler_params=_params(("parallel",)),
        name="attn_prompt",
    )(cols3, cos, sin)


def _attn_sample_body(t_len, l_cache, qkv_ref, kc_ref, vc_ref, cos_ref, sin_ref, y_ref, k_ref, pad_scr):
    tp = 8
    pad_scr[...] = jnp.zeros(pad_scr.shape, F32)
    pad_scr[pl.ds(0, t_len), :] = qkv_ref[0]
    qkv = pad_scr[...]
    cos = cos_ref[...]
    sin = sin_ref[...]
    q = _rope_rows(qkv[:, 0:BRANCH_W], cos, sin) * (HEAD_W ** -0.5)
    k_new = _rope_rows(qkv[:, BRANCH_W:2 * BRANCH_W], cos, sin)
    v_new = qkv[:, 2 * BRANCH_W:3 * BRANCH_W]
    k_ref[0] = k_new[0:t_len]
    lane_head = lax.broadcasted_iota(jnp.int32, (tp, BRANCH_W), 1) >> HEAD_SHIFT
    q_bd = jnp.concatenate([jnp.where(lane_head == h, q, 0.0) for h in range(N_HEADS)], axis=0).astype(BF16)
    n_row = N_HEADS * tp
    s_c = _dot_nt(q_bd, kc_ref[0].astype(BF16))
    s_n = _dot_nt(q_bd, k_new.astype(BF16))
    t_row = lax.broadcasted_iota(jnp.int32, (n_row, 1), 0) & (tp - 1)
    col_c = lax.broadcasted_iota(jnp.int32, (n_row, l_cache), 1)
    w_c = _pattern_weight(l_cache + t_row - col_c)
    col_n = lax.broadcasted_iota(jnp.int32, (n_row, tp), 1)
    w_n = jnp.where(col_n < t_len, _pattern_weight(t_row - col_n), 0.0)
    s_c = jnp.where(w_c > 0.0, s_c, NEG_BIG)
    s_n = jnp.where(w_n > 0.0, s_n, NEG_BIG)
    m = jnp.maximum(jnp.max(s_c, axis=-1, keepdims=True), jnp.max(s_n, axis=-1, keepdims=True))
    p_c = w_c * jnp.exp(s_c - m)
    p_n = w_n * jnp.exp(s_n - m)
    l = jnp.sum(p_c, axis=-1, keepdims=True) + jnp.sum(p_n, axis=-1, keepdims=True)
    num = _dot(p_c.astype(BF16), vc_ref[0].astype(BF16)) + _dot(p_n.astype(BF16), v_new.astype(BF16))
    ratio = num / l
    out = jnp.zeros((tp, BRANCH_W), F32)
    for h in range(N_HEADS):
        out = jnp.where(lane_head == h, ratio[h * tp:(h + 1) * tp, :], out)
    y_ref[0] = out[0:t_len]


def _attn_sample(cols3, k_cache, v_cache, cos, sin):
    b, t, _ = cols3.shape
    l_cache = k_cache.shape[1]
    assert t <= 8
    body = functools.partial(_attn_sample_body, t, l_cache)
    return pl.pallas_call(
        body,
        out_shape=(jax.ShapeDtypeStruct((b, t, BRANCH_W), F32), jax.ShapeDtypeStruct((b, t, BRANCH_W), F32)),
        grid=(b,),
        in_specs=[pl.BlockSpec((1, t, 3 * BRANCH_W), lambda i: (i, 0, 4)),
                  pl.BlockSpec((1, l_cache, BRANCH_W), lambda i: (i, 0, 0)),
                  pl.BlockSpec((1, l_cache, BRANCH_W), lambda i: (i, 0, 0)),
                  pl.BlockSpec((8, BRANCH_W), lambda i: (0, 0)),
                  pl.BlockSpec((8, BRANCH_W), lambda i: (0, 0))],
        out_specs=(pl.BlockSpec((1, t, BRANCH_W), lambda i: (i, 0, 0)),
                   pl.BlockSpec((1, t, BRANCH_W), lambda i: (i, 0, 0))),
        scratch_shapes=[pltpu.VMEM((8, 3 * BRANCH_W), F32)],
        compiler_params=_params(("parallel",)),
        name="attn_sample",
    )(cols3, k_cache, v_cache, cos, sin)


def _merge_body(x_ref, g_ref, wgt_ref, ya_ref, yb_ref, yc_ref, yd_ref, wb_ref, wo_ref, o_ref):
    x = x_ref[...]
    u = _rms_rows(x, g_ref[...]).astype(BF16)
    merged = None
    for n, y_ref in enumerate((ya_ref, yb_ref, yc_ref, yd_ref)):
        z = _dot(y_ref[...].astype(BF16), wb_ref[n])
        gate = jax.nn.sigmoid(_dot(u, wgt_ref[:, n * D_MODEL:(n + 1) * D_MODEL]))
        merged = gate * z if merged is None else merged + gate * z
    o_ref[...] = x + _dot(merged.astype(BF16), wo_ref[...])


def _merge(x2d, g, wgt, ys, wb, wo):
    n, d = x2d.shape
    tm = min(n, 512)
    row = lambda w: pl.BlockSpec((tm, w), lambda i: (i, 0))
    return pl.pallas_call(
        _merge_body,
        out_shape=jax.ShapeDtypeStruct((n, d), F32),
        grid=(n // tm,),
        in_specs=[row(d), pl.BlockSpec((1, d), lambda i: (0, 0)), _resident(wgt.shape),
                  row(BRANCH_W), row(BRANCH_W), row(BRANCH_W), row(BRANCH_W),
                  _resident(wb.shape), _resident(wo.shape)],
        out_specs=row(d),
        compiler_params=_params(("parallel",)),
        name="merge",
    )(x2d, g, wgt, *ys, wb, wo)


FFN_COL = 1024


def _ffn_body(t_len, tm, n_t, x_ref, g_ref, wg_ref, wu_ref, wd_ref, cw_ref, cb_ref, st_ref, o_ref, ns_ref,
              pad_scr, carry_scr):
    ti = pl.program_id(1)
    t_blk = min(t_len, tm)

    @pl.when(ti == 0)
    def _():
        carry_scr[...] = st_ref[0]

    if t_blk != tm:
        pad_scr[...] = jnp.zeros(pad_scr.shape, F32)
        pad_scr[pl.ds(0, t_blk), :] = x_ref[0]
        x = pad_scr[...]
    else:
        x = x_ref[0]
    v = _rms_rows(x, g_ref[...]).astype(BF16)
    row = lax.broadcasted_iota(jnp.int32, (tm, 1), 0)
    acc = jnp.zeros((tm, D_MODEL), F32)
    cw = cw_ref[...]
    for c in range(D_FF // FFN_COL):
        sl = slice(c * FFN_COL, (c + 1) * FFN_COL)
        hg = _dot(v, wg_ref[:, sl])
        hu = _dot(v, wu_ref[:, sl])
        p2 = carry_scr[0:1, sl]
        p1 = carry_scr[1:2, sl]
        h1 = jnp.where(row == 0, p1, pltpu.roll(hg, 1, 0))
        h2 = jnp.where(row == 0, p2, jnp.where(row == 1, p1, pltpu.roll(hg, 2, 0)))
        conv = cb_ref[:, sl] + cw[2:3, sl] * hg + cw[1:2, sl] * h1 + cw[0:1, sl] * h2
        hmid = (jax.nn.gelu(conv) * hu).astype(BF16)
        acc = acc + _dot(hmid, wd_ref[sl, :])
        carry_scr[:, sl] = hg[t_blk - 2:t_blk, :]
    y = x + acc
    o_ref[0] = y[0:t_blk]

    @pl.when(ti == n_t - 1)
    def _():
        ns_ref[0] = carry_scr[...]


def _ffn(x3, g, wg, wu, wd, cw, cb, state):
    b, t, d = x3.shape
    tm = min(_round_up(t, 8), 512)
    n_t = max(1, t // tm)
    t_blk = min(t, tm)
    assert t_blk * n_t == t and t_blk >= 2
    body = functools.partial(_ffn_body, t, tm, n_t)
    return pl.pallas_call(
        body,
        out_shape=(jax.ShapeDtypeStruct((b, t, d), F32), jax.ShapeDtypeStruct((b, 2, D_FF), F32)),
        grid=(b, n_t),
        in_specs=[pl.BlockSpec((1, t_blk, d), lambda i, j: (i, j, 0)),
                  pl.BlockSpec((1, d), lambda i, j: (0, 0)),
                  _resident(wg.shape), _resident(wu.shape), _resident(wd.shape),
                  pl.BlockSpec((3, D_FF), lambda i, j: (0, 0)),
                  pl.BlockSpec((1, D_FF), lambda i, j: (0, 0)),
                  pl.BlockSpec((1, 2, D_FF), lambda i, j: (i, 0, 0))],
        out_specs=(pl.BlockSpec((1, t_blk, d), lambda i, j: (i, j, 0)),
                   pl.BlockSpec((1, 2, D_FF), lambda i, j: (i, 0, 0))),
        scratch_shapes=[pltpu.VMEM((tm, d), F32), pltpu.VMEM((2, D_FF), F32)],
        compiler_params=_params(("parallel", "arbitrary")),
        name="convffn",
    )(x3, g, wg, wu, wd, cw, cb, state)


def _norm_body(x_ref, g_ref, o_ref):
    o_ref[...] = _rms_rows(x_ref[...], g_ref[...])


def _final_norm(x2d, g):
    n, d = x2d.shape
    tm = min(n, 1024)
    return pl.pallas_call(
        _norm_body,
        out_shape=jax.ShapeDtypeStruct((n, d), F32),
        grid=(n // tm,),
        in_specs=[pl.BlockSpec((tm, d), lambda i: (i, 0)), pl.BlockSpec((1, d), lambda i: (0, 0))],
        out_specs=pl.BlockSpec((tm, d), lambda i: (i, 0)),
        compiler_params=_params(("parallel",)),
        name="final_norm",
    )(x2d, g)


def _block_diag(w):
    out = jnp.zeros((BRANCH_W, BRANCH_W), F32)
    for h in range(N_HEADS):
        out = out.at[h * HEAD_W:(h + 1) * HEAD_W, h * HEAD_W:(h + 1) * HEAD_W].set(w[h])
    return out


def _rope_tables(pos, rows):
    half = HEAD_W // 2
    inv = ROPE_THETA ** (-jnp.arange(half, dtype=F32) / half)
    ang = pos.astype(F32)[:, None] * inv[None, :]
    cos = jnp.cos(ang)
    sin = jnp.sin(ang)
    cos_t = jnp.tile(jnp.concatenate([cos, cos], axis=1), (1, N_HEADS))
    sin_t = jnp.tile(jnp.concatenate([-sin, sin], axis=1), (1, N_HEADS))
    pad = rows - pos.shape[0]
    if pad:
        cos_t = jnp.pad(cos_t, ((0, pad), (0, 0)))
        sin_t = jnp.pad(sin_t, ((0, pad), (0, 0)))
    return cos_t, sin_t


def _layer_params(l, W):
    w_in = W["w_in"][l]
    o = 0
    parts = {}
    for name, width in (("ax", 256), ("ag", 256), ("cb", 1024), ("cq", 512), ("cf", 512), ("ci", 256),
                        ("cg", 256), ("dq", 256), ("dk", 256), ("dv", 256), ("gt", 4096)):
        parts[name] = w_in[:, o:o + width]
        o += width
    order = ("cb", "ax", "ag", "cq", "cf", "ci", "cg", "dq", "dk", "dv")
    p = {
        "w_mix": jnp.concatenate([parts[n] for n in order], axis=1).astype(BF16),
        "w_gt": parts["gt"].astype(BF16),
        "norm1_g": W["norm1_g"][l][None, :],
        "a_conv_w": W["a_conv_w"][l], "a_conv_b": W["a_conv_b"][l][None, :],
        "a_gx_bd": _block_diag(W["a_gx_w"][l]), "a_gx_b": W["a_gx_b"][l][None, :],
        "a_ga_bd": _block_diag(W["a_ga_w"][l]), "a_ga_b": W["a_ga_b"][l][None, :],
        "a_lambda": W["a_lambda"][l][None, :],
        "b_mu": W["b_mu"][l][None, :], "b_w0": W["b_w0"][l][None, :], "b_a0": W["b_a0"][l][None, :],
        "b_w2p": jnp.concatenate([W["b_w2"][l], jnp.zeros((64, BRANCH_W), F32)], axis=0),
        "b_a2p": jnp.concatenate([jnp.zeros((64, BRANCH_W), F32), W["b_a2"][l]], axis=0),
        "b_g2": W["b_g2"][l],
        "b_k_k": W["b_k_k"][l][None, :], "b_k_a": W["b_k_a"][l][None, :],
        "b_r_k": W["b_r_k"][l].reshape(1, BRANCH_W),
        "b_ln_w": W["b_ln_w"][l][None, :], "b_ln_b": W["b_ln_b"][l][None, :],
        "c_norm_g": W["c_norm_g"][l][None, :],
        "w_branch": W["w_branch"][l].astype(BF16), "w_out": W["w_out"][l].astype(BF16),
        "norm2_g": W["norm2_g"][l][None, :],
        "ffn_w_gate": W["ffn_w_gate"][l].astype(BF16), "ffn_w_up": W["ffn_w_up"][l].astype(BF16),
        "ffn_w_down": W["ffn_w_down"][l].astype(BF16),
        "ffn_conv_w": W["ffn_conv_w"][l], "ffn_conv_b": W["ffn_conv_b"][l][None, :],
    }
    return p


def _trunk(x, pos0, st, W, params, lb_all):
    b, t, d = x.shape
    depth = W["w_in"].shape[0]
    has_cache = st["k"] is not None
    if has_cache:
        cos, sin = _rope_tables(pos0 + jnp.arange(t), 8)
    else:
        cos, sin = _rope_tables(pos0 + jnp.arange(t), t)
    outs = {n: [] for n in ("ha", "ca", "wkv", "sh", "sc", "k", "v", "cf")}
    for l in range(depth):
        p = params[l]
        cols = _inproj(x.reshape(b * t, d), p["norm1_g"], p["w_mix"]).reshape(b, t, N_MIX_COLS)
        y_a, h_new, ca_new = _rglru(cols, st["ha"][l][:, None, :], st["ca"][l], p)
        y_b, sh_new, wkv_new = _rwkv(cols, st["sh"][l][:, None, :], st["wkv"][l], p)
        y_c, sc_new = _hgrn(cols, st["sc"][l], lb_all[l][None, :], p["c_norm_g"])
        if has_cache:
            lc = st["k"].shape[2]
            y_d, k_rows = _attn_sample(cols, st["k"][l].reshape(b, lc, BRANCH_W),
                                       st["v"][l].reshape(b, lc, BRANCH_W), cos, sin)
        else:
            y_d, k_rows = _attn_prompt(cols, cos, sin)
        v_rows = cols[:, :, N_MIX_COLS - BRANCH_W:]
        ys = [y.reshape(b * t, BRANCH_W) for y in (y_a, y_b, y_c, y_d)]
        x1 = _merge(x.reshape(b * t, d), p["norm1_g"], p["w_gt"], ys, p["w_branch"], p["w_out"])
        x, cf_new = _ffn(x1.reshape(b, t, d), p["norm2_g"], p["ffn_w_gate"], p["ffn_w_up"], p["ffn_w_down"],
                         p["ffn_conv_w"], p["ffn_conv_b"], st["cf"][l])
        outs["ha"].append(h_new[:, 0, :])
        outs["ca"].append(ca_new)
        outs["wkv"].append(wkv_new)
        outs["sh"].append(sh_new[:, 0, :])
        outs["sc"].append(sc_new)
        keep = t if has_cache else min(D_WIN_MAX, t)
        outs["k"].append(k_rows[:, t - keep:].reshape(b, keep, N_HEADS, HEAD_W))
        outs["v"].append(v_rows[:, t - keep:].reshape(b, keep, N_HEADS, HEAD_W))
        outs["cf"].append(cf_new)
    y = _final_norm(x.reshape(b * t, d), W["final_norm_g"][None, :]).reshape(b, t, d)
    return (y,) + tuple(jnp.stack(outs[n]) for n in ("ha", "ca", "wkv", "sh", "sc", "k", "v", "cf"))


def kernel(x_prompt, x_sample, state_a_h, state_a_conv, state_b_wkv, state_b_shift, state_c_s, cache_d_k, cache_d_v, state_ffn_conv, norm1_g, w_in, a_conv_w, a_conv_b, a_gx_w, a_gx_b, a_ga_w, a_ga_b, a_lambda, b_mu, b_w0, b_w2, b_a0, b_a2, b_g2, b_k_k, b_k_a, b_r_k, b_ln_w, b_ln_b, c_lb, c_norm_g, w_branch, w_out, norm2_g, ffn_w_gate, ffn_w_up, ffn_conv_w, ffn_conv_b, ffn_w_down, final_norm_g):
    W = {"norm1_g": norm1_g, "w_in": w_in, "a_conv_w": a_conv_w, "a_conv_b": a_conv_b, "a_gx_w": a_gx_w,
         "a_gx_b": a_gx_b, "a_ga_w": a_ga_w, "a_ga_b": a_ga_b, "a_lambda": a_lambda, "b_mu": b_mu,
         "b_w0": b_w0, "b_w2": b_w2, "b_a0": b_a0, "b_a2": b_a2, "b_g2": b_g2, "b_k_k": b_k_k,
         "b_k_a": b_k_a, "b_r_k": b_r_k, "b_ln_w": b_ln_w, "b_ln_b": b_ln_b, "c_lb": c_lb,
         "c_norm_g": c_norm_g, "w_branch": w_branch, "w_out": w_out, "norm2_g": norm2_g,
         "ffn_w_gate": ffn_w_gate, "ffn_w_up": ffn_w_up, "ffn_conv_w": ffn_conv_w,
         "ffn_conv_b": ffn_conv_b, "ffn_w_down": ffn_w_down, "final_norm_g": final_norm_g}
    depth = w_in.shape[0]
    params = [_layer_params(l, W) for l in range(depth)]
    lb_sm = jax.nn.softmax(c_lb.astype(F32), axis=0)
    lb_all = jnp.cumsum(lb_sm, axis=0) - lb_sm[0]
    b_p, t_p = x_prompt.shape[:2]

    def zeros(*s):
        return jnp.zeros((depth, b_p) + s, F32)

    st_p = {"ha": zeros(BRANCH_W), "ca": zeros(3, BRANCH_W), "wkv": zeros(N_HEADS, HEAD_W, HEAD_W),
            "sh": zeros(B_COLS), "sc": zeros(N_HEADS, C_EXPAND, HEAD_W), "k": None, "v": None,
            "cf": zeros(2, D_FF)}
    out_p = _trunk(x_prompt, 0, st_p, W, params, lb_all)
    st_s = {"ha": state_a_h, "ca": state_a_conv, "wkv": state_b_wkv, "sh": state_b_shift, "sc": state_c_s,
            "k": cache_d_k, "v": cache_d_v, "cf": state_ffn_conv}
    out_s = _trunk(x_sample, PAST_LEN, st_s, W, params, lb_all)
    return (out_p[0], out_s[0]) + out_p[1:] + out_s[1:]
```

```python
import functools

import numpy as np

import jax
import jax.numpy as jnp
from jax import lax
from jax.experimental import pallas as pl
from jax.experimental.pallas import tpu as pltpu

F32 = jnp.float32
BF16 = jnp.bfloat16
HI = lax.Precision.HIGHEST

D_MODEL = 1024
BRANCH_W = 256
N_HEADS = 4
HEAD_W = 64
HEAD_SHIFT = 6
C_EXPAND = 128
C_EXPAND_SHIFT = 7
C_FDIM = 512
B_COLS = 1024
D_FF = 3072
NORM_EPS = 1e-6
A_C = 8.0
B_GN_EPS = 64e-5
ROPE_THETA = 10000.0
PAST_LEN = 8192
D_WIN_MAX = 2048
NEG_BIG = -1e30
N_MIX_COLS = 3840
ATT_BLK = 128

VMEM_LIMIT = 56 * 1024 * 1024


def _dot(a, b, prec=None):
    return jnp.dot(a, b, preferred_element_type=F32, precision=prec)


def _dot_nt(a, b, prec=None):
    return lax.dot_general(a, b, (((1,), (1,)), ((), ())), preferred_element_type=F32, precision=prec)


def _dot_tn(a, b, prec=None):
    return lax.dot_general(a, b, (((0,), (0,)), ((), ())), preferred_element_type=F32, precision=prec)


def _rms_rows(x, g):
    ms = jnp.mean(x * x, axis=-1, keepdims=True)
    return x * lax.rsqrt(ms + NORM_EPS) * g


def _round_up(n, m):
    return (n + m - 1) // m * m


def _params(sem):
    return pltpu.CompilerParams(dimension_semantics=sem, vmem_limit_bytes=VMEM_LIMIT)


def _resident(shape):
    nd = len(shape)
    return pl.BlockSpec(shape, lambda *_: (0,) * nd, pipeline_mode=pl.Buffered(1))


def _split3(x):
    hi = x.astype(BF16)
    r1 = x - hi.astype(F32)
    mid = r1.astype(BF16)
    lo = (r1 - mid.astype(F32)).astype(BF16)
    return hi, mid, lo


def _head_sum(x, e3):
    return _dot(jnp.concatenate(_split3(x), axis=1), e3)


def _stack_heads(x, lane_head):
    return jnp.concatenate([jnp.where(lane_head == h, x, jnp.zeros_like(x)) for h in range(N_HEADS)], axis=0)


def _inproj_body(x_ref, g_ref, w_ref, o_ref, u_scr):
    @pl.when(pl.program_id(1) == 0)
    def _():
        u_scr[...] = _rms_rows(x_ref[...], g_ref[...]).astype(BF16)

    o_ref[...] = _dot(u_scr[...], w_ref[...])


def _inproj(x2d, g, w):
    n, d = x2d.shape
    nc = w.shape[1]
    tm = min(n, 1024)
    tn = 768
    return pl.pallas_call(
        _inproj_body,
        out_shape=jax.ShapeDtypeStruct((n, nc), F32),
        grid=(n // tm, nc // tn),
        in_specs=[pl.BlockSpec((tm, d), lambda i, j: (i, 0)),
                  pl.BlockSpec((1, d), lambda i, j: (0, 0)),
                  pl.BlockSpec((d, tn), lambda i, j: (0, j))],
        out_specs=pl.BlockSpec((tm, tn), lambda i, j: (i, j)),
        scratch_shapes=[pltpu.VMEM((tm, d), BF16)],
        compiler_params=_params(("parallel", "arbitrary")),
        name="inproj",
    )(x2d, g, w)


def _rglru_body(t_len, tp, axg_ref, h0_ref, cbuf_ref, cw_ref, cb_ref, wgx_ref, bgx_ref, wga_ref, bga_ref,
                lam_ref, y_ref, hl_ref, nb_ref, xs_scr, a_scr, b_scr):
    if tp != t_len:
        xs_scr[...] = jnp.zeros(xs_scr.shape, F32)
    xs_scr[pl.ds(5, 3), :] = jnp.concatenate([cbuf_ref[0], jnp.zeros((3, BRANCH_W), F32)], axis=1)
    xs_scr[pl.ds(8, t_len), :] = axg_ref[0]
    x0 = xs_scr[pl.ds(8, tp), 0:BRANCH_W]
    x1 = xs_scr[pl.ds(7, tp), 0:BRANCH_W]
    x2 = xs_scr[pl.ds(6, tp), 0:BRANCH_W]
    x3 = xs_scr[pl.ds(5, tp), 0:BRANCH_W]
    cw = cw_ref[...]
    xc = cb_ref[...] + cw[3:4] * x0 + cw[2:3] * x1 + cw[1:2] * x2 + cw[0:1] * x3
    gate_x = jax.nn.sigmoid(_dot(xc, wgx_ref[...], HI) + bgx_ref[...])
    gate_a = jax.nn.sigmoid(_dot(xc, wga_ref[...], HI) + bga_ref[...])
    log_a = -A_C * gate_a * jax.nn.softplus(-lam_ref[...])
    a = jnp.exp(log_a)
    th = jnp.tanh(log_a)
    b_in = jnp.sqrt(-2.0 * th / (1.0 - th)) * (gate_x * xc)
    if tp != t_len:
        row = lax.broadcasted_iota(jnp.int32, (tp, 1), 0)
        a = jnp.where(row < t_len, a, 1.0)
        b_in = jnp.where(row < t_len, b_in, 0.0)
    a_scr[...] = a
    b_scr[...] = b_in

    row8 = lax.broadcasted_iota(jnp.int32, (8, 1), 0)

    def group(g, carry):
        r0 = pl.multiple_of(g * 8, 8)
        ag = a_scr[pl.ds(r0, 8), :]
        bg = b_scr[pl.ds(r0, 8), :]
        for s in (1, 2, 4):
            a_sh = jnp.where(row8 >= s, pltpu.roll(ag, s, 0), 1.0)
            b_sh = jnp.where(row8 >= s, pltpu.roll(bg, s, 0), 0.0)
            bg = ag * b_sh + bg
            ag = ag * a_sh
        h = ag * carry + bg
        b_scr[pl.ds(r0, 8), :] = h
        return h[7:8, :]

    h_last = lax.fori_loop(0, tp // 8, group, h0_ref[0])
    h = b_scr[...]
    gate = xs_scr[pl.ds(8, tp), BRANCH_W:2 * BRANCH_W]
    y = h * jax.nn.gelu(gate)
    y_ref[0] = y[0:t_len]
    hl_ref[0] = h_last
    nb_ref[0] = xs_scr[pl.ds(8 + t_len - 3, 3), 0:BRANCH_W]


def _rglru(cols3, h0, cbuf, p):
    b, t, _ = cols3.shape
    tp = _round_up(t, 8)
    body = functools.partial(_rglru_body, t, tp)
    vec = pl.BlockSpec((1, BRANCH_W), lambda i: (0, 0))
    mat = pl.BlockSpec((BRANCH_W, BRANCH_W), lambda i: (0, 0))
    return pl.pallas_call(
        body,
        out_shape=(jax.ShapeDtypeStruct((b, t, BRANCH_W), F32),
                   jax.ShapeDtypeStruct((b, 1, BRANCH_W), F32),
                   jax.ShapeDtypeStruct((b, 3, BRANCH_W), F32)),
        grid=(b,),
        in_specs=[pl.BlockSpec((1, t, 2 * BRANCH_W), lambda i: (i, 0, 2)),
                  pl.BlockSpec((1, 1, BRANCH_W), lambda i: (i, 0, 0)),
                  pl.BlockSpec((1, 3, BRANCH_W), lambda i: (i, 0, 0)),
                  pl.BlockSpec((4, BRANCH_W), lambda i: (0, 0)),
                  vec, mat, vec, mat, vec, vec],
        out_specs=(pl.BlockSpec((1, t, BRANCH_W), lambda i: (i, 0, 0)),
                   pl.BlockSpec((1, 1, BRANCH_W), lambda i: (i, 0, 0)),
                   pl.BlockSpec((1, 3, BRANCH_W), lambda i: (i, 0, 0))),
        scratch_shapes=[pltpu.VMEM((tp + 8, 2 * BRANCH_W), F32),
                        pltpu.VMEM((tp, BRANCH_W), F32),
                        pltpu.VMEM((tp, BRANCH_W), F32)],
        compiler_params=_params(("parallel",)),
        name="rglru",
    )(cols3, h0, cbuf, p["a_conv_w"], p["a_conv_b"], p["a_gx_bd"], p["a_gx_b"], p["a_ga_bd"], p["a_ga_b"],
      p["a_lambda"])


def _rwkv_consts(tc, ch):
    t = np.arange(tc)
    same = (t[:, None] // ch) == (t[None, :] // ch)
    tril = same & (t[None, :] <= t[:, None])
    return jnp.asarray(np.concatenate([tril, same], axis=0).astype(np.float32), BF16)


def _rwkv_body(t_len, tc, n_t, ch, cb_ref, shift_ref, s0_ref, mu_ref, w0_ref, w2_ref, a0_ref, a2_ref, g2_ref,
               kk_ref, ka_ref, rk_ref, lnw_ref, lnb_ref, cs_ref, e3_ref, y_ref, last_ref, sn_ref,
               xs_scr, s_scr, at_scr, bt_scr, kt_scr, bh_scr, kh_scr, v_scr, rt_scr, gc_scr, y_scr, bon_scr,
               g_scr):
    ti = pl.program_id(1)
    t_blk = min(t_len, tc)
    cs = N_HEADS * ch

    @pl.when(ti == 0)
    def _():
        if t_blk != tc:
            xs_scr[...] = jnp.zeros(xs_scr.shape, F32)
        xs_scr[pl.ds(7, 1), :] = shift_ref[0]
        s_scr[...] = jnp.zeros(s_scr.shape, F32)
        for h in range(N_HEADS):
            s_scr[h * HEAD_W:(h + 1) * HEAD_W, h * HEAD_W:(h + 1) * HEAD_W] = s0_ref[0, h]

    xs_scr[pl.ds(8, t_blk), :] = cb_ref[0]
    cb = xs_scr[pl.ds(8, tc), :]
    shifted = xs_scr[pl.ds(7, tc), :]
    cm = cb + (shifted - cb) * mu_ref[...]
    r = cm[:, 0:256]
    k = cm[:, 256:512]
    v = cm[:, 512:768]
    lora = cm[:, 768:896]
    w = -jax.nn.softplus(-(w0_ref[...] + _dot(jnp.tanh(lora), w2_ref[...], HI))) - 0.5
    log_w = -jnp.exp(w)
    a = jax.nn.sigmoid(a0_ref[...] + _dot(lora, a2_ref[...], HI))
    g_scr[...] = _dot(jax.nn.sigmoid(cm[:, 896:1024]), g2_ref[...], HI)
    e3 = e3_ref[...]
    kk = k * kk_ref[...]
    kk = kk / jnp.maximum(jnp.sqrt(_head_sum(kk * kk, e3)), 1e-12)
    k2 = k * (1.0 + (a - 1.0) * ka_ref[...])
    bon_scr[...] = _head_sum(r * k2 * rk_ref[...], e3) * v
    a_s = -kk
    b_s = kk * a
    if t_blk != tc:
        row = lax.broadcasted_iota(jnp.int32, (tc, 1), 0)
        live = row < t_blk
        log_w = jnp.where(live, log_w, 0.0)
        a_s = jnp.where(live, a_s, 0.0)
        b_s = jnp.where(live, b_s, 0.0)
        k2 = jnp.where(live, k2, 0.0)
    cc = _dot(cs_ref[...], jnp.concatenate(_split3(log_w), axis=1))
    cc = cc[:, 0:256] + cc[:, 256:512] + cc[:, 512:768]
    cum = cc[0:tc]
    tot = cc[tc:2 * tc]
    inv = jnp.exp(-cum)
    tail = jnp.exp(tot - cum)
    at_scr[...] = (a_s * jnp.exp(cum - log_w)).astype(BF16)
    bt_scr[...] = (b_s * inv).astype(BF16)
    kt_scr[...] = (k2 * inv).astype(BF16)
    bh_scr[...] = (b_s * tail).astype(BF16)
    kh_scr[...] = (k2 * tail).astype(BF16)
    v_scr[...] = v.astype(BF16)
    rt_scr[...] = r * jnp.exp(cum)
    gc_scr[...] = jnp.exp(tot)

    row_t = lax.broadcasted_iota(jnp.int32, (cs, cs), 0) & (ch - 1)
    col_t = lax.broadcasted_iota(jnp.int32, (cs, cs), 1) & (ch - 1)
    strict = col_t < row_t
    row_t2 = lax.broadcasted_iota(jnp.int32, (cs, 2 * cs), 0) & (ch - 1)
    col_t2 = lax.broadcasted_iota(jnp.int32, (cs, 2 * cs), 1) & (ch - 1)
    incl2 = col_t2 <= row_t2
    eye = (lax.broadcasted_iota(jnp.int32, (cs, cs), 0) == lax.broadcasted_iota(jnp.int32, (cs, cs), 1)).astype(F32)
    lane_head = lax.broadcasted_iota(jnp.int32, (ch, BRANCH_W), 1) >> HEAD_SHIFT
    n_dbl = ch.bit_length() - 1

    def chunk(c, _):
        r0 = pl.multiple_of(c * ch, ch)
        abd, bbd, kbd, bhd, khd, vbd = (_stack_heads(s[pl.ds(r0, ch), :], lane_head)
                                        for s in (at_scr, bt_scr, kt_scr, bh_scr, kh_scr, v_scr))
        rbd = _stack_heads(rt_scr[pl.ds(r0, ch), :], lane_head)
        s1 = _dot_nt(jnp.concatenate([abd, rbd.astype(BF16)], axis=0), jnp.concatenate([bbd, kbd], axis=0))
        l_ab = jnp.where(strict, s1[0:cs, 0:cs], 0.0)
        l_ak = jnp.where(strict, s1[0:cs, cs:2 * cs], 0.0)
        m_r = jnp.where(incl2, s1[cs:2 * cs, :], 0.0)
        tinv = eye + l_ab
        lb16 = l_ab.astype(BF16)
        pw = _dot(lb16, lb16)
        for i in range(1, n_dbl):
            pw16 = pw.astype(BF16)
            if i < n_dbl - 1:
                both = _dot(jnp.concatenate([tinv.astype(BF16), pw16], axis=0), pw16)
                tinv = tinv + both[0:cs]
                pw = both[cs:2 * cs]
            else:
                tinv = tinv + _dot(tinv.astype(BF16), pw16)
        w1 = _dot(l_ak.astype(BF16), vbd)
        x = _dot(tinv.astype(BF16), jnp.concatenate([abd, w1.astype(BF16)], axis=1))
        low = jnp.concatenate([jnp.zeros((cs, BRANCH_W), BF16), vbd], axis=1)
        z = _dot(m_r.astype(BF16), jnp.concatenate([x.astype(BF16), low], axis=0))
        r_hat = rbd + z[:, 0:BRANCH_W]
        sb = s_scr[...]
        uy = _dot_nt(jnp.concatenate([x[:, 0:BRANCH_W], r_hat], axis=0).astype(BF16), sb.astype(BF16))
        u = uy[0:cs] + x[:, BRANCH_W:2 * BRANCH_W]
        ys = uy[cs:2 * cs] + z[:, BRANCH_W:2 * BRANCH_W]
        y_scr[pl.ds(r0, ch), :] = ys[0:ch] + ys[ch:2 * ch] + ys[2 * ch:3 * ch] + ys[3 * ch:4 * ch]
        g_end = gc_scr[pl.ds(r0, ch), :][0:1, :]
        s_scr[...] = sb * g_end + _dot_tn(jnp.concatenate([u.astype(BF16), vbd], axis=0),
                                            jnp.concatenate([bhd, khd], axis=0))
        return 0

    lax.fori_loop(0, tc // ch, chunk, 0)

    y = y_scr[...]
    mean = _head_sum(y, e3) * (1.0 / HEAD_W)
    dev = y - mean
    var = _head_sum(dev * dev, e3) * (1.0 / HEAD_W)
    yn = dev * lax.rsqrt(var + B_GN_EPS) * lnw_ref[...] + lnb_ref[...]
    out = (yn + bon_scr[...]) * g_scr[...]
    y_ref[0] = out[0:t_blk]
    xs_scr[pl.ds(7, 1), :] = xs_scr[pl.ds(8 + t_blk - 1, 1), :]

    @pl.when(ti == n_t - 1)
    def _():
        last_ref[0] = xs_scr[pl.ds(7, 1), :]
        for h in range(N_HEADS):
            sn_ref[0, h] = s_scr[h * HEAD_W:(h + 1) * HEAD_W, h * HEAD_W:(h + 1) * HEAD_W]


def _rwkv(cols3, shift, s0, p):
    b, t, _ = cols3.shape
    if t >= 64:
        ch, tc = 64, min(t, 512)
        assert t % tc == 0
    else:
        ch = 16
        tc = _round_up(t, ch)
    n_t = max(1, t // tc)
    t_blk = min(t, tc)
    body = functools.partial(_rwkv_body, t, tc, n_t, ch)
    consts = _rwkv_consts(tc, ch)
    vec = lambda w: pl.BlockSpec((1, w), lambda i, j: (0, 0))
    sc16 = lambda: pltpu.VMEM((tc, BRANCH_W), BF16)
    sc32 = lambda: pltpu.VMEM((tc, BRANCH_W), F32)
    return pl.pallas_call(
        body,
        out_shape=(jax.ShapeDtypeStruct((b, t, BRANCH_W), F32),
                   jax.ShapeDtypeStruct((b, 1, B_COLS), F32),
                   jax.ShapeDtypeStruct((b, N_HEADS, HEAD_W, HEAD_W), F32)),
        grid=(b, n_t),
        in_specs=[pl.BlockSpec((1, t_blk, B_COLS), lambda i, j: (i, j, 0)),
                  pl.BlockSpec((1, 1, B_COLS), lambda i, j: (i, 0, 0)),
                  pl.BlockSpec((1, N_HEADS, HEAD_W, HEAD_W), lambda i, j: (i, 0, 0, 0)),
                  vec(B_COLS), vec(BRANCH_W),
                  pl.BlockSpec((128, BRANCH_W), lambda i, j: (0, 0)),
                  vec(BRANCH_W),
                  pl.BlockSpec((128, BRANCH_W), lambda i, j: (0, 0)),
                  pl.BlockSpec((128, BRANCH_W), lambda i, j: (0, 0)),
                  vec(BRANCH_W), vec(BRANCH_W), vec(BRANCH_W), vec(BRANCH_W), vec(BRANCH_W),
                  _resident(consts.shape), _resident(p["e3"].shape)],
        out_specs=(pl.BlockSpec((1, t_blk, BRANCH_W), lambda i, j: (i, j, 0)),
                   pl.BlockSpec((1, 1, B_COLS), lambda i, j: (i, 0, 0)),
                   pl.BlockSpec((1, N_HEADS, HEAD_W, HEAD_W), lambda i, j: (i, 0, 0, 0))),
        scratch_shapes=[pltpu.VMEM((tc + 8, B_COLS), F32),
                        pltpu.VMEM((BRANCH_W, BRANCH_W), F32),
                        sc16(), sc16(), sc16(), sc16(), sc16(), sc16(),
                        sc32(), sc32(), sc32(), sc32(), sc32()],
        compiler_params=_params(("parallel", "arbitrary")),
        name="rwkv7",
    )(cols3, shift, s0, p["b_mu"], p["b_w0"], p["b_w2p"], p["b_a0"], p["b_a2p"], p["b_g2"],
      p["b_k_k"], p["b_k_a"], p["b_r_k"], p["b_ln_w"], p["b_ln_b"], consts, p["e3"])


def _hgrn_consts(tc):
    nl = tc.bit_length() - 1
    t = np.arange(tc)
    sel = np.zeros((nl + 1, tc, tc), np.float32)
    for l in range(nl):
        m = 1 << l
        ref = (t // (2 * m)) * (2 * m) + m - 1
        j = t[None, :]
        later = (t > ref)[:, None] & (j > ref[:, None]) & (j <= t[:, None])
        earlier = (t <= ref)[:, None] & (j > t[:, None]) & (j <= ref[:, None])
        sel[l] = later | earlier
    sel[nl] = t[None, :] <= t[:, None]
    x = t[:, None] ^ t[None, :]
    lvl = np.floor(np.log2(np.maximum(x, 1))).astype(np.int32)
    lvl = np.where(t[:, None] > t[None, :], lvl, np.where(t[:, None] == t[None, :], nl, -1))
    return jnp.asarray(sel.reshape((nl + 1) * tc, tc), BF16), jnp.asarray(lvl, jnp.int32), nl


def _hgrn_body(t_len, tc, n_t, nl, cq_ref, cf_ref, cig_ref, s0_ref, lb_ref, ng_ref, sel_ref, lvl_ref, e3_ref,
               y_ref, sn_ref, in_scr, st_scr):
    ti = pl.program_id(1)
    t_blk = min(t_len, tc)

    @pl.when(ti == 0)
    def _():
        st_scr[...] = jnp.zeros(st_scr.shape, F32)
        for h in range(N_HEADS):
            st_scr[h * HEAD_W:(h + 1) * HEAD_W, h * C_EXPAND:(h + 1) * C_EXPAND] = s0_ref[0, h].T

    if t_blk != tc:
        in_scr[...] = jnp.zeros(in_scr.shape, F32)
        in_scr[0, pl.ds(0, t_blk), :] = cq_ref[0]
        in_scr[1, pl.ds(0, t_blk), :] = cf_ref[0]
        in_scr[2, pl.ds(0, t_blk), :] = cig_ref[0]
        q, f, cig = in_scr[0], in_scr[1], in_scr[2]
    else:
        q, f, cig = cq_ref[0], cf_ref[0], cig_ref[0]
    lb = lb_ref[...]
    fg = lb + (1.0 - lb) * jax.nn.sigmoid(f)
    log_f = jnp.log(fg)
    kk = 1.0 - fg
    if t_blk != tc:
        row = lax.broadcasted_iota(jnp.int32, (tc, 1), 0)
        live = row < t_blk
        log_f = jnp.where(live, log_f, 0.0)
        kk = jnp.where(live, kk, 0.0)
    hi = log_f.astype(BF16)
    lo = (log_f - hi.astype(F32)).astype(BF16)
    parts = jnp.concatenate([hi, lo], axis=1)

    def seg_sum(l):
        d = _dot(sel_ref[pl.ds(l * tc, tc), :], parts)
        return d[:, 0:C_FDIM] + d[:, C_FDIM:2 * C_FDIM]

    lvl = lvl_ref[...]
    att = [None] * N_HEADS
    for l in range(nl + 1):
        if l < nl:
            e = jnp.exp(seg_sum(l))
            qs = (q * e).astype(BF16)
            ks = (kk * e).astype(BF16)
        else:
            qs = q.astype(BF16)
            ks = kk.astype(BF16)
        hit = lvl == l
        for h in range(N_HEADS):
            kl = slice(h * C_EXPAND, (h + 1) * C_EXPAND)
            a = _dot_nt(qs[:, kl], ks[:, kl])
            att[h] = jnp.where(hit, a, 0.0 if att[h] is None else att[h])
    cum = seg_sum(nl)
    c_last = cum[tc - 1:tc, :]
    lane_head = lax.broadcasted_iota(jnp.int32, (tc, BRANCH_W), 1) >> HEAD_SHIFT
    v = cig[:, 0:BRANCH_W]
    v_stack = _stack_heads(v, lane_head).astype(BF16)
    p_cat = jnp.concatenate([a.astype(BF16) for a in att], axis=1)
    st = st_scr[...]
    o = _dot(p_cat, v_stack) + _dot_nt((q * jnp.exp(cum)).astype(BF16), st.astype(BF16))
    ms = _head_sum(o * o, e3_ref[...]) * (1.0 / HEAD_W)
    y = o * lax.rsqrt(ms + NORM_EPS) * ng_ref[...] * jax.nn.silu(cig[:, BRANCH_W:2 * BRANCH_W])
    y_ref[0] = y[0:t_blk]
    upd = _dot_tn(v.astype(BF16), (kk * jnp.exp(c_last - cum)).astype(BF16))
    rh = lax.broadcasted_iota(jnp.int32, (BRANCH_W, C_FDIM), 0) >> HEAD_SHIFT
    ch_ = lax.broadcasted_iota(jnp.int32, (BRANCH_W, C_FDIM), 1) >> C_EXPAND_SHIFT
    st_scr[...] = st * jnp.exp(c_last) + jnp.where(rh == ch_, upd, 0.0)

    @pl.when(ti == n_t - 1)
    def _():
        for h in range(N_HEADS):
            sn_ref[0, h] = st_scr[h * HEAD_W:(h + 1) * HEAD_W, h * C_EXPAND:(h + 1) * C_EXPAND].T


def _hgrn(cols3, s0, lb, ng, e3):
    b, t, _ = cols3.shape
    tc = 256 if t >= 256 else 16
    assert t % tc == 0 or t < tc
    n_t = max(1, t // tc)
    t_blk = min(t, tc)
    sel, lvl, nl = _hgrn_consts(tc)
    body = functools.partial(_hgrn_body, t, tc, n_t, nl)
    blk = lambda idx: pl.BlockSpec((1, t_blk, C_FDIM), lambda i, j: (i, j, idx))
    pad_shape = (3, tc, C_FDIM) if t_blk != tc else (1, 8, 128)
    return pl.pallas_call(
        body,
        out_shape=(jax.ShapeDtypeStruct((b, t, BRANCH_W), F32),
                   jax.ShapeDtypeStruct((b, N_HEADS, C_EXPAND, HEAD_W), F32)),
        grid=(b, n_t),
        in_specs=[blk(3), blk(4), blk(5),
                  pl.BlockSpec((1, N_HEADS, C_EXPAND, HEAD_W), lambda i, j: (i, 0, 0, 0)),
                  pl.BlockSpec((1, C_FDIM), lambda i, j: (0, 0)),
                  pl.BlockSpec((1, BRANCH_W), lambda i, j: (0, 0)),
                  _resident(sel.shape), _resident(lvl.shape), _resident(e3.shape)],
        out_specs=(pl.BlockSpec((1, t_blk, BRANCH_W), lambda i, j: (i, j, 0)),
                   pl.BlockSpec((1, N_HEADS, C_EXPAND, HEAD_W), lambda i, j: (i, 0, 0, 0))),
        scratch_shapes=[pltpu.VMEM(pad_shape, F32),
                        pltpu.VMEM((BRANCH_W, C_FDIM), F32)],
        compiler_params=_params(("parallel", "arbitrary")),
        name="hgrn2",
    )(cols3, cols3, cols3, s0, lb, ng, sel, lvl, e3)


def _rope_rows(x, cos, sin_signed):
    lane = lax.broadcasted_iota(jnp.int32, x.shape, 1)
    swapped = jnp.where((lane & (HEAD_W - 1)) < HEAD_W // 2,
                        pltpu.roll(x, BRANCH_W - HEAD_W // 2, 1), pltpu.roll(x, HEAD_W // 2, 1))
    return x * cos + swapped * sin_signed


def _pattern_weight(d):
    ok = d >= 0
    w = ((d <= 128).astype(F32) + (((d & 3) == 0) & (d <= 512)).astype(F32)
         + (((d & 15) == 0) & (d <= 2048)).astype(F32))
    return jnp.where(ok, w, 0.0)


def _attn_prompt_body(n_blk, qkv_ref, cos_ref, sin_ref, y_ref, k_ref, q_scr, kb_scr, vh_scr, s_scr):
    t = n_blk * ATT_BLK
    qkv = qkv_ref[0]
    cos = cos_ref[...]
    sin = sin_ref[...]
    q = _rope_rows(qkv[:, 0:BRANCH_W], cos, sin) * (HEAD_W ** -0.5)
    k = _rope_rows(qkv[:, BRANCH_W:2 * BRANCH_W], cos, sin)
    k_ref[0] = k
    q_scr[...] = q
    kb_scr[...] = k.astype(BF16)
    v = qkv[:, 2 * BRANCH_W:3 * BRANCH_W]
    lane_head = lax.broadcasted_iota(jnp.int32, (ATT_BLK, BRANCH_W), 1) >> HEAD_SHIFT
    for j in range(n_blk):
        vj = v[j * ATT_BLK:(j + 1) * ATT_BLK, :]
        for h in range(N_HEADS):
            vh_scr[j, pl.ds(h * ATT_BLK, ATT_BLK), :] = jnp.where(lane_head == h, vj, 0.0).astype(BF16)

    rr = lax.broadcasted_iota(jnp.int32, (ATT_BLK, ATT_BLK), 0)
    cc = lax.broadcasted_iota(jnp.int32, (ATT_BLK, ATT_BLK), 1)
    rel = rr - cc

    def q_block(i, _):
        q0 = pl.multiple_of(i * ATT_BLK, ATT_BLK)
        qi = q_scr[pl.ds(q0, ATT_BLK), :]
        q_bd = jnp.concatenate([jnp.where(lane_head == h, qi, 0.0) for h in range(N_HEADS)], axis=0).astype(BF16)

        def weights(j):
            w1 = _pattern_weight((i - j) * ATT_BLK + rel)
            return jnp.concatenate([w1] * N_HEADS, axis=0)

        def pass1(j, m_run):
            k0 = pl.multiple_of(j * ATT_BLK, ATT_BLK)
            s = _dot_nt(q_bd, kb_scr[pl.ds(k0, ATT_BLK), :])
            s = jnp.where(weights(j) > 0.0, s, NEG_BIG)
            s_scr[j] = s
            return jnp.maximum(m_run, s)

        m_run = lax.fori_loop(0, i + 1, pass1, jnp.full((N_HEADS * ATT_BLK, ATT_BLK), NEG_BIG, F32))
        m = jnp.max(m_run, axis=-1, keepdims=True)

        def pass2(j, carry):
            l_run, acc = carry
            p = weights(j) * jnp.exp(s_scr[j] - m)
            p_cat = jnp.concatenate([p[h * ATT_BLK:(h + 1) * ATT_BLK, :] for h in range(N_HEADS)], axis=1)
            acc = acc + _dot(p_cat.astype(BF16), vh_scr[j])
            return l_run + p, acc

        l_run, acc = lax.fori_loop(0, i + 1, pass2, (jnp.zeros((N_HEADS * ATT_BLK, ATT_BLK), F32),
                                                     jnp.zeros((ATT_BLK, BRANCH_W), F32)))
        l = jnp.sum(l_run, axis=-1, keepdims=True)
        den = jnp.zeros((ATT_BLK, BRANCH_W), F32)
        for h in range(N_HEADS):
            den = jnp.where(lane_head == h, l[h * ATT_BLK:(h + 1) * ATT_BLK, :], den)
        y_ref[0, pl.ds(q0, ATT_BLK), :] = acc / den
        return 0

    lax.fori_loop(0, n_blk, q_block, 0)


def _attn_prompt(cols3, cos, sin):
    b, t, _ = cols3.shape
    n_blk = t // ATT_BLK
    assert n_blk * ATT_BLK == t
    body = functools.partial(_attn_prompt_body, n_blk)
    return pl.pallas_call(
        body,
        out_shape=(jax.ShapeDtypeStruct((b, t, BRANCH_W), F32), jax.ShapeDtypeStruct((b, t, BRANCH_W), F32)),
        grid=(b,),
        in_specs=[pl.BlockSpec((1, t, 3 * BRANCH_W), lambda i: (i, 0, 4)),
                  _resident((t, BRANCH_W)), _resident((t, BRANCH_W))],
        out_specs=(pl.BlockSpec((1, t, BRANCH_W), lambda i: (i, 0, 0)),
                   pl.BlockSpec((1, t, BRANCH_W), lambda i: (i, 0, 0))),
        scratch_shapes=[pltpu.VMEM((t, BRANCH_W), F32),
                        pltpu.VMEM((t, BRANCH_W), BF16),
                        pltpu.VMEM((n_blk, N_HEADS * ATT_BLK, BRANCH_W), BF16),
                        pltpu.VMEM((n_blk, N_HEADS * ATT_BLK, ATT_BLK), F32)],
        compiler_params=_params(("parallel",)),
        name="attn_prompt",
    )(cols3, cos, sin)


def _attn_sample_body(t_len, l_cache, qkv_ref, kc_ref, vc_ref, cos_ref, sin_ref, y_ref, k_ref, pad_scr):
    tp = 8
    pad_scr[...] = jnp.zeros(pad_scr.shape, F32)
    pad_scr[pl.ds(0, t_len), :] = qkv_ref[0]
    qkv = pad_scr[...]
    cos = cos_ref[...]
    sin = sin_ref[...]
    q = _rope_rows(qkv[:, 0:BRANCH_W], cos, sin) * (HEAD_W ** -0.5)
    k_new = _rope_rows(qkv[:, BRANCH_W:2 * BRANCH_W], cos, sin)
    v_new = qkv[:, 2 * BRANCH_W:3 * BRANCH_W]
    k_ref[0] = k_new[0:t_len]
    lane_head = lax.broadcasted_iota(jnp.int32, (tp, BRANCH_W), 1) >> HEAD_SHIFT
    q_bd = jnp.concatenate([jnp.where(lane_head == h, q, 0.0) for h in range(N_HEADS)], axis=0).astype(BF16)
    n_row = N_HEADS * tp
    s_c = _dot_nt(q_bd, kc_ref[0].astype(BF16))
    s_n = _dot_nt(q_bd, k_new.astype(BF16))
    t_row = lax.broadcasted_iota(jnp.int32, (n_row, 1), 0) & (tp - 1)
    col_c = lax.broadcasted_iota(jnp.int32, (n_row, l_cache), 1)
    w_c = _pattern_weight(l_cache + t_row - col_c)
    col_n = lax.broadcasted_iota(jnp.int32, (n_row, tp), 1)
    w_n = jnp.where(col_n < t_len, _pattern_weight(t_row - col_n), 0.0)
    s_c = jnp.where(w_c > 0.0, s_c, NEG_BIG)
    s_n = jnp.where(w_n > 0.0, s_n, NEG_BIG)
    m = jnp.maximum(jnp.max(s_c, axis=-1, keepdims=True), jnp.max(s_n, axis=-1, keepdims=True))
    p_c = w_c * jnp.exp(s_c - m)
    p_n = w_n * jnp.exp(s_n - m)
    l = jnp.sum(p_c, axis=-1, keepdims=True) + jnp.sum(p_n, axis=-1, keepdims=True)
    num = _dot(p_c.astype(BF16), vc_ref[0].astype(BF16)) + _dot(p_n.astype(BF16), v_new.astype(BF16))
    ratio = num / l
    out = jnp.zeros((tp, BRANCH_W), F32)
    for h in range(N_HEADS):
        out = jnp.where(lane_head == h, ratio[h * tp:(h + 1) * tp, :], out)
    y_ref[0] = out[0:t_len]


def _attn_sample(cols3, k_cache, v_cache, cos, sin):
    b, t, _ = cols3.shape
    l_cache = k_cache.shape[1]
    assert t <= 8
    body = functools.partial(_attn_sample_body, t, l_cache)
    return pl.pallas_call(
        body,
        out_shape=(jax.ShapeDtypeStruct((b, t, BRANCH_W), F32), jax.ShapeDtypeStruct((b, t, BRANCH_W), F32)),
        grid=(b,),
        in_specs=[pl.BlockSpec((1, t, 3 * BRANCH_W), lambda i: (i, 0, 4)),
                  pl.BlockSpec((1, l_cache, BRANCH_W), lambda i: (i, 0, 0)),
                  pl.BlockSpec((1, l_cache, BRANCH_W), lambda i: (i, 0, 0)),
                  pl.BlockSpec((8, BRANCH_W), lambda i: (0, 0)),
                  pl.BlockSpec((8, BRANCH_W), lambda i: (0, 0))],
        out_specs=(pl.BlockSpec((1, t, BRANCH_W), lambda i: (i, 0, 0)),
                   pl.BlockSpec((1, t, BRANCH_W), lambda i: (i, 0, 0))),
        scratch_shapes=[pltpu.VMEM((8, 3 * BRANCH_W), F32)],
        compiler_params=_params(("parallel",)),
        name="attn_sample",
    )(cols3, k_cache, v_cache, cos, sin)


def _merge_body(x_ref, g_ref, wgt_ref, ya_ref, yb_ref, yc_ref, yd_ref, wb_ref, wo_ref, o_ref):
    x = x_ref[...]
    u = _rms_rows(x, g_ref[...]).astype(BF16)
    merged = None
    for n, y_ref in enumerate((ya_ref, yb_ref, yc_ref, yd_ref)):
        z = _dot(y_ref[...].astype(BF16), wb_ref[n])
        gate = jax.nn.sigmoid(_dot(u, wgt_ref[:, n * D_MODEL:(n + 1) * D_MODEL]))
        merged = gate * z if merged is None else merged + gate * z
    o_ref[...] = x + _dot(merged.astype(BF16), wo_ref[...])


def _merge(x2d, g, wgt, ys, wb, wo):
    n, d = x2d.shape
    tm = min(n, 512)
    row = lambda w: pl.BlockSpec((tm, w), lambda i: (i, 0))
    return pl.pallas_call(
        _merge_body,
        out_shape=jax.ShapeDtypeStruct((n, d), F32),
        grid=(n // tm,),
        in_specs=[row(d), pl.BlockSpec((1, d), lambda i: (0, 0)), _resident(wgt.shape),
                  row(BRANCH_W), row(BRANCH_W), row(BRANCH_W), row(BRANCH_W),
                  _resident(wb.shape), _resident(wo.shape)],
        out_specs=row(d),
        compiler_params=_params(("parallel",)),
        name="merge",
    )(x2d, g, wgt, *ys, wb, wo)


FFN_COL = 1024


def _ffn_body(t_len, tm, n_t, x_ref, g_ref, wg_ref, wu_ref, wd_ref, cw_ref, cb_ref, st_ref, o_ref, ns_ref,
              pad_scr, carry_scr):
    ti = pl.program_id(1)
    t_blk = min(t_len, tm)

    @pl.when(ti == 0)
    def _():
        carry_scr[...] = st_ref[0]

    if t_blk != tm:
        pad_scr[...] = jnp.zeros(pad_scr.shape, F32)
        pad_scr[pl.ds(0, t_blk), :] = x_ref[0]
        x = pad_scr[...]
    else:
        x = x_ref[0]
    v = _rms_rows(x, g_ref[...]).astype(BF16)
    row = lax.broadcasted_iota(jnp.int32, (tm, 1), 0)
    acc = jnp.zeros((tm, D_MODEL), F32)
    cw = cw_ref[...]
    for c in range(D_FF // FFN_COL):
        sl = slice(c * FFN_COL, (c + 1) * FFN_COL)
        hg = _dot(v, wg_ref[:, sl])
        hu = _dot(v, wu_ref[:, sl])
        p2 = carry_scr[0:1, sl]
        p1 = carry_scr[1:2, sl]
        h1 = jnp.where(row == 0, p1, pltpu.roll(hg, 1, 0))
        h2 = jnp.where(row == 0, p2, jnp.where(row == 1, p1, pltpu.roll(hg, 2, 0)))
        conv = cb_ref[:, sl] + cw[2:3, sl] * hg + cw[1:2, sl] * h1 + cw[0:1, sl] * h2
        hmid = (jax.nn.gelu(conv) * hu).astype(BF16)
        acc = acc + _dot(hmid, wd_ref[sl, :])
        carry_scr[:, sl] = hg[t_blk - 2:t_blk, :]
    y = x + acc
    o_ref[0] = y[0:t_blk]

    @pl.when(ti == n_t - 1)
    def _():
        ns_ref[0] = carry_scr[...]


def _ffn(x3, g, wg, wu, wd, cw, cb, state):
    b, t, d = x3.shape
    tm = min(_round_up(t, 8), 512)
    n_t = max(1, t // tm)
    t_blk = min(t, tm)
    assert t_blk * n_t == t and t_blk >= 2
    body = functools.partial(_ffn_body, t, tm, n_t)
    return pl.pallas_call(
        body,
        out_shape=(jax.ShapeDtypeStruct((b, t, d), F32), jax.ShapeDtypeStruct((b, 2, D_FF), F32)),
        grid=(b, n_t),
        in_specs=[pl.BlockSpec((1, t_blk, d), lambda i, j: (i, j, 0)),
                  pl.BlockSpec((1, d), lambda i, j: (0, 0)),
                  _resident(wg.shape), _resident(wu.shape), _resident(wd.shape),
                  pl.BlockSpec((3, D_FF), lambda i, j: (0, 0)),
                  pl.BlockSpec((1, D_FF), lambda i, j: (0, 0)),
                  pl.BlockSpec((1, 2, D_FF), lambda i, j: (i, 0, 0))],
        out_specs=(pl.BlockSpec((1, t_blk, d), lambda i, j: (i, j, 0)),
                   pl.BlockSpec((1, 2, D_FF), lambda i, j: (i, 0, 0))),
        scratch_shapes=[pltpu.VMEM((tm, d), F32), pltpu.VMEM((2, D_FF), F32)],
        compiler_params=_params(("parallel", "arbitrary")),
        name="convffn",
    )(x3, g, wg, wu, wd, cw, cb, state)


def _norm_body(x_ref, g_ref, o_ref):
    o_ref[...] = _rms_rows(x_ref[...], g_ref[...])


def _final_norm(x2d, g):
    n, d = x2d.shape
    tm = min(n, 1024)
    return pl.pallas_call(
        _norm_body,
        out_shape=jax.ShapeDtypeStruct((n, d), F32),
        grid=(n // tm,),
        in_specs=[pl.BlockSpec((tm, d), lambda i: (i, 0)), pl.BlockSpec((1, d), lambda i: (0, 0))],
        out_specs=pl.BlockSpec((tm, d), lambda i: (i, 0)),
        compiler_params=_params(("parallel",)),
        name="final_norm",
    )(x2d, g)


def _block_diag(w):
    out = jnp.zeros((BRANCH_W, BRANCH_W), F32)
    for h in range(N_HEADS):
        out = out.at[h * HEAD_W:(h + 1) * HEAD_W, h * HEAD_W:(h + 1) * HEAD_W].set(w[h])
    return out


def _head_sum_matrix3():
    i = np.arange(BRANCH_W) // HEAD_W
    e = (i[:, None] == i[None, :]).astype(np.float32)
    return jnp.asarray(np.concatenate([e, e, e], axis=0), BF16)


def _rope_tables(pos, rows):
    half = HEAD_W // 2
    inv = ROPE_THETA ** (-jnp.arange(half, dtype=F32) / half)
    ang = pos.astype(F32)[:, None] * inv[None, :]
    cos = jnp.cos(ang)
    sin = jnp.sin(ang)
    cos_t = jnp.tile(jnp.concatenate([cos, cos], axis=1), (1, N_HEADS))
    sin_t = jnp.tile(jnp.concatenate([-sin, sin], axis=1), (1, N_HEADS))
    pad = rows - pos.shape[0]
    if pad:
        cos_t = jnp.pad(cos_t, ((0, pad), (0, 0)))
        sin_t = jnp.pad(sin_t, ((0, pad), (0, 0)))
    return cos_t, sin_t


def _layer_params(l, W):
    w_in = W["w_in"][l]
    o = 0
    parts = {}
    for name, width in (("ax", 256), ("ag", 256), ("cb", 1024), ("cq", 512), ("cf", 512), ("ci", 256),
                        ("cg", 256), ("dq", 256), ("dk", 256), ("dv", 256), ("gt", 4096)):
        parts[name] = w_in[:, o:o + width]
        o += width
    order = ("cb", "ax", "ag", "cq", "cf", "ci", "cg", "dq", "dk", "dv")
    p = {
        "w_mix": jnp.concatenate([parts[n] for n in order], axis=1).astype(BF16),
        "w_gt": parts["gt"].astype(BF16),
        "norm1_g": W["norm1_g"][l][None, :],
        "a_conv_w": W["a_conv_w"][l], "a_conv_b": W["a_conv_b"][l][None, :],
        "a_gx_bd": _block_diag(W["a_gx_w"][l]), "a_gx_b": W["a_gx_b"][l][None, :],
        "a_ga_bd": _block_diag(W["a_ga_w"][l]), "a_ga_b": W["a_ga_b"][l][None, :],
        "a_lambda": W["a_lambda"][l][None, :],
        "b_mu": W["b_mu"][l][None, :], "b_w0": W["b_w0"][l][None, :], "b_a0": W["b_a0"][l][None, :],
        "b_w2p": jnp.concatenate([W["b_w2"][l], jnp.zeros((64, BRANCH_W), F32)], axis=0),
        "b_a2p": jnp.concatenate([jnp.zeros((64, BRANCH_W), F32), W["b_a2"][l]], axis=0),
        "b_g2": W["b_g2"][l],
        "b_k_k": W["b_k_k"][l][None, :], "b_k_a": W["b_k_a"][l][None, :],
        "b_r_k": W["b_r_k"][l].reshape(1, BRANCH_W),
        "b_ln_w": W["b_ln_w"][l][None, :], "b_ln_b": W["b_ln_b"][l][None, :],
        "c_norm_g": jnp.tile(W["c_norm_g"][l], N_HEADS)[None, :],
        "w_branch": W["w_branch"][l].astype(BF16), "w_out": W["w_out"][l].astype(BF16),
        "norm2_g": W["norm2_g"][l][None, :],
        "ffn_w_gate": W["ffn_w_gate"][l].astype(BF16), "ffn_w_up": W["ffn_w_up"][l].astype(BF16),
        "ffn_w_down": W["ffn_w_down"][l].astype(BF16),
        "ffn_conv_w": W["ffn_conv_w"][l], "ffn_conv_b": W["ffn_conv_b"][l][None, :],
        "e3": _head_sum_matrix3(),
    }
    return p


def _trunk(x, pos0, st, W, params, lb_all):
    b, t, d = x.shape
    depth = W["w_in"].shape[0]
    has_cache = st["k"] is not None
    if has_cache:
        cos, sin = _rope_tables(pos0 + jnp.arange(t), 8)
    else:
        cos, sin = _rope_tables(pos0 + jnp.arange(t), t)
    outs = {n: [] for n in ("ha", "ca", "wkv", "sh", "sc", "k", "v", "cf")}
    for l in range(depth):
        p = params[l]
        cols = _inproj(x.reshape(b * t, d), p["norm1_g"], p["w_mix"]).reshape(b, t, N_MIX_COLS)
        y_a, h_new, ca_new = _rglru(cols, st["ha"][l][:, None, :], st["ca"][l], p)
        y_b, sh_new, wkv_new = _rwkv(cols, st["sh"][l][:, None, :], st["wkv"][l], p)
        y_c, sc_new = _hgrn(cols, st["sc"][l], lb_all[l][None, :], p["c_norm_g"], p["e3"])
        if has_cache:
            lc = st["k"].shape[2]
            y_d, k_rows = _attn_sample(cols, st["k"][l].reshape(b, lc, BRANCH_W),
                                       st["v"][l].reshape(b, lc, BRANCH_W), cos, sin)
        else:
            y_d, k_rows = _attn_prompt(cols, cos, sin)
        v_rows = cols[:, :, N_MIX_COLS - BRANCH_W:]
        ys = [y.reshape(b * t, BRANCH_W) for y in (y_a, y_b, y_c, y_d)]
        x1 = _merge(x.reshape(b * t, d), p["norm1_g"], p["w_gt"], ys, p["w_branch"], p["w_out"])
        x, cf_new = _ffn(x1.reshape(b, t, d), p["norm2_g"], p["ffn_w_gate"], p["ffn_w_up"], p["ffn_w_down"],
                         p["ffn_conv_w"], p["ffn_conv_b"], st["cf"][l])
        outs["ha"].append(h_new[:, 0, :])
        outs["ca"].append(ca_new)
        outs["wkv"].append(wkv_new)
        outs["sh"].append(sh_new[:, 0, :])
        outs["sc"].append(sc_new)
        keep = t if has_cache else min(D_WIN_MAX, t)
        outs["k"].append(k_rows[:, t - keep:].reshape(b, keep, N_HEADS, HEAD_W))
        outs["v"].append(v_rows[:, t - keep:].reshape(b, keep, N_HEADS, HEAD_W))
        outs["cf"].append(cf_new)
    y = _final_norm(x.reshape(b * t, d), W["final_norm_g"][None, :]).reshape(b, t, d)
    return (y,) + tuple(jnp.stack(outs[n]) for n in ("ha", "ca", "wkv", "sh", "sc", "k", "v", "cf"))


def kernel(x_prompt, x_sample, state_a_h, state_a_conv, state_b_wkv, state_b_shift, state_c_s, cache_d_k, cache_d_v, state_ffn_conv, norm1_g, w_in, a_conv_w, a_conv_b, a_gx_w, a_gx_b, a_ga_w, a_ga_b, a_lambda, b_mu, b_w0, b_w2, b_a0, b_a2, b_g2, b_k_k, b_k_a, b_r_k, b_ln_w, b_ln_b, c_lb, c_norm_g, w_branch, w_out, norm2_g, ffn_w_gate, ffn_w_up, ffn_conv_w, ffn_conv_b, ffn_w_down, final_norm_g):
    W = {"norm1_g": norm1_g, "w_in": w_in, "a_conv_w": a_conv_w, "a_conv_b": a_conv_b, "a_gx_w": a_gx_w,
         "a_gx_b": a_gx_b, "a_ga_w": a_ga_w, "a_ga_b": a_ga_b, "a_lambda": a_lambda, "b_mu": b_mu,
         "b_w0": b_w0, "b_w2": b_w2, "b_a0": b_a0, "b_a2": b_a2, "b_g2": b_g2, "b_k_k": b_k_k,
         "b_k_a": b_k_a, "b_r_k": b_r_k, "b_ln_w": b_ln_w, "b_ln_b": b_ln_b, "c_lb": c_lb,
         "c_norm_g": c_norm_g, "w_branch": w_branch, "w_out": w_out, "norm2_g": norm2_g,
         "ffn_w_gate": ffn_w_gate, "ffn_w_up": ffn_w_up, "ffn_conv_w": ffn_conv_w,
         "ffn_conv_b": ffn_conv_b, "ffn_w_down": ffn_w_down, "final_norm_g": final_norm_g}
    depth = w_in.shape[0]
    params = [_layer_params(l, W) for l in range(depth)]
    lb_sm = jax.nn.softmax(c_lb.astype(F32), axis=0)
    lb_all = jnp.cumsum(lb_sm, axis=0) - lb_sm[0]
    b_p, t_p = x_prompt.shape[:2]

    def zeros(*s):
        return jnp.zeros((depth, b_p) + s, F32)

    st_p = {"ha": zeros(BRANCH_W), "ca": zeros(3, BRANCH_W), "wkv": zeros(N_HEADS, HEAD_W, HEAD_W),
            "sh": zeros(B_COLS), "sc": zeros(N_HEADS, C_EXPAND, HEAD_W), "k": None, "v": None,
            "cf": zeros(2, D_FF)}
    out_p = _trunk(x_prompt, 0, st_p, W, params, lb_all)
    st_s = {"ha": state_a_h, "ca": state_a_conv, "wkv": state_b_wkv, "sh": state_b_shift, "sc": state_c_s,
            "k": cache_d_k, "v": cache_d_v, "cf": state_ffn_conv}
    out_s = _trunk(x_sample, PAST_LEN, st_s, W, params, lb_all)
    return (out_p[0], out_s[0]) + out_p[1:] + out_s[1:]
```

```python
import functools

import numpy as np

import jax
import jax.numpy as jnp
from jax import lax
from jax.experimental import pallas as pl
from jax.experimental.pallas import tpu as pltpu

F32 = jnp.float32
BF16 = jnp.bfloat16
HI = lax.Precision.HIGHEST

D_MODEL = 1024
BRANCH_W = 256
N_HEADS = 4
HEAD_W = 64
HEAD_SHIFT = 6
C_EXPAND = 128
C_EXPAND_SHIFT = 7
C_FDIM = 512
B_COLS = 1024
D_FF = 3072
NORM_EPS = 1e-6
A_C = 8.0
B_GN_EPS = 64e-5
ROPE_THETA = 10000.0
PAST_LEN = 8192
D_WIN_MAX = 2048
NEG_BIG = -1e30
N_MIX_COLS = 3840
ATT_BLK = 128

VMEM_LIMIT = 56 * 1024 * 1024


def _dot(a, b, prec=None):
    return jnp.dot(a, b, preferred_element_type=F32, precision=prec)


def _dot_nt(a, b, prec=None):
    return lax.dot_general(a, b, (((1,), (1,)), ((), ())), preferred_element_type=F32, precision=prec)


def _dot_tn(a, b, prec=None):
    return lax.dot_general(a, b, (((0,), (0,)), ((), ())), preferred_element_type=F32, precision=prec)


def _rms_rows(x, g):
    ms = jnp.mean(x * x, axis=-1, keepdims=True)
    return x * lax.rsqrt(ms + NORM_EPS) * g


def _round_up(n, m):
    return (n + m - 1) // m * m


def _params(sem):
    return pltpu.CompilerParams(dimension_semantics=sem, vmem_limit_bytes=VMEM_LIMIT)


def _resident(shape):
    nd = len(shape)
    return pl.BlockSpec(shape, lambda *_: (0,) * nd, pipeline_mode=pl.Buffered(1))


def _split3(x):
    hi = x.astype(BF16)
    r1 = x - hi.astype(F32)
    mid = r1.astype(BF16)
    lo = (r1 - mid.astype(F32)).astype(BF16)
    return hi, mid, lo


def _head_sum(x, e3):
    return _dot(jnp.concatenate(_split3(x), axis=1), e3)


def _dot_split(x, w3):
    hi = x.astype(BF16)
    lo = (x - hi.astype(F32)).astype(BF16)
    return _dot(jnp.concatenate([hi, lo, hi], axis=1), w3)


def _split_weight(w):
    hi = w.astype(BF16)
    lo = (w - hi.astype(F32)).astype(BF16)
    return jnp.concatenate([hi, hi, lo], axis=0)


def _stack_heads(x, lane_head):
    return jnp.concatenate([jnp.where(lane_head == h, x, jnp.zeros_like(x)) for h in range(N_HEADS)], axis=0)


def _inproj_body(x_ref, g_ref, w_ref, o_ref, u_scr):
    @pl.when(pl.program_id(1) == 0)
    def _():
        u_scr[...] = _rms_rows(x_ref[...], g_ref[...]).astype(BF16)

    o_ref[...] = _dot(u_scr[...], w_ref[...])


def _inproj(x2d, g, w):
    n, d = x2d.shape
    nc = w.shape[1]
    tm = min(n, 1024)
    tn = 768
    return pl.pallas_call(
        _inproj_body,
        out_shape=jax.ShapeDtypeStruct((n, nc), F32),
        grid=(n // tm, nc // tn),
        in_specs=[pl.BlockSpec((tm, d), lambda i, j: (i, 0)),
                  pl.BlockSpec((1, d), lambda i, j: (0, 0)),
                  pl.BlockSpec((d, tn), lambda i, j: (0, j))],
        out_specs=pl.BlockSpec((tm, tn), lambda i, j: (i, j)),
        scratch_shapes=[pltpu.VMEM((tm, d), BF16)],
        compiler_params=_params(("parallel", "arbitrary")),
        name="inproj",
    )(x2d, g, w)


def _rglru_body(t_len, tp, axg_ref, h0_ref, cbuf_ref, cw_ref, cb_ref, wgx_ref, bgx_ref, wga_ref, bga_ref,
                lam_ref, y_ref, hl_ref, nb_ref, xs_scr, a_scr, b_scr):
    if tp != t_len:
        xs_scr[...] = jnp.zeros(xs_scr.shape, F32)
    xs_scr[pl.ds(5, 3), :] = jnp.concatenate([cbuf_ref[0], jnp.zeros((3, BRANCH_W), F32)], axis=1)
    xs_scr[pl.ds(8, t_len), :] = axg_ref[0]
    x0 = xs_scr[pl.ds(8, tp), 0:BRANCH_W]
    x1 = xs_scr[pl.ds(7, tp), 0:BRANCH_W]
    x2 = xs_scr[pl.ds(6, tp), 0:BRANCH_W]
    x3 = xs_scr[pl.ds(5, tp), 0:BRANCH_W]
    cw = cw_ref[...]
    xc = cb_ref[...] + cw[3:4] * x0 + cw[2:3] * x1 + cw[1:2] * x2 + cw[0:1] * x3
    gate_x = jax.nn.sigmoid(_dot_split(xc, wgx_ref[...]) + bgx_ref[...])
    gate_a = jax.nn.sigmoid(_dot_split(xc, wga_ref[...]) + bga_ref[...])
    log_a = -A_C * gate_a * jax.nn.softplus(-lam_ref[...])
    a = jnp.exp(log_a)
    th = jnp.tanh(log_a)
    b_in = jnp.sqrt(-2.0 * th / (1.0 - th)) * (gate_x * xc)
    if tp != t_len:
        row = lax.broadcasted_iota(jnp.int32, (tp, 1), 0)
        a = jnp.where(row < t_len, a, 1.0)
        b_in = jnp.where(row < t_len, b_in, 0.0)
    a_scr[...] = a
    b_scr[...] = b_in

    row8 = lax.broadcasted_iota(jnp.int32, (8, 1), 0)

    def group(g, carry):
        r0 = pl.multiple_of(g * 8, 8)
        ag = a_scr[pl.ds(r0, 8), :]
        bg = b_scr[pl.ds(r0, 8), :]
        for s in (1, 2, 4):
            a_sh = jnp.where(row8 >= s, pltpu.roll(ag, s, 0), 1.0)
            b_sh = jnp.where(row8 >= s, pltpu.roll(bg, s, 0), 0.0)
            bg = ag * b_sh + bg
            ag = ag * a_sh
        h = ag * carry + bg
        b_scr[pl.ds(r0, 8), :] = h
        return h[7:8, :]

    h_last = lax.fori_loop(0, tp // 8, group, h0_ref[0])
    h = b_scr[...]
    gate = xs_scr[pl.ds(8, tp), BRANCH_W:2 * BRANCH_W]
    y = h * jax.nn.gelu(gate)
    y_ref[0] = y[0:t_len]
    hl_ref[0] = h_last
    nb_ref[0] = xs_scr[pl.ds(8 + t_len - 3, 3), 0:BRANCH_W]


def _rglru(cols3, h0, cbuf, p):
    b, t, _ = cols3.shape
    tp = _round_up(t, 8)
    body = functools.partial(_rglru_body, t, tp)
    vec = pl.BlockSpec((1, BRANCH_W), lambda i: (0, 0))
    mat = pl.BlockSpec((3 * BRANCH_W, BRANCH_W), lambda i: (0, 0))
    return pl.pallas_call(
        body,
        out_shape=(jax.ShapeDtypeStruct((b, t, BRANCH_W), F32),
                   jax.ShapeDtypeStruct((b, 1, BRANCH_W), F32),
                   jax.ShapeDtypeStruct((b, 3, BRANCH_W), F32)),
        grid=(b,),
        in_specs=[pl.BlockSpec((1, t, 2 * BRANCH_W), lambda i: (i, 0, 2)),
                  pl.BlockSpec((1, 1, BRANCH_W), lambda i: (i, 0, 0)),
                  pl.BlockSpec((1, 3, BRANCH_W), lambda i: (i, 0, 0)),
                  pl.BlockSpec((4, BRANCH_W), lambda i: (0, 0)),
                  vec, mat, vec, mat, vec, vec],
        out_specs=(pl.BlockSpec((1, t, BRANCH_W), lambda i: (i, 0, 0)),
                   pl.BlockSpec((1, 1, BRANCH_W), lambda i: (i, 0, 0)),
                   pl.BlockSpec((1, 3, BRANCH_W), lambda i: (i, 0, 0))),
        scratch_shapes=[pltpu.VMEM((tp + 8, 2 * BRANCH_W), F32),
                        pltpu.VMEM((tp, BRANCH_W), F32),
                        pltpu.VMEM((tp, BRANCH_W), F32)],
        compiler_params=_params(("parallel",)),
        name="rglru",
    )(cols3, h0, cbuf, p["a_conv_w"], p["a_conv_b"], p["a_gx_bd"], p["a_gx_b"], p["a_ga_bd"], p["a_ga_b"],
      p["a_lambda"])


RWKV_GROUP = 4


def _rwkv_consts(tc, ch):
    t = np.arange(tc)
    same = (t[:, None] // ch) == (t[None, :] // ch)
    tril = same & (t[None, :] <= t[:, None])
    return jnp.asarray(np.concatenate([tril, same], axis=0).astype(np.float32), BF16)


def _rwkv_body(t_len, tc, n_t, ch, cb_ref, shift_ref, s0_ref, mu_ref, w0_ref, w2_ref, a0_ref, a2_ref, g2_ref,
               kk_ref, ka_ref, rk_ref, lnw_ref, lnb_ref, cs_ref, e3_ref, y_ref, last_ref, sn_ref,
               xs_scr, s_scr, at_scr, bt_scr, kt_scr, bh_scr, kh_scr, v_scr, rt_scr, gc_scr, y_scr, bon_scr,
               g_scr):
    ti = pl.program_id(1)
    t_blk = min(t_len, tc)
    cs = N_HEADS * ch

    @pl.when(ti == 0)
    def _():
        if t_blk != tc:
            xs_scr[...] = jnp.zeros(xs_scr.shape, F32)
        xs_scr[pl.ds(7, 1), :] = shift_ref[0]
        s_scr[...] = jnp.zeros(s_scr.shape, F32)
        for h in range(N_HEADS):
            s_scr[h * HEAD_W:(h + 1) * HEAD_W, h * HEAD_W:(h + 1) * HEAD_W] = s0_ref[0, h]

    xs_scr[pl.ds(8, t_blk), :] = cb_ref[0]
    cb = xs_scr[pl.ds(8, tc), :]
    shifted = xs_scr[pl.ds(7, tc), :]
    cm = cb + (shifted - cb) * mu_ref[...]
    r = cm[:, 0:256]
    k = cm[:, 256:512]
    v = cm[:, 512:768]
    lora = cm[:, 768:896]
    w = -jax.nn.softplus(-(w0_ref[...] + _dot_split(jnp.tanh(lora), w2_ref[...]))) - 0.5
    log_w = -jnp.exp(w)
    a = jax.nn.sigmoid(a0_ref[...] + _dot_split(lora, a2_ref[...]))
    g_scr[...] = _dot_split(jax.nn.sigmoid(cm[:, 896:1024]), g2_ref[...])
    e3 = e3_ref[...]
    kk = k * kk_ref[...]
    kk = kk / jnp.maximum(jnp.sqrt(_head_sum(kk * kk, e3)), 1e-12)
    k2 = k * (1.0 + (a - 1.0) * ka_ref[...])
    bon_scr[...] = _head_sum(r * k2 * rk_ref[...], e3) * v
    a_s = -kk
    b_s = kk * a
    if t_blk != tc:
        row = lax.broadcasted_iota(jnp.int32, (tc, 1), 0)
        live = row < t_blk
        log_w = jnp.where(live, log_w, 0.0)
        a_s = jnp.where(live, a_s, 0.0)
        b_s = jnp.where(live, b_s, 0.0)
        k2 = jnp.where(live, k2, 0.0)
    parts = jnp.concatenate(_split3(log_w), axis=1)
    sub = cs_ref.shape[1]
    cum, tot = [], []
    for j in range(tc // sub):
        cc = _dot(cs_ref[...], parts[j * sub:(j + 1) * sub, :])
        cc = cc[:, 0:256] + cc[:, 256:512] + cc[:, 512:768]
        cum.append(cc[0:sub])
        tot.append(cc[sub:2 * sub])
    cum = jnp.concatenate(cum, axis=0)
    tot = jnp.concatenate(tot, axis=0)
    inv = jnp.exp(-cum)
    tail = jnp.exp(tot - cum)
    at_scr[...] = (a_s * jnp.exp(cum - log_w)).astype(BF16)
    bt_scr[...] = (b_s * inv).astype(BF16)
    kt_scr[...] = (k2 * inv).astype(BF16)
    bh_scr[...] = (b_s * tail).astype(BF16)
    kh_scr[...] = (k2 * tail).astype(BF16)
    v_scr[...] = v.astype(BF16)
    rt_scr[...] = r * jnp.exp(cum)
    gc_scr[...] = jnp.exp(tot)

    row_t = lax.broadcasted_iota(jnp.int32, (cs, cs), 0) & (ch - 1)
    col_t = lax.broadcasted_iota(jnp.int32, (cs, cs), 1) & (ch - 1)
    strict = col_t < row_t
    row_t2 = lax.broadcasted_iota(jnp.int32, (cs, 2 * cs), 0) & (ch - 1)
    col_t2 = lax.broadcasted_iota(jnp.int32, (cs, 2 * cs), 1) & (ch - 1)
    incl2 = col_t2 <= row_t2
    eye = (lax.broadcasted_iota(jnp.int32, (cs, cs), 0) == lax.broadcasted_iota(jnp.int32, (cs, cs), 1)).astype(F32)
    lane_head = lax.broadcasted_iota(jnp.int32, (ch, BRANCH_W), 1) >> HEAD_SHIFT
    n_dbl = ch.bit_length() - 1

    def prepare(c):
        r0 = c * ch if isinstance(c, int) else pl.multiple_of(c * ch, ch)
        abd, bbd, kbd, bhd, khd, vbd = (_stack_heads(s[pl.ds(r0, ch), :], lane_head)
                                        for s in (at_scr, bt_scr, kt_scr, bh_scr, kh_scr, v_scr))
        rbd = _stack_heads(rt_scr[pl.ds(r0, ch), :], lane_head)
        s1 = _dot_nt(jnp.concatenate([abd, rbd.astype(BF16)], axis=0), jnp.concatenate([bbd, kbd], axis=0))
        l_ab = jnp.where(strict, s1[0:cs, 0:cs], 0.0)
        l_ak = jnp.where(strict, s1[0:cs, cs:2 * cs], 0.0)
        m_r = jnp.where(incl2, s1[cs:2 * cs, :], 0.0).astype(BF16)
        tinv = eye + l_ab
        lb16 = l_ab.astype(BF16)
        yield
        pw = _dot(lb16, lb16)
        for i in range(1, n_dbl):
            pw16 = pw.astype(BF16)
            yield
            if i < n_dbl - 1:
                both = _dot(jnp.concatenate([tinv.astype(BF16), pw16], axis=0), pw16)
                tinv = tinv + both[0:cs]
                pw = both[cs:2 * cs]
            else:
                tinv = tinv + _dot(tinv.astype(BF16), pw16)
        yield
        w1 = _dot(l_ak.astype(BF16), vbd)
        yield
        x = _dot(tinv.astype(BF16), jnp.concatenate([abd, w1.astype(BF16)], axis=1))
        yield
        z = _dot(m_r[:, 0:cs], x.astype(BF16))
        r_hat = rbd + z[:, 0:BRANCH_W]
        yield
        y_hat = z[:, BRANCH_W:2 * BRANCH_W] + _dot(m_r[:, cs:2 * cs], vbd)
        ar = jnp.concatenate([x[:, 0:BRANCH_W], r_hat], axis=0).astype(BF16)
        g_end = gc_scr[pl.ds(r0, ch), :][0:1, :]
        return ar, x[:, BRANCH_W:2 * BRANCH_W], y_hat, vbd, jnp.concatenate([bhd, khd], axis=0), g_end

    def advance(c0, prepared):
        for n, (ar, u_hat, y_hat, vbd, bk, g_end) in enumerate(prepared):
            c = c0 + n
            r0 = c * ch if isinstance(c, int) else pl.multiple_of(c * ch, ch)
            sb = s_scr[...]
            uy = _dot_nt(ar, sb.astype(BF16))
            u = uy[0:cs] + u_hat
            ys = uy[cs:2 * cs] + y_hat
            y_scr[pl.ds(r0, ch), :] = ys[0:ch] + ys[ch:2 * ch] + ys[2 * ch:3 * ch] + ys[3 * ch:4 * ch]
            yield
            s_scr[...] = sb * g_end + _dot_tn(jnp.concatenate([u.astype(BF16), vbd], axis=0), bk)
            yield

    def interleave(gens):
        done = [None] * len(gens)
        live = list(range(len(gens)))
        while live:
            for i in list(live):
                try:
                    next(gens[i])
                except StopIteration as stop:
                    done[i] = stop.value
                    live.remove(i)
        return done

    n_chunk = tc // ch
    grp = RWKV_GROUP if n_chunk % RWKV_GROUP == 0 else 1
    first = tuple(interleave([prepare(c) for c in range(grp)]))
    if n_chunk > grp:
        def group(i, carry):
            c0 = i * grp
            res = interleave([prepare(c0 + grp + n) for n in range(grp)] + [advance(c0, carry)])
            return tuple(res[0:grp])

        first = lax.fori_loop(0, n_chunk // grp - 1, group, first)
    interleave([advance(n_chunk - grp, first)])

    y = y_scr[...]
    mean = _head_sum(y, e3) * (1.0 / HEAD_W)
    dev = y - mean
    var = _head_sum(dev * dev, e3) * (1.0 / HEAD_W)
    yn = dev * lax.rsqrt(var + B_GN_EPS) * lnw_ref[...] + lnb_ref[...]
    out = (yn + bon_scr[...]) * g_scr[...]
    y_ref[0] = out[0:t_blk]
    xs_scr[pl.ds(7, 1), :] = xs_scr[pl.ds(8 + t_blk - 1, 1), :]

    @pl.when(ti == n_t - 1)
    def _():
        last_ref[0] = xs_scr[pl.ds(7, 1), :]
        for h in range(N_HEADS):
            sn_ref[0, h] = s_scr[h * HEAD_W:(h + 1) * HEAD_W, h * HEAD_W:(h + 1) * HEAD_W]


def _rwkv(cols3, shift, s0, p):
    b, t, _ = cols3.shape
    if t >= 64:
        ch, tc = 64, min(t, 1024)
        assert t % tc == 0
    else:
        ch = 16
        tc = _round_up(t, ch)
    n_t = max(1, t // tc)
    t_blk = min(t, tc)
    body = functools.partial(_rwkv_body, t, tc, n_t, ch)
    consts = _rwkv_consts(min(tc, 256), ch)
    vec = lambda w: pl.BlockSpec((1, w), lambda i, j: (0, 0))
    sc16 = lambda: pltpu.VMEM((tc, BRANCH_W), BF16)
    sc32 = lambda: pltpu.VMEM((tc, BRANCH_W), F32)
    return pl.pallas_call(
        body,
        out_shape=(jax.ShapeDtypeStruct((b, t, BRANCH_W), F32),
                   jax.ShapeDtypeStruct((b, 1, B_COLS), F32),
                   jax.ShapeDtypeStruct((b, N_HEADS, HEAD_W, HEAD_W), F32)),
        grid=(b, n_t),
        in_specs=[pl.BlockSpec((1, t_blk, B_COLS), lambda i, j: (i, j, 0)),
                  pl.BlockSpec((1, 1, B_COLS), lambda i, j: (i, 0, 0)),
                  pl.BlockSpec((1, N_HEADS, HEAD_W, HEAD_W), lambda i, j: (i, 0, 0, 0)),
                  vec(B_COLS), vec(BRANCH_W),
                  pl.BlockSpec((384, BRANCH_W), lambda i, j: (0, 0)),
                  vec(BRANCH_W),
                  pl.BlockSpec((384, BRANCH_W), lambda i, j: (0, 0)),
                  pl.BlockSpec((384, BRANCH_W), lambda i, j: (0, 0)),
                  vec(BRANCH_W), vec(BRANCH_W), vec(BRANCH_W), vec(BRANCH_W), vec(BRANCH_W),
                  _resident(consts.shape), _resident(p["e3"].shape)],
        out_specs=(pl.BlockSpec((1, t_blk, BRANCH_W), lambda i, j: (i, j, 0)),
                   pl.BlockSpec((1, 1, B_COLS), lambda i, j: (i, 0, 0)),
                   pl.BlockSpec((1, N_HEADS, HEAD_W, HEAD_W), lambda i, j: (i, 0, 0, 0))),
        scratch_shapes=[pltpu.VMEM((tc + 8, B_COLS), F32),
                        pltpu.VMEM((BRANCH_W, BRANCH_W), F32),
                        sc16(), sc16(), sc16(), sc16(), sc16(), sc16(),
                        sc32(), sc32(), sc32(), sc32(), sc32()],
        compiler_params=_params(("parallel", "arbitrary")),
        name="rwkv7",
    )(cols3, shift, s0, p["b_mu"], p["b_w0"], p["b_w2p"], p["b_a0"], p["b_a2p"], p["b_g2"],
      p["b_k_k"], p["b_k_a"], p["b_r_k"], p["b_ln_w"], p["b_ln_b"], consts, p["e3"])


def _hgrn_consts(tc):
    nl = tc.bit_length() - 1
    t = np.arange(tc)
    sel = np.zeros((nl + 1, tc, tc), np.float32)
    for l in range(nl):
        m = 1 << l
        ref = (t // (2 * m)) * (2 * m) + m - 1
        j = t[None, :]
        later = (t > ref)[:, None] & (j > ref[:, None]) & (j <= t[:, None])
        earlier = (t <= ref)[:, None] & (j > t[:, None]) & (j <= ref[:, None])
        sel[l] = later | earlier
    sel[nl] = t[None, :] <= t[:, None]
    x = t[:, None] ^ t[None, :]
    lvl = np.floor(np.log2(np.maximum(x, 1))).astype(np.int32)
    lvl = np.where(t[:, None] > t[None, :], lvl, np.where(t[:, None] == t[None, :], nl, -1))
    return jnp.asarray(sel.reshape((nl + 1) * tc, tc), BF16), jnp.asarray(lvl, jnp.int32), nl


def _hgrn_body(t_len, tc, n_t, nl, cq_ref, cf_ref, cig_ref, s0_ref, lb_ref, ng_ref, sel_ref, lvl_ref, e3_ref,
               y_ref, sn_ref, in_scr, st_scr):
    ti = pl.program_id(1)
    t_blk = min(t_len, tc)

    @pl.when(ti == 0)
    def _():
        st_scr[...] = jnp.zeros(st_scr.shape, F32)
        for h in range(N_HEADS):
            st_scr[h * HEAD_W:(h + 1) * HEAD_W, h * C_EXPAND:(h + 1) * C_EXPAND] = s0_ref[0, h].T

    if t_blk != tc:
        in_scr[...] = jnp.zeros(in_scr.shape, F32)
        in_scr[0, pl.ds(0, t_blk), :] = cq_ref[0]
        in_scr[1, pl.ds(0, t_blk), :] = cf_ref[0]
        in_scr[2, pl.ds(0, t_blk), :] = cig_ref[0]
        q, f, cig = in_scr[0], in_scr[1], in_scr[2]
    else:
        q, f, cig = cq_ref[0], cf_ref[0], cig_ref[0]
    lb = lb_ref[...]
    fg = lb + (1.0 - lb) * jax.nn.sigmoid(f)
    log_f = jnp.log(fg)
    kk = 1.0 - fg
    if t_blk != tc:
        row = lax.broadcasted_iota(jnp.int32, (tc, 1), 0)
        live = row < t_blk
        log_f = jnp.where(live, log_f, 0.0)
        kk = jnp.where(live, kk, 0.0)
    hi = log_f.astype(BF16)
    lo = (log_f - hi.astype(F32)).astype(BF16)
    parts = jnp.concatenate([hi, lo], axis=1)

    def seg_sum(l):
        d = _dot(sel_ref[pl.ds(l * tc, tc), :], parts)
        return d[:, 0:C_FDIM] + d[:, C_FDIM:2 * C_FDIM]

    lvl = lvl_ref[...]
    att = [None] * N_HEADS
    for l in range(nl + 1):
        if l < nl:
            e = jnp.exp(seg_sum(l))
            qs = (q * e).astype(BF16)
            ks = (kk * e).astype(BF16)
        else:
            qs = q.astype(BF16)
            ks = kk.astype(BF16)
        hit = lvl == l
        for h in range(N_HEADS):
            kl = slice(h * C_EXPAND, (h + 1) * C_EXPAND)
            a = _dot_nt(qs[:, kl], ks[:, kl])
            att[h] = jnp.where(hit, a, 0.0 if att[h] is None else att[h])
    cum = seg_sum(nl)
    c_last = cum[tc - 1:tc, :]
    lane_head = lax.broadcasted_iota(jnp.int32, (tc, BRANCH_W), 1) >> HEAD_SHIFT
    v = cig[:, 0:BRANCH_W]
    v_stack = _stack_heads(v, lane_head).astype(BF16)
    p_cat = jnp.concatenate([a.astype(BF16) for a in att], axis=1)
    st = st_scr[...]
    o = _dot(p_cat, v_stack) + _dot_nt((q * jnp.exp(cum)).astype(BF16), st.astype(BF16))
    ms = _head_sum(o * o, e3_ref[...]) * (1.0 / HEAD_W)
    y = o * lax.rsqrt(ms + NORM_EPS) * ng_ref[...] * jax.nn.silu(cig[:, BRANCH_W:2 * BRANCH_W])
    y_ref[0] = y[0:t_blk]
    upd = _dot_tn(v.astype(BF16), (kk * jnp.exp(c_last - cum)).astype(BF16))
    rh = lax.broadcasted_iota(jnp.int32, (BRANCH_W, C_FDIM), 0) >> HEAD_SHIFT
    ch_ = lax.broadcasted_iota(jnp.int32, (BRANCH_W, C_FDIM), 1) >> C_EXPAND_SHIFT
    st_scr[...] = st * jnp.exp(c_last) + jnp.where(rh == ch_, upd, 0.0)

    @pl.when(ti == n_t - 1)
    def _():
        for h in range(N_HEADS):
            sn_ref[0, h] = st_scr[h * HEAD_W:(h + 1) * HEAD_W, h * C_EXPAND:(h + 1) * C_EXPAND].T


def _hgrn(cols3, s0, lb, ng, e3):
    b, t, _ = cols3.shape
    tc = 256 if t >= 256 else 16
    assert t % tc == 0 or t < tc
    n_t = max(1, t // tc)
    t_blk = min(t, tc)
    sel, lvl, nl = _hgrn_consts(tc)
    body = functools.partial(_hgrn_body, t, tc, n_t, nl)
    blk = lambda idx: pl.BlockSpec((1, t_blk, C_FDIM), lambda i, j: (i, j, idx))
    pad_shape = (3, tc, C_FDIM) if t_blk != tc else (1, 8, 128)
    return pl.pallas_call(
        body,
        out_shape=(jax.ShapeDtypeStruct((b, t, BRANCH_W), F32),
                   jax.ShapeDtypeStruct((b, N_HEADS, C_EXPAND, HEAD_W), F32)),
        grid=(b, n_t),
        in_specs=[blk(3), blk(4), blk(5),
                  pl.BlockSpec((1, N_HEADS, C_EXPAND, HEAD_W), lambda i, j: (i, 0, 0, 0)),
                  pl.BlockSpec((1, C_FDIM), lambda i, j: (0, 0)),
                  pl.BlockSpec((1, BRANCH_W), lambda i, j: (0, 0)),
                  _resident(sel.shape), _resident(lvl.shape), _resident(e3.shape)],
        out_specs=(pl.BlockSpec((1, t_blk, BRANCH_W), lambda i, j: (i, j, 0)),
                   pl.BlockSpec((1, N_HEADS, C_EXPAND, HEAD_W), lambda i, j: (i, 0, 0, 0))),
        scratch_shapes=[pltpu.VMEM(pad_shape, F32),
                        pltpu.VMEM((BRANCH_W, C_FDIM), F32)],
        compiler_params=_params(("parallel", "arbitrary")),
        name="hgrn2",
    )(cols3, cols3, cols3, s0, lb, ng, sel, lvl, e3)


def _rope_rows(x, cos, sin_signed):
    lane = lax.broadcasted_iota(jnp.int32, x.shape, 1)
    swapped = jnp.where((lane & (HEAD_W - 1)) < HEAD_W // 2,
                        pltpu.roll(x, BRANCH_W - HEAD_W // 2, 1), pltpu.roll(x, HEAD_W // 2, 1))
    return x * cos + swapped * sin_signed


def _pattern_weight(d):
    ok = d >= 0
    w = ((d <= 128).astype(F32) + (((d & 3) == 0) & (d <= 512)).astype(F32)
         + (((d & 15) == 0) & (d <= 2048)).astype(F32))
    return jnp.where(ok, w, 0.0)


def _attn_prompt_body(n_blk, qkv_ref, cos_ref, sin_ref, y_ref, kt_ref, vt_ref, q_scr, kb_scr, vh_scr, s_scr):
    t = n_blk * ATT_BLK
    qkv = qkv_ref[0]
    cos = cos_ref[...]
    sin = sin_ref[...]
    q = _rope_rows(qkv[:, 0:BRANCH_W], cos, sin) * (HEAD_W ** -0.5)
    k = _rope_rows(qkv[:, BRANCH_W:2 * BRANCH_W], cos, sin)
    q_scr[...] = q
    kb_scr[...] = k.astype(BF16)
    v = qkv[:, 2 * BRANCH_W:3 * BRANCH_W]
    kt_ref[0] = k.T
    vt_ref[0] = v.T
    lane_head = lax.broadcasted_iota(jnp.int32, (ATT_BLK, BRANCH_W), 1) >> HEAD_SHIFT
    for j in range(n_blk):
        vj = v[j * ATT_BLK:(j + 1) * ATT_BLK, :]
        for h in range(N_HEADS):
            vh_scr[j, pl.ds(h * ATT_BLK, ATT_BLK), :] = jnp.where(lane_head == h, vj, 0.0).astype(BF16)

    rr = lax.broadcasted_iota(jnp.int32, (ATT_BLK, ATT_BLK), 0)
    cc = lax.broadcasted_iota(jnp.int32, (ATT_BLK, ATT_BLK), 1)
    rel = rr - cc

    def q_block(i, _):
        q0 = pl.multiple_of(i * ATT_BLK, ATT_BLK)
        qi = q_scr[pl.ds(q0, ATT_BLK), :]
        q_bd = jnp.concatenate([jnp.where(lane_head == h, qi, 0.0) for h in range(N_HEADS)], axis=0).astype(BF16)

        def weights(j):
            w1 = _pattern_weight((i - j) * ATT_BLK + rel)
            return jnp.concatenate([w1] * N_HEADS, axis=0)

        def pass1(j, m_run):
            k0 = pl.multiple_of(j * ATT_BLK, ATT_BLK)
            s = _dot_nt(q_bd, kb_scr[pl.ds(k0, ATT_BLK), :])
            s = jnp.where(weights(j) > 0.0, s, NEG_BIG)
            s_scr[j] = s
            return jnp.maximum(m_run, s)

        m_run = lax.fori_loop(0, i + 1, pass1, jnp.full((N_HEADS * ATT_BLK, ATT_BLK), NEG_BIG, F32))
        m = jnp.max(m_run, axis=-1, keepdims=True)

        def pass2(j, carry):
            l_run, acc = carry
            p = weights(j) * jnp.exp(s_scr[j] - m)
            p_cat = jnp.concatenate([p[h * ATT_BLK:(h + 1) * ATT_BLK, :] for h in range(N_HEADS)], axis=1)
            acc = acc + _dot(p_cat.astype(BF16), vh_scr[j])
            return l_run + p, acc

        l_run, acc = lax.fori_loop(0, i + 1, pass2, (jnp.zeros((N_HEADS * ATT_BLK, ATT_BLK), F32),
                                                     jnp.zeros((ATT_BLK, BRANCH_W), F32)))
        l = jnp.sum(l_run, axis=-1, keepdims=True)
        den = jnp.zeros((ATT_BLK, BRANCH_W), F32)
        for h in range(N_HEADS):
            den = jnp.where(lane_head == h, l[h * ATT_BLK:(h + 1) * ATT_BLK, :], den)
        y_ref[0, pl.ds(q0, ATT_BLK), :] = acc / den
        return 0

    lax.fori_loop(0, n_blk, q_block, 0)


def _attn_prompt(cols3, cos, sin):
    b, t, _ = cols3.shape
    n_blk = t // ATT_BLK
    assert n_blk * ATT_BLK == t
    body = functools.partial(_attn_prompt_body, n_blk)
    return pl.pallas_call(
        body,
        out_shape=(jax.ShapeDtypeStruct((b, t, BRANCH_W), F32), jax.ShapeDtypeStruct((b, BRANCH_W, t), F32),
                   jax.ShapeDtypeStruct((b, BRANCH_W, t), F32)),
        grid=(b,),
        in_specs=[pl.BlockSpec((1, t, 3 * BRANCH_W), lambda i: (i, 0, 4)),
                  _resident((t, BRANCH_W)), _resident((t, BRANCH_W))],
        out_specs=(pl.BlockSpec((1, t, BRANCH_W), lambda i: (i, 0, 0)),
                   pl.BlockSpec((1, BRANCH_W, t), lambda i: (i, 0, 0)),
                   pl.BlockSpec((1, BRANCH_W, t), lambda i: (i, 0, 0))),
        scratch_shapes=[pltpu.VMEM((t, BRANCH_W), F32),
                        pltpu.VMEM((t, BRANCH_W), BF16),
                        pltpu.VMEM((n_blk, N_HEADS * ATT_BLK, BRANCH_W), BF16),
                        pltpu.VMEM((n_blk, N_HEADS * ATT_BLK, ATT_BLK), F32)],
        compiler_params=_params(("parallel",)),
        name="attn_prompt",
    )(cols3, cos, sin)


def _attn_sample_body(t_len, l_cache, qkv_ref, kc_ref, vc_ref, cos_ref, sin_ref, y_ref, k_ref, pad_scr):
    tp = 8
    pad_scr[...] = jnp.zeros(pad_scr.shape, F32)
    pad_scr[pl.ds(0, t_len), :] = qkv_ref[0]
    qkv = pad_scr[...]
    cos = cos_ref[...]
    sin = sin_ref[...]
    q = _rope_rows(qkv[:, 0:BRANCH_W], cos, sin) * (HEAD_W ** -0.5)
    k_new = _rope_rows(qkv[:, BRANCH_W:2 * BRANCH_W], cos, sin)
    v_new = qkv[:, 2 * BRANCH_W:3 * BRANCH_W]
    k_ref[0] = k_new[0:t_len]
    lane_head = lax.broadcasted_iota(jnp.int32, (tp, BRANCH_W), 1) >> HEAD_SHIFT
    q_bd = jnp.concatenate([jnp.where(lane_head == h, q, 0.0) for h in range(N_HEADS)], axis=0).astype(BF16)
    n_row = N_HEADS * tp
    k_t = kc_ref[0, 0].reshape(BRANCH_W, l_cache)
    v_t = vc_ref[0, 0].reshape(BRANCH_W, l_cache)
    s_c = _dot(q_bd, k_t.astype(BF16))
    s_n = _dot_nt(q_bd, k_new.astype(BF16))
    t_row = lax.broadcasted_iota(jnp.int32, (n_row, 1), 0) & (tp - 1)
    col_c = lax.broadcasted_iota(jnp.int32, (n_row, l_cache), 1)
    w_c = _pattern_weight(l_cache + t_row - col_c)
    col_n = lax.broadcasted_iota(jnp.int32, (n_row, tp), 1)
    w_n = jnp.where(col_n < t_len, _pattern_weight(t_row - col_n), 0.0)
    s_c = jnp.where(w_c > 0.0, s_c, NEG_BIG)
    s_n = jnp.where(w_n > 0.0, s_n, NEG_BIG)
    m = jnp.maximum(jnp.max(s_c, axis=-1, keepdims=True), jnp.max(s_n, axis=-1, keepdims=True))
    p_c = w_c * jnp.exp(s_c - m)
    p_n = w_n * jnp.exp(s_n - m)
    l = jnp.sum(p_c, axis=-1, keepdims=True) + jnp.sum(p_n, axis=-1, keepdims=True)
    num = _dot_nt(p_c.astype(BF16), v_t.astype(BF16)) + _dot(p_n.astype(BF16), v_new.astype(BF16))
    ratio = num / l
    out = jnp.zeros((tp, BRANCH_W), F32)
    for h in range(N_HEADS):
        out = jnp.where(lane_head == h, ratio[h * tp:(h + 1) * tp, :], out)
    y_ref[0] = out[0:t_len]


def _attn_sample(cols3, k_cache_t, v_cache_t, layer, cos, sin):
    b, t, _ = cols3.shape
    l_cache = k_cache_t.shape[-1]
    assert t <= 8
    body = functools.partial(_attn_sample_body, t, l_cache)
    cache_spec = pl.BlockSpec((1, 1, N_HEADS, HEAD_W, l_cache), lambda i: (layer, i, 0, 0, 0))
    return pl.pallas_call(
        body,
        out_shape=(jax.ShapeDtypeStruct((b, t, BRANCH_W), F32), jax.ShapeDtypeStruct((b, t, BRANCH_W), F32)),
        grid=(b,),
        in_specs=[pl.BlockSpec((1, t, 3 * BRANCH_W), lambda i: (i, 0, 4)),
                  cache_spec, cache_spec,
                  pl.BlockSpec((8, BRANCH_W), lambda i: (0, 0)),
                  pl.BlockSpec((8, BRANCH_W), lambda i: (0, 0))],
        out_specs=(pl.BlockSpec((1, t, BRANCH_W), lambda i: (i, 0, 0)),
                   pl.BlockSpec((1, t, BRANCH_W), lambda i: (i, 0, 0))),
        scratch_shapes=[pltpu.VMEM((8, 3 * BRANCH_W), F32)],
        compiler_params=_params(("parallel",)),
        name="attn_sample",
    )(cols3, k_cache_t, v_cache_t, cos, sin)


def _merge_body(x_ref, g_ref, wgt_ref, ya_ref, yb_ref, yc_ref, yd_ref, wb_ref, wo_ref, o_ref):
    x = x_ref[...]
    u = _rms_rows(x, g_ref[...]).astype(BF16)
    merged = None
    for n, y_ref in enumerate((ya_ref, yb_ref, yc_ref, yd_ref)):
        z = _dot(y_ref[...].astype(BF16), wb_ref[n])
        gate = jax.nn.sigmoid(_dot(u, wgt_ref[:, n * D_MODEL:(n + 1) * D_MODEL]))
        merged = gate * z if merged is None else merged + gate * z
    o_ref[...] = x + _dot(merged.astype(BF16), wo_ref[...])


def _merge(x2d, g, wgt, ys, wb, wo):
    n, d = x2d.shape
    tm = min(n, 512)
    row = lambda w: pl.BlockSpec((tm, w), lambda i: (i, 0))
    return pl.pallas_call(
        _merge_body,
        out_shape=jax.ShapeDtypeStruct((n, d), F32),
        grid=(n // tm,),
        in_specs=[row(d), pl.BlockSpec((1, d), lambda i: (0, 0)), _resident(wgt.shape),
                  row(BRANCH_W), row(BRANCH_W), row(BRANCH_W), row(BRANCH_W),
                  _resident(wb.shape), _resident(wo.shape)],
        out_specs=row(d),
        compiler_params=_params(("parallel",)),
        name="merge",
    )(x2d, g, wgt, *ys, wb, wo)


FFN_COL = 1024


def _ffn_body(t_len, tm, n_t, x_ref, g_ref, wg_ref, wu_ref, wd_ref, cw_ref, cb_ref, st_ref, o_ref, ns_ref,
              pad_scr, carry_scr):
    ti = pl.program_id(1)
    t_blk = min(t_len, tm)

    @pl.when(ti == 0)
    def _():
        carry_scr[...] = st_ref[0]

    if t_blk != tm:
        pad_scr[...] = jnp.zeros(pad_scr.shape, F32)
        pad_scr[pl.ds(0, t_blk), :] = x_ref[0]
        x = pad_scr[...]
    else:
        x = x_ref[0]
    v = _rms_rows(x, g_ref[...]).astype(BF16)
    row = lax.broadcasted_iota(jnp.int32, (tm, 1), 0)
    acc = jnp.zeros((tm, D_MODEL), F32)
    cw = cw_ref[...]
    for c in range(D_FF // FFN_COL):
        sl = slice(c * FFN_COL, (c + 1) * FFN_COL)
        hg = _dot(v, wg_ref[:, sl])
        hu = _dot(v, wu_ref[:, sl])
        p2 = carry_scr[0:1, sl]
        p1 = carry_scr[1:2, sl]
        h1 = jnp.where(row == 0, p1, pltpu.roll(hg, 1, 0))
        h2 = jnp.where(row == 0, p2, jnp.where(row == 1, p1, pltpu.roll(hg, 2, 0)))
        conv = cb_ref[:, sl] + cw[2:3, sl] * hg + cw[1:2, sl] * h1 + cw[0:1, sl] * h2
        hmid = (jax.nn.gelu(conv) * hu).astype(BF16)
        acc = acc + _dot(hmid, wd_ref[sl, :])
        carry_scr[:, sl] = hg[t_blk - 2:t_blk, :]
    y = x + acc
    o_ref[0] = y[0:t_blk]

    @pl.when(ti == n_t - 1)
    def _():
        ns_ref[0] = carry_scr[...]


def _ffn(x3, g, wg, wu, wd, cw, cb, state):
    b, t, d = x3.shape
    tm = min(_round_up(t, 8), 512)
    n_t = max(1, t // tm)
    t_blk = min(t, tm)
    assert t_blk * n_t == t and t_blk >= 2
    body = functools.partial(_ffn_body, t, tm, n_t)
    return pl.pallas_call(
        body,
        out_shape=(jax.ShapeDtypeStruct((b, t, d), F32), jax.ShapeDtypeStruct((b, 2, D_FF), F32)),
        grid=(b, n_t),
        in_specs=[pl.BlockSpec((1, t_blk, d), lambda i, j: (i, j, 0)),
                  pl.BlockSpec((1, d), lambda i, j: (0, 0)),
                  _resident(wg.shape), _resident(wu.shape), _resident(wd.shape),
                  pl.BlockSpec((3, D_FF), lambda i, j: (0, 0)),
                  pl.BlockSpec((1, D_FF), lambda i, j: (0, 0)),
                  pl.BlockSpec((1, 2, D_FF), lambda i, j: (i, 0, 0))],
        out_specs=(pl.BlockSpec((1, t_blk, d), lambda i, j: (i, j, 0)),
                   pl.BlockSpec((1, 2, D_FF), lambda i, j: (i, 0, 0))),
        scratch_shapes=[pltpu.VMEM((tm, d), F32), pltpu.VMEM((2, D_FF), F32)],
        compiler_params=_params(("parallel", "arbitrary")),
        name="convffn",
    )(x3, g, wg, wu, wd, cw, cb, state)


def _norm_body(x_ref, g_ref, o_ref):
    o_ref[...] = _rms_rows(x_ref[...], g_ref[...])


def _final_norm(x2d, g):
    n, d = x2d.shape
    tm = min(n, 1024)
    return pl.pallas_call(
        _norm_body,
        out_shape=jax.ShapeDtypeStruct((n, d), F32),
        grid=(n // tm,),
        in_specs=[pl.BlockSpec((tm, d), lambda i: (i, 0)), pl.BlockSpec((1, d), lambda i: (0, 0))],
        out_specs=pl.BlockSpec((tm, d), lambda i: (i, 0)),
        compiler_params=_params(("parallel",)),
        name="final_norm",
    )(x2d, g)


def _block_diag(w):
    out = jnp.zeros((BRANCH_W, BRANCH_W), F32)
    for h in range(N_HEADS):
        out = out.at[h * HEAD_W:(h + 1) * HEAD_W, h * HEAD_W:(h + 1) * HEAD_W].set(w[h])
    return out


def _head_sum_matrix3():
    i = np.arange(BRANCH_W) // HEAD_W
    e = (i[:, None] == i[None, :]).astype(np.float32)
    return jnp.asarray(np.concatenate([e, e, e], axis=0), BF16)


def _rope_tables(pos, rows):
    half = HEAD_W // 2
    inv = ROPE_THETA ** (-jnp.arange(half, dtype=F32) / half)
    ang = pos.astype(F32)[:, None] * inv[None, :]
    cos = jnp.cos(ang)
    sin = jnp.sin(ang)
    cos_t = jnp.tile(jnp.concatenate([cos, cos], axis=1), (1, N_HEADS))
    sin_t = jnp.tile(jnp.concatenate([-sin, sin], axis=1), (1, N_HEADS))
    pad = rows - pos.shape[0]
    if pad:
        cos_t = jnp.pad(cos_t, ((0, pad), (0, 0)))
        sin_t = jnp.pad(sin_t, ((0, pad), (0, 0)))
    return cos_t, sin_t


def _layer_params(l, W):
    w_in = W["w_in"][l]
    o = 0
    parts = {}
    for name, width in (("ax", 256), ("ag", 256), ("cb", 1024), ("cq", 512), ("cf", 512), ("ci", 256),
                        ("cg", 256), ("dq", 256), ("dk", 256), ("dv", 256), ("gt", 4096)):
        parts[name] = w_in[:, o:o + width]
        o += width
    order = ("cb", "ax", "ag", "cq", "cf", "ci", "cg", "dq", "dk", "dv")
    p = {
        "w_mix": jnp.concatenate([parts[n] for n in order], axis=1).astype(BF16),
        "w_gt": parts["gt"].astype(BF16),
        "norm1_g": W["norm1_g"][l][None, :],
        "a_conv_w": W["a_conv_w"][l], "a_conv_b": W["a_conv_b"][l][None, :],
        "a_gx_bd": _split_weight(_block_diag(W["a_gx_w"][l])), "a_gx_b": W["a_gx_b"][l][None, :],
        "a_ga_bd": _split_weight(_block_diag(W["a_ga_w"][l])), "a_ga_b": W["a_ga_b"][l][None, :],
        "a_lambda": W["a_lambda"][l][None, :],
        "b_mu": W["b_mu"][l][None, :], "b_w0": W["b_w0"][l][None, :], "b_a0": W["b_a0"][l][None, :],
        "b_w2p": _split_weight(jnp.concatenate([W["b_w2"][l], jnp.zeros((64, BRANCH_W), F32)], axis=0)),
        "b_a2p": _split_weight(jnp.concatenate([jnp.zeros((64, BRANCH_W), F32), W["b_a2"][l]], axis=0)),
        "b_g2": _split_weight(W["b_g2"][l]),
        "b_k_k": W["b_k_k"][l][None, :], "b_k_a": W["b_k_a"][l][None, :],
        "b_r_k": W["b_r_k"][l].reshape(1, BRANCH_W),
        "b_ln_w": W["b_ln_w"][l][None, :], "b_ln_b": W["b_ln_b"][l][None, :],
        "c_norm_g": jnp.tile(W["c_norm_g"][l], N_HEADS)[None, :],
        "w_branch": W["w_branch"][l].astype(BF16), "w_out": W["w_out"][l].astype(BF16),
        "norm2_g": W["norm2_g"][l][None, :],
        "ffn_w_gate": W["ffn_w_gate"][l].astype(BF16), "ffn_w_up": W["ffn_w_up"][l].astype(BF16),
        "ffn_w_down": W["ffn_w_down"][l].astype(BF16),
        "ffn_conv_w": W["ffn_conv_w"][l], "ffn_conv_b": W["ffn_conv_b"][l][None, :],
        "e3": _head_sum_matrix3(),
    }
    return p


def _trunk(x, pos0, st, W, params, lb_all):
    b, t, d = x.shape
    depth = W["w_in"].shape[0]
    has_cache = st["k"] is not None
    if has_cache:
        cos, sin = _rope_tables(pos0 + jnp.arange(t), 8)
        k_cache_t = jnp.transpose(st["k"], (0, 1, 3, 4, 2))
        v_cache_t = jnp.transpose(st["v"], (0, 1, 3, 4, 2))
    else:
        cos, sin = _rope_tables(pos0 + jnp.arange(t), t)
    outs = {n: [] for n in ("ha", "ca", "wkv", "sh", "sc", "k", "v", "cf")}
    for l in range(depth):
        p = params[l]
        cols = _inproj(x.reshape(b * t, d), p["norm1_g"], p["w_mix"]).reshape(b, t, N_MIX_COLS)
        y_a, h_new, ca_new = _rglru(cols, st["ha"][l][:, None, :], st["ca"][l], p)
        y_b, sh_new, wkv_new = _rwkv(cols, st["sh"][l][:, None, :], st["wkv"][l], p)
        y_c, sc_new = _hgrn(cols, st["sc"][l], lb_all[l][None, :], p["c_norm_g"], p["e3"])
        if has_cache:
            y_d, k_rows = _attn_sample(cols, k_cache_t, v_cache_t, l, cos, sin)
            k_out = k_rows.reshape(b, t, N_HEADS, HEAD_W)
            v_out = cols[:, :, N_MIX_COLS - BRANCH_W:].reshape(b, t, N_HEADS, HEAD_W)
        else:
            y_d, k_t, v_t = _attn_prompt(cols, cos, sin)
            keep = min(D_WIN_MAX, t)
            k_out = jnp.transpose(k_t.reshape(b, N_HEADS, HEAD_W, t)[..., t - keep:], (0, 3, 1, 2))
            v_out = jnp.transpose(v_t.reshape(b, N_HEADS, HEAD_W, t)[..., t - keep:], (0, 3, 1, 2))
        ys = [y.reshape(b * t, BRANCH_W) for y in (y_a, y_b, y_c, y_d)]
        x1 = _merge(x.reshape(b * t, d), p["norm1_g"], p["w_gt"], ys, p["w_branch"], p["w_out"])
        x, cf_new = _ffn(x1.reshape(b, t, d), p["norm2_g"], p["ffn_w_gate"], p["ffn_w_up"], p["ffn_w_down"],
                         p["ffn_conv_w"], p["ffn_conv_b"], st["cf"][l])
        outs["ha"].append(h_new[:, 0, :])
        outs["ca"].append(ca_new)
        outs["wkv"].append(wkv_new)
        outs["sh"].append(sh_new[:, 0, :])
        outs["sc"].append(sc_new)
        outs["k"].append(k_out)
        outs["v"].append(v_out)
        outs["cf"].append(cf_new)
    y = _final_norm(x.reshape(b * t, d), W["final_norm_g"][None, :]).reshape(b, t, d)
    return (y,) + tuple(jnp.stack(outs[n]) for n in ("ha", "ca", "wkv", "sh", "sc", "k", "v", "cf"))


def kernel(x_prompt, x_sample, state_a_h, state_a_conv, state_b_wkv, state_b_shift, state_c_s, cache_d_k, cache_d_v, state_ffn_conv, norm1_g, w_in, a_conv_w, a_conv_b, a_gx_w, a_gx_b, a_ga_w, a_ga_b, a_lambda, b_mu, b_w0, b_w2, b_a0, b_a2, b_g2, b_k_k, b_k_a, b_r_k, b_ln_w, b_ln_b, c_lb, c_norm_g, w_branch, w_out, norm2_g, ffn_w_gate, ffn_w_up, ffn_conv_w, ffn_conv_b, ffn_w_down, final_norm_g):
    W = {"norm1_g": norm1_g, "w_in": w_in, "a_conv_w": a_conv_w, "a_conv_b": a_conv_b, "a_gx_w": a_gx_w,
         "a_gx_b": a_gx_b, "a_ga_w": a_ga_w, "a_ga_b": a_ga_b, "a_lambda": a_lambda, "b_mu": b_mu,
         "b_w0": b_w0, "b_w2": b_w2, "b_a0": b_a0, "b_a2": b_a2, "b_g2": b_g2, "b_k_k": b_k_k,
         "b_k_a": b_k_a, "b_r_k": b_r_k, "b_ln_w": b_ln_w, "b_ln_b": b_ln_b, "c_lb": c_lb,
         "c_norm_g": c_norm_g, "w_branch": w_branch, "w_out": w_out, "norm2_g": norm2_g,
         "ffn_w_gate": ffn_w_gate, "ffn_w_up": ffn_w_up, "ffn_conv_w": ffn_conv_w,
         "ffn_conv_b": ffn_conv_b, "ffn_w_down": ffn_w_down, "final_norm_g": final_norm_g}
    depth = w_in.shape[0]
    params = [_layer_params(l, W) for l in range(depth)]
    lb_sm = jax.nn.softmax(c_lb.astype(F32), axis=0)
    lb_all = jnp.cumsum(lb_sm, axis=0) - lb_sm[0]
    b_p, t_p = x_prompt.shape[:2]

    def zeros(*s):
        return jnp.zeros((depth, b_p) + s, F32)

    st_p = {"ha": zeros(BRANCH_W), "ca": zeros(3, BRANCH_W), "wkv": zeros(N_HEADS, HEAD_W, HEAD_W),
            "sh": zeros(B_COLS), "sc": zeros(N_HEADS, C_EXPAND, HEAD_W), "k": None, "v": None,
            "cf": zeros(2, D_FF)}
    out_p = _trunk(x_prompt, 0, st_p, W, params, lb_all)
    st_s = {"ha": state_a_h, "ca": state_a_conv, "wkv": state_b_wkv, "sh": state_b_shift, "sc": state_c_s,
            "k": cache_d_k, "v": cache_d_v, "cf": state_ffn_conv}
    out_s = _trunk(x_sample, PAST_LEN, st_s, W, params, lb_all)
    return (out_p[0], out_s[0]) + out_p[1:] + out_s[1:]
```

```python
import functools

import numpy as np

import jax
import jax.numpy as jnp
from jax import lax
from jax.experimental import pallas as pl
from jax.experimental.pallas import tpu as pltpu

F32 = jnp.float32
BF16 = jnp.bfloat16
HI = lax.Precision.HIGHEST

D_MODEL = 1024
BRANCH_W = 256
N_HEADS = 4
HEAD_W = 64
HEAD_SHIFT = 6
C_EXPAND = 128
C_EXPAND_SHIFT = 7
C_FDIM = 512
B_COLS = 1024
D_FF = 3072
NORM_EPS = 1e-6
A_C = 8.0
B_GN_EPS = 64e-5
ROPE_THETA = 10000.0
PAST_LEN = 8192
D_WIN_MAX = 2048
NEG_BIG = -1e30
N_MIX_COLS = 3840
ATT_BLK = 128

VMEM_LIMIT = 56 * 1024 * 1024


def _dot(a, b, prec=None):
    return jnp.dot(a, b, preferred_element_type=F32, precision=prec)


def _dot_nt(a, b, prec=None):
    return lax.dot_general(a, b, (((1,), (1,)), ((), ())), preferred_element_type=F32, precision=prec)


def _dot_tn(a, b, prec=None):
    return lax.dot_general(a, b, (((0,), (0,)), ((), ())), preferred_element_type=F32, precision=prec)


def _rms_rows(x, g):
    ms = jnp.mean(x * x, axis=-1, keepdims=True)
    return x * lax.rsqrt(ms + NORM_EPS) * g


def _round_up(n, m):
    return (n + m - 1) // m * m


def _params(sem):
    return pltpu.CompilerParams(dimension_semantics=sem, vmem_limit_bytes=VMEM_LIMIT)


def _resident(shape):
    nd = len(shape)
    return pl.BlockSpec(shape, lambda *_: (0,) * nd, pipeline_mode=pl.Buffered(1))


def _split3(x):
    hi = x.astype(BF16)
    r1 = x - hi.astype(F32)
    mid = r1.astype(BF16)
    lo = (r1 - mid.astype(F32)).astype(BF16)
    return hi, mid, lo


def _head_sum(x, e3):
    return _dot(jnp.concatenate(_split3(x), axis=1), e3)


def _dot_split(x, w3):
    hi = x.astype(BF16)
    lo = (x - hi.astype(F32)).astype(BF16)
    return _dot(jnp.concatenate([hi, lo, hi], axis=1), w3)


def _split_weight(w):
    hi = w.astype(BF16)
    lo = (w - hi.astype(F32)).astype(BF16)
    return jnp.concatenate([hi, hi, lo], axis=0)


def _stack_heads(x, lane_head):
    return jnp.concatenate([jnp.where(lane_head == h, x, jnp.zeros_like(x)) for h in range(N_HEADS)], axis=0)


def _inproj_body(x_ref, g_ref, w_ref, o_ref, u_scr):
    @pl.when(pl.program_id(1) == 0)
    def _():
        u_scr[...] = _rms_rows(x_ref[...], g_ref[...]).astype(BF16)

    o_ref[...] = _dot(u_scr[...], w_ref[...])


def _inproj(x2d, g, w):
    n, d = x2d.shape
    nc = w.shape[1]
    tm = min(n, 1024)
    tn = 768
    return pl.pallas_call(
        _inproj_body,
        out_shape=jax.ShapeDtypeStruct((n, nc), F32),
        grid=(n // tm, nc // tn),
        in_specs=[pl.BlockSpec((tm, d), lambda i, j: (i, 0)),
                  pl.BlockSpec((1, d), lambda i, j: (0, 0)),
                  pl.BlockSpec((d, tn), lambda i, j: (0, j))],
        out_specs=pl.BlockSpec((tm, tn), lambda i, j: (i, j)),
        scratch_shapes=[pltpu.VMEM((tm, d), BF16)],
        compiler_params=_params(("parallel", "arbitrary")),
        name="inproj",
    )(x2d, g, w)


def _rglru_body(t_len, tp, axg_ref, h0_ref, cbuf_ref, cw_ref, cb_ref, wgx_ref, bgx_ref, wga_ref, bga_ref,
                lam_ref, y_ref, hl_ref, nb_ref, xs_scr, a_scr, b_scr):
    if tp != t_len:
        xs_scr[...] = jnp.zeros(xs_scr.shape, F32)
    xs_scr[pl.ds(5, 3), :] = jnp.concatenate([cbuf_ref[0], jnp.zeros((3, BRANCH_W), F32)], axis=1)
    xs_scr[pl.ds(8, t_len), :] = axg_ref[0]
    x0 = xs_scr[pl.ds(8, tp), 0:BRANCH_W]
    x1 = xs_scr[pl.ds(7, tp), 0:BRANCH_W]
    x2 = xs_scr[pl.ds(6, tp), 0:BRANCH_W]
    x3 = xs_scr[pl.ds(5, tp), 0:BRANCH_W]
    cw = cw_ref[...]
    xc = cb_ref[...] + cw[3:4] * x0 + cw[2:3] * x1 + cw[1:2] * x2 + cw[0:1] * x3
    gate_x = jax.nn.sigmoid(_dot_split(xc, wgx_ref[...]) + bgx_ref[...])
    gate_a = jax.nn.sigmoid(_dot_split(xc, wga_ref[...]) + bga_ref[...])
    log_a = -A_C * gate_a * jax.nn.softplus(-lam_ref[...])
    a = jnp.exp(log_a)
    th = jnp.tanh(log_a)
    b_in = jnp.sqrt(-2.0 * th / (1.0 - th)) * (gate_x * xc)
    if tp != t_len:
        row = lax.broadcasted_iota(jnp.int32, (tp, 1), 0)
        a = jnp.where(row < t_len, a, 1.0)
        b_in = jnp.where(row < t_len, b_in, 0.0)
    a_scr[...] = a
    b_scr[...] = b_in

    row8 = lax.broadcasted_iota(jnp.int32, (8, 1), 0)

    def group(g, carry):
        r0 = pl.multiple_of(g * 8, 8)
        ag = a_scr[pl.ds(r0, 8), :]
        bg = b_scr[pl.ds(r0, 8), :]
        for s in (1, 2, 4):
            a_sh = jnp.where(row8 >= s, pltpu.roll(ag, s, 0), 1.0)
            b_sh = jnp.where(row8 >= s, pltpu.roll(bg, s, 0), 0.0)
            bg = ag * b_sh + bg
            ag = ag * a_sh
        h = ag * carry + bg
        b_scr[pl.ds(r0, 8), :] = h
        return h[7:8, :]

    h_last = lax.fori_loop(0, tp // 8, group, h0_ref[0])
    h = b_scr[...]
    gate = xs_scr[pl.ds(8, tp), BRANCH_W:2 * BRANCH_W]
    y = h * jax.nn.gelu(gate)
    y_ref[0] = y[0:t_len]
    hl_ref[0] = h_last
    nb_ref[0] = xs_scr[pl.ds(8 + t_len - 3, 3), 0:BRANCH_W]


def _rglru(cols3, h0, cbuf, p):
    b, t, _ = cols3.shape
    tp = _round_up(t, 8)
    body = functools.partial(_rglru_body, t, tp)
    vec = pl.BlockSpec((1, BRANCH_W), lambda i: (0, 0))
    mat = pl.BlockSpec((3 * BRANCH_W, BRANCH_W), lambda i: (0, 0))
    return pl.pallas_call(
        body,
        out_shape=(jax.ShapeDtypeStruct((b, t, BRANCH_W), F32),
                   jax.ShapeDtypeStruct((b, 1, BRANCH_W), F32),
                   jax.ShapeDtypeStruct((b, 3, BRANCH_W), F32)),
        grid=(b,),
        in_specs=[pl.BlockSpec((1, t, 2 * BRANCH_W), lambda i: (i, 0, 2)),
                  pl.BlockSpec((1, 1, BRANCH_W), lambda i: (i, 0, 0)),
                  pl.BlockSpec((1, 3, BRANCH_W), lambda i: (i, 0, 0)),
                  pl.BlockSpec((4, BRANCH_W), lambda i: (0, 0)),
                  vec, mat, vec, mat, vec, vec],
        out_specs=(pl.BlockSpec((1, t, BRANCH_W), lambda i: (i, 0, 0)),
                   pl.BlockSpec((1, 1, BRANCH_W), lambda i: (i, 0, 0)),
                   pl.BlockSpec((1, 3, BRANCH_W), lambda i: (i, 0, 0))),
        scratch_shapes=[pltpu.VMEM((tp + 8, 2 * BRANCH_W), F32),
                        pltpu.VMEM((tp, BRANCH_W), F32),
                        pltpu.VMEM((tp, BRANCH_W), F32)],
        compiler_params=_params(("parallel",)),
        name="rglru",
    )(cols3, h0, cbuf, p["a_conv_w"], p["a_conv_b"], p["a_gx_bd"], p["a_gx_b"], p["a_ga_bd"], p["a_ga_b"],
      p["a_lambda"])


RWKV_GROUP = 4


def _rwkv_consts(tc, ch):
    t = np.arange(tc)
    same = (t[:, None] // ch) == (t[None, :] // ch)
    tril = same & (t[None, :] <= t[:, None])
    return jnp.asarray(np.concatenate([tril, same], axis=0).astype(np.float32), BF16)


def _rwkv_body(t_len, tc, n_t, ch, cb_ref, shift_ref, s0_ref, mu_ref, w0_ref, w2_ref, a0_ref, a2_ref, g2_ref,
               kk_ref, ka_ref, rk_ref, lnw_ref, lnb_ref, cs_ref, e3_ref, y_ref, last_ref, sn_ref,
               xs_scr, s_scr, at_scr, bt_scr, kt_scr, bh_scr, kh_scr, v_scr, rt_scr, gc_scr, y_scr, bon_scr,
               g_scr):
    ti = pl.program_id(1)
    t_blk = min(t_len, tc)
    cs = N_HEADS * ch

    @pl.when(ti == 0)
    def _():
        if t_blk != tc:
            xs_scr[...] = jnp.zeros(xs_scr.shape, F32)
        xs_scr[pl.ds(7, 1), :] = shift_ref[0]
        s_scr[...] = jnp.zeros(s_scr.shape, F32)
        for h in range(N_HEADS):
            s_scr[h * HEAD_W:(h + 1) * HEAD_W, h * HEAD_W:(h + 1) * HEAD_W] = s0_ref[0, h]

    xs_scr[pl.ds(8, t_blk), :] = cb_ref[0]
    cb = xs_scr[pl.ds(8, tc), :]
    shifted = xs_scr[pl.ds(7, tc), :]
    cm = cb + (shifted - cb) * mu_ref[...]
    r = cm[:, 0:256]
    k = cm[:, 256:512]
    v = cm[:, 512:768]
    lora = cm[:, 768:896]
    w = -jax.nn.softplus(-(w0_ref[...] + _dot_split(jnp.tanh(lora), w2_ref[...]))) - 0.5
    log_w = -jnp.exp(w)
    a = jax.nn.sigmoid(a0_ref[...] + _dot_split(lora, a2_ref[...]))
    g_scr[...] = _dot_split(jax.nn.sigmoid(cm[:, 896:1024]), g2_ref[...])
    e3 = e3_ref[...]
    kk = k * kk_ref[...]
    kk = kk / jnp.maximum(jnp.sqrt(_head_sum(kk * kk, e3)), 1e-12)
    k2 = k * (1.0 + (a - 1.0) * ka_ref[...])
    bon_scr[...] = _head_sum(r * k2 * rk_ref[...], e3) * v
    a_s = -kk
    b_s = kk * a
    if t_blk != tc:
        row = lax.broadcasted_iota(jnp.int32, (tc, 1), 0)
        live = row < t_blk
        log_w = jnp.where(live, log_w, 0.0)
        a_s = jnp.where(live, a_s, 0.0)
        b_s = jnp.where(live, b_s, 0.0)
        k2 = jnp.where(live, k2, 0.0)
    parts = jnp.concatenate(_split3(log_w), axis=1)
    sub = cs_ref.shape[1]
    cum, tot = [], []
    for j in range(tc // sub):
        cc = _dot(cs_ref[...], parts[j * sub:(j + 1) * sub, :])
        cc = cc[:, 0:256] + cc[:, 256:512] + cc[:, 512:768]
        cum.append(cc[0:sub])
        tot.append(cc[sub:2 * sub])
    cum = jnp.concatenate(cum, axis=0)
    tot = jnp.concatenate(tot, axis=0)
    inv = jnp.exp(-cum)
    tail = jnp.exp(tot - cum)
    at_scr[...] = (a_s * jnp.exp(cum - log_w)).astype(BF16)
    bt_scr[...] = (b_s * inv).astype(BF16)
    kt_scr[...] = (k2 * inv).astype(BF16)
    bh_scr[...] = (b_s * tail).astype(BF16)
    kh_scr[...] = (k2 * tail).astype(BF16)
    v_scr[...] = v.astype(BF16)
    rt_scr[...] = r * jnp.exp(cum)
    gc_scr[...] = jnp.exp(tot)

    row_t = lax.broadcasted_iota(jnp.int32, (cs, cs), 0) & (ch - 1)
    col_t = lax.broadcasted_iota(jnp.int32, (cs, cs), 1) & (ch - 1)
    strict = col_t < row_t
    row_t2 = lax.broadcasted_iota(jnp.int32, (cs, 2 * cs), 0) & (ch - 1)
    col_t2 = lax.broadcasted_iota(jnp.int32, (cs, 2 * cs), 1) & (ch - 1)
    incl2 = col_t2 <= row_t2
    eye = (lax.broadcasted_iota(jnp.int32, (cs, cs), 0) == lax.broadcasted_iota(jnp.int32, (cs, cs), 1)).astype(F32)
    lane_head = lax.broadcasted_iota(jnp.int32, (ch, BRANCH_W), 1) >> HEAD_SHIFT
    n_dbl = ch.bit_length() - 1

    def prepare(c):
        r0 = c * ch if isinstance(c, int) else pl.multiple_of(c * ch, ch)
        abd, bbd, kbd, bhd, khd, vbd = (_stack_heads(s[pl.ds(r0, ch), :], lane_head)
                                        for s in (at_scr, bt_scr, kt_scr, bh_scr, kh_scr, v_scr))
        rbd = _stack_heads(rt_scr[pl.ds(r0, ch), :], lane_head)
        s1 = _dot_nt(jnp.concatenate([abd, rbd.astype(BF16)], axis=0), jnp.concatenate([bbd, kbd], axis=0))
        l_ab = jnp.where(strict, s1[0:cs, 0:cs], 0.0)
        l_ak = jnp.where(strict, s1[0:cs, cs:2 * cs], 0.0)
        m_r = jnp.where(incl2, s1[cs:2 * cs, :], 0.0).astype(BF16)
        tinv = eye + l_ab
        lb16 = l_ab.astype(BF16)
        yield
        pw = _dot(lb16, lb16)
        for i in range(1, n_dbl):
            pw16 = pw.astype(BF16)
            yield
            if i < n_dbl - 1:
                both = _dot(jnp.concatenate([tinv.astype(BF16), pw16], axis=0), pw16)
                tinv = tinv + both[0:cs]
                pw = both[cs:2 * cs]
            else:
                tinv = tinv + _dot(tinv.astype(BF16), pw16)
        yield
        w1 = _dot(l_ak.astype(BF16), vbd)
        yield
        x = _dot(tinv.astype(BF16), jnp.concatenate([abd, w1.astype(BF16)], axis=1))
        yield
        z = _dot(m_r[:, 0:cs], x.astype(BF16))
        r_hat = rbd + z[:, 0:BRANCH_W]
        yield
        y_hat = z[:, BRANCH_W:2 * BRANCH_W] + _dot(m_r[:, cs:2 * cs], vbd)
        ar = jnp.concatenate([x[:, 0:BRANCH_W], r_hat], axis=0).astype(BF16)
        g_end = gc_scr[pl.ds(r0, ch), :][0:1, :]
        return ar, x[:, BRANCH_W:2 * BRANCH_W], y_hat, vbd, jnp.concatenate([bhd, khd], axis=0), g_end

    def advance(c0, prepared):
        for n, (ar, u_hat, y_hat, vbd, bk, g_end) in enumerate(prepared):
            c = c0 + n
            r0 = c * ch if isinstance(c, int) else pl.multiple_of(c * ch, ch)
            sb = s_scr[...]
            uy = _dot_nt(ar, sb.astype(BF16))
            u = uy[0:cs] + u_hat
            ys = uy[cs:2 * cs] + y_hat
            y_scr[pl.ds(r0, ch), :] = ys[0:ch] + ys[ch:2 * ch] + ys[2 * ch:3 * ch] + ys[3 * ch:4 * ch]
            yield
            s_scr[...] = sb * g_end + _dot_tn(jnp.concatenate([u.astype(BF16), vbd], axis=0), bk)
            yield

    def interleave(gens):
        done = [None] * len(gens)
        live = list(range(len(gens)))
        while live:
            for i in list(live):
                try:
                    next(gens[i])
                except StopIteration as stop:
                    done[i] = stop.value
                    live.remove(i)
        return done

    n_chunk = tc // ch
    grp = RWKV_GROUP if n_chunk % RWKV_GROUP == 0 else 1
    first = tuple(interleave([prepare(c) for c in range(grp)]))
    if n_chunk > grp:
        def group(i, carry):
            c0 = i * grp
            res = interleave([prepare(c0 + grp + n) for n in range(grp)] + [advance(c0, carry)])
            return tuple(res[0:grp])

        first = lax.fori_loop(0, n_chunk // grp - 1, group, first)
    interleave([advance(n_chunk - grp, first)])

    y = y_scr[...]
    mean = _head_sum(y, e3) * (1.0 / HEAD_W)
    dev = y - mean
    var = _head_sum(dev * dev, e3) * (1.0 / HEAD_W)
    yn = dev * lax.rsqrt(var + B_GN_EPS) * lnw_ref[...] + lnb_ref[...]
    out = (yn + bon_scr[...]) * g_scr[...]
    y_ref[0] = out[0:t_blk]
    xs_scr[pl.ds(7, 1), :] = xs_scr[pl.ds(8 + t_blk - 1, 1), :]

    @pl.when(ti == n_t - 1)
    def _():
        last_ref[0] = xs_scr[pl.ds(7, 1), :]
        for h in range(N_HEADS):
            sn_ref[0, h] = s_scr[h * HEAD_W:(h + 1) * HEAD_W, h * HEAD_W:(h + 1) * HEAD_W]


def _rwkv(cols3, shift, s0, p):
    b, t, _ = cols3.shape
    if t >= 64:
        ch, tc = 64, min(t, 1024)
        assert t % tc == 0
    else:
        ch = 16
        tc = _round_up(t, ch)
    n_t = max(1, t // tc)
    t_blk = min(t, tc)
    body = functools.partial(_rwkv_body, t, tc, n_t, ch)
    consts = _rwkv_consts(min(tc, 256), ch)
    vec = lambda w: pl.BlockSpec((1, w), lambda i, j: (0, 0))
    sc16 = lambda: pltpu.VMEM((tc, BRANCH_W), BF16)
    sc32 = lambda: pltpu.VMEM((tc, BRANCH_W), F32)
    return pl.pallas_call(
        body,
        out_shape=(jax.ShapeDtypeStruct((b, t, BRANCH_W), F32),
                   jax.ShapeDtypeStruct((b, 1, B_COLS), F32),
                   jax.ShapeDtypeStruct((b, N_HEADS, HEAD_W, HEAD_W), F32)),
        grid=(b, n_t),
        in_specs=[pl.BlockSpec((1, t_blk, B_COLS), lambda i, j: (i, j, 0)),
                  pl.BlockSpec((1, 1, B_COLS), lambda i, j: (i, 0, 0)),
                  pl.BlockSpec((1, N_HEADS, HEAD_W, HEAD_W), lambda i, j: (i, 0, 0, 0)),
                  vec(B_COLS), vec(BRANCH_W),
                  pl.BlockSpec((384, BRANCH_W), lambda i, j: (0, 0)),
                  vec(BRANCH_W),
                  pl.BlockSpec((384, BRANCH_W), lambda i, j: (0, 0)),
                  pl.BlockSpec((384, BRANCH_W), lambda i, j: (0, 0)),
                  vec(BRANCH_W), vec(BRANCH_W), vec(BRANCH_W), vec(BRANCH_W), vec(BRANCH_W),
                  _resident(consts.shape), _resident(p["e3"].shape)],
        out_specs=(pl.BlockSpec((1, t_blk, BRANCH_W), lambda i, j: (i, j, 0)),
                   pl.BlockSpec((1, 1, B_COLS), lambda i, j: (i, 0, 0)),
                   pl.BlockSpec((1, N_HEADS, HEAD_W, HEAD_W), lambda i, j: (i, 0, 0, 0))),
        scratch_shapes=[pltpu.VMEM((tc + 8, B_COLS), F32),
                        pltpu.VMEM((BRANCH_W, BRANCH_W), F32),
                        sc16(), sc16(), sc16(), sc16(), sc16(), sc16(),
                        sc32(), sc32(), sc32(), sc32(), sc32()],
        compiler_params=_params(("parallel", "arbitrary")),
        name="rwkv7",
    )(cols3, shift, s0, p["b_mu"], p["b_w0"], p["b_w2p"], p["b_a0"], p["b_a2p"], p["b_g2"],
      p["b_k_k"], p["b_k_a"], p["b_r_k"], p["b_ln_w"], p["b_ln_b"], consts, p["e3"])


def _hgrn_consts(tc):
    nl = tc.bit_length() - 1
    t = np.arange(tc)
    tril = (t[None, :] <= t[:, None]).astype(np.float32)
    x = t[:, None] ^ t[None, :]
    lvl = np.floor(np.log2(np.maximum(x, 1))).astype(np.int32)
    lvl = np.where(t[:, None] > t[None, :], lvl, np.where(t[:, None] == t[None, :], nl, -1))
    return jnp.asarray(tril, BF16), jnp.asarray(lvl, jnp.int32), nl


def _hgrn_body(t_len, tc, n_t, nl, cq_ref, cf_ref, cig_ref, s0_ref, lb_ref, ng_ref, sel_ref, lvl_ref, e3_ref,
               y_ref, sn_ref, in_scr, st_scr):
    ti = pl.program_id(1)
    t_blk = min(t_len, tc)

    @pl.when(ti == 0)
    def _():
        st_scr[...] = jnp.zeros(st_scr.shape, F32)
        for h in range(N_HEADS):
            st_scr[h * HEAD_W:(h + 1) * HEAD_W, h * C_EXPAND:(h + 1) * C_EXPAND] = s0_ref[0, h].T

    if t_blk != tc:
        in_scr[...] = jnp.zeros(in_scr.shape, F32)
        in_scr[0, pl.ds(0, t_blk), :] = cq_ref[0]
        in_scr[1, pl.ds(0, t_blk), :] = cf_ref[0]
        in_scr[2, pl.ds(0, t_blk), :] = cig_ref[0]
        q, f, cig = in_scr[0], in_scr[1], in_scr[2]
    else:
        q, f, cig = cq_ref[0], cf_ref[0], cig_ref[0]
    lb = lb_ref[...]
    fg = lb + (1.0 - lb) * jax.nn.sigmoid(f)
    log_f = jnp.log(fg)
    kk = 1.0 - fg
    fg_live = fg
    if t_blk != tc:
        row = lax.broadcasted_iota(jnp.int32, (tc, 1), 0)
        live = row < t_blk
        log_f = jnp.where(live, log_f, 0.0)
        kk = jnp.where(live, kk, 0.0)
        fg_live = jnp.where(live, fg, 1.0)
    hi = log_f.astype(BF16)
    lo = (log_f - hi.astype(F32)).astype(BF16)
    parts = jnp.concatenate([hi, lo], axis=1)

    d = _dot(sel_ref[...], parts)
    cum = d[:, 0:C_FDIM] + d[:, C_FDIM:2 * C_FDIM]
    odd = (lax.broadcasted_iota(jnp.int32, (tc, 1), 0) & 1) == 1

    lvl = lvl_ref[...]
    att = [None] * N_HEADS
    for l in range(nl + 1):
        if l == 0:
            e = jnp.where(odd, fg_live, 1.0)
        elif l < nl:
            m = 1 << l
            ref_rows = cum.reshape(tc // (2 * m), 2 * m, C_FDIM)[:, m - 1:m, :]
            ref_rows = jnp.broadcast_to(ref_rows, (tc // (2 * m), 2 * m, C_FDIM)).reshape(tc, C_FDIM)
            e = jnp.exp(-jnp.abs(cum - ref_rows))
        if l < nl:
            qs = (q * e).astype(BF16)
            ks = (kk * e).astype(BF16)
        else:
            qs = q.astype(BF16)
            ks = kk.astype(BF16)
        hit = lvl == l
        for h in range(N_HEADS):
            kl = slice(h * C_EXPAND, (h + 1) * C_EXPAND)
            a = _dot_nt(qs[:, kl], ks[:, kl])
            att[h] = jnp.where(hit, a, 0.0 if att[h] is None else att[h])
    c_last = cum[tc - 1:tc, :]
    lane_head = lax.broadcasted_iota(jnp.int32, (tc, BRANCH_W), 1) >> HEAD_SHIFT
    v = cig[:, 0:BRANCH_W]
    v_stack = _stack_heads(v, lane_head).astype(BF16)
    p_cat = jnp.concatenate([a.astype(BF16) for a in att], axis=1)
    st = st_scr[...]
    o = _dot(p_cat, v_stack) + _dot_nt((q * jnp.exp(cum)).astype(BF16), st.astype(BF16))
    ms = _head_sum(o * o, e3_ref[...]) * (1.0 / HEAD_W)
    y = o * lax.rsqrt(ms + NORM_EPS) * ng_ref[...] * jax.nn.silu(cig[:, BRANCH_W:2 * BRANCH_W])
    y_ref[0] = y[0:t_blk]
    upd = _dot_tn(v.astype(BF16), (kk * jnp.exp(c_last - cum)).astype(BF16))
    rh = lax.broadcasted_iota(jnp.int32, (BRANCH_W, C_FDIM), 0) >> HEAD_SHIFT
    ch_ = lax.broadcasted_iota(jnp.int32, (BRANCH_W, C_FDIM), 1) >> C_EXPAND_SHIFT
    st_scr[...] = st * jnp.exp(c_last) + jnp.where(rh == ch_, upd, 0.0)

    @pl.when(ti == n_t - 1)
    def _():
        for h in range(N_HEADS):
            sn_ref[0, h] = st_scr[h * HEAD_W:(h + 1) * HEAD_W, h * C_EXPAND:(h + 1) * C_EXPAND].T


def _hgrn(cols3, s0, lb, ng, e3):
    b, t, _ = cols3.shape
    tc = 256 if t >= 256 else 16
    assert t % tc == 0 or t < tc
    n_t = max(1, t // tc)
    t_blk = min(t, tc)
    sel, lvl, nl = _hgrn_consts(tc)
    body = functools.partial(_hgrn_body, t, tc, n_t, nl)
    blk = lambda idx: pl.BlockSpec((1, t_blk, C_FDIM), lambda i, j: (i, j, idx))
    pad_shape = (3, tc, C_FDIM) if t_blk != tc else (1, 8, 128)
    return pl.pallas_call(
        body,
        out_shape=(jax.ShapeDtypeStruct((b, t, BRANCH_W), F32),
                   jax.ShapeDtypeStruct((b, N_HEADS, C_EXPAND, HEAD_W), F32)),
        grid=(b, n_t),
        in_specs=[blk(3), blk(4), blk(5),
                  pl.BlockSpec((1, N_HEADS, C_EXPAND, HEAD_W), lambda i, j: (i, 0, 0, 0)),
                  pl.BlockSpec((1, C_FDIM), lambda i, j: (0, 0)),
                  pl.BlockSpec((1, BRANCH_W), lambda i, j: (0, 0)),
                  _resident(sel.shape), _resident(lvl.shape), _resident(e3.shape)],
        out_specs=(pl.BlockSpec((1, t_blk, BRANCH_W), lambda i, j: (i, j, 0)),
                   pl.BlockSpec((1, N_HEADS, C_EXPAND, HEAD_W), lambda i, j: (i, 0, 0, 0))),
        scratch_shapes=[pltpu.VMEM(pad_shape, F32),
                        pltpu.VMEM((BRANCH_W, C_FDIM), F32)],
        compiler_params=_params(("parallel", "arbitrary")),
        name="hgrn2",
    )(cols3, cols3, cols3, s0, lb, ng, sel, lvl, e3)


def _rope_rows(x, cos, sin_signed):
    lane = lax.broadcasted_iota(jnp.int32, x.shape, 1)
    swapped = jnp.where((lane & (HEAD_W - 1)) < HEAD_W // 2,
                        pltpu.roll(x, BRANCH_W - HEAD_W // 2, 1), pltpu.roll(x, HEAD_W // 2, 1))
    return x * cos + swapped * sin_signed


def _pattern_weight(d):
    ok = d >= 0
    w = ((d <= 128).astype(F32) + (((d & 3) == 0) & (d <= 512)).astype(F32)
         + (((d & 15) == 0) & (d <= 2048)).astype(F32))
    return jnp.where(ok, w, 0.0)


def _att_group(n_blk):
    return next(g for g in (4, 2, 1) if n_blk % g == 0)


def _att_weight_table(n_blk):
    grp = _att_group(n_blk)
    r = np.arange(ATT_BLK)
    delta = np.arange(-(grp - 1), n_blk)
    d = delta[:, None, None] * ATT_BLK + r[None, :, None] - r[None, None, :]
    w = ((d <= 128).astype(np.float32) + ((d % 4 == 0) & (d <= 512)) + ((d % 16 == 0) & (d <= 2048)))
    return jnp.asarray(np.where(d >= 0, w, 0.0), F32)


def _attn_prompt_body(n_blk, qkv_ref, cos_ref, sin_ref, wt_ref, y_ref, kt_ref, vt_ref, q_scr, kb_scr, vh_scr,
                      s_scr):
    t = n_blk * ATT_BLK
    qkv = qkv_ref[0]
    cos = cos_ref[...]
    sin = sin_ref[...]
    q = _rope_rows(qkv[:, 0:BRANCH_W], cos, sin) * (HEAD_W ** -0.5)
    k = _rope_rows(qkv[:, BRANCH_W:2 * BRANCH_W], cos, sin)
    q_scr[...] = q
    kb_scr[...] = k.astype(BF16)
    v = qkv[:, 2 * BRANCH_W:3 * BRANCH_W]
    kt_ref[0] = k.T
    vt_ref[0] = v.T
    lane_head = lax.broadcasted_iota(jnp.int32, (ATT_BLK, BRANCH_W), 1) >> HEAD_SHIFT
    for j in range(n_blk):
        vj = v[j * ATT_BLK:(j + 1) * ATT_BLK, :]
        for h in range(N_HEADS):
            vh_scr[j, pl.ds(h * ATT_BLK, ATT_BLK), :] = jnp.where(lane_head == h, vj, 0.0).astype(BF16)

    grp = _att_group(n_blk)

    def q_block(i, _):
        q0 = pl.multiple_of(i * ATT_BLK, ATT_BLK)
        qi = q_scr[pl.ds(q0, ATT_BLK), :]
        q_bd = jnp.concatenate([jnp.where(lane_head == h, qi, 0.0) for h in range(N_HEADS)], axis=0).astype(BF16)

        def weights(j):
            w1 = wt_ref[i - j + (grp - 1)]
            return jnp.concatenate([w1] * N_HEADS, axis=0)

        n_grp = (i + grp) // grp

        def pass1(gi, m_run):
            js = [gi * grp + n for n in range(grp)]
            ss = [_dot_nt(q_bd, kb_scr[pl.ds(pl.multiple_of(j * ATT_BLK, ATT_BLK), ATT_BLK), :]) for j in js]
            for j, s in zip(js, ss):
                s = jnp.where(weights(j) > 0.0, s, NEG_BIG)
                s_scr[j] = s
                m_run = jnp.maximum(m_run, s)
            return m_run

        m_run = lax.fori_loop(0, n_grp, pass1, jnp.full((N_HEADS * ATT_BLK, ATT_BLK), NEG_BIG, F32))
        m = jnp.max(m_run, axis=-1, keepdims=True)

        def pass2(gi, carry):
            l_run, acc = carry
            js = [gi * grp + n for n in range(grp)]
            ps = [weights(j) * jnp.exp(s_scr[j] - m) for j in js]
            cats = [jnp.concatenate([p[h * ATT_BLK:(h + 1) * ATT_BLK, :] for h in range(N_HEADS)],
                                    axis=1).astype(BF16) for p in ps]
            for j, p, p_cat in zip(js, ps, cats):
                acc = acc + _dot(p_cat, vh_scr[j])
                l_run = l_run + p
            return l_run, acc

        l_run, acc = lax.fori_loop(0, n_grp, pass2, (jnp.zeros((N_HEADS * ATT_BLK, ATT_BLK), F32),
                                                     jnp.zeros((ATT_BLK, BRANCH_W), F32)))
        l = jnp.sum(l_run, axis=-1, keepdims=True)
        den = jnp.zeros((ATT_BLK, BRANCH_W), F32)
        for h in range(N_HEADS):
            den = jnp.where(lane_head == h, l[h * ATT_BLK:(h + 1) * ATT_BLK, :], den)
        y_ref[0, pl.ds(q0, ATT_BLK), :] = acc / den
        return 0

    lax.fori_loop(0, n_blk, q_block, 0)


def _attn_prompt(cols3, cos, sin):
    b, t, _ = cols3.shape
    n_blk = t // ATT_BLK
    assert n_blk * ATT_BLK == t
    body = functools.partial(_attn_prompt_body, n_blk)
    wtab = _att_weight_table(n_blk)
    return pl.pallas_call(
        body,
        out_shape=(jax.ShapeDtypeStruct((b, t, BRANCH_W), F32), jax.ShapeDtypeStruct((b, BRANCH_W, t), F32),
                   jax.ShapeDtypeStruct((b, BRANCH_W, t), F32)),
        grid=(b,),
        in_specs=[pl.BlockSpec((1, t, 3 * BRANCH_W), lambda i: (i, 0, 4)),
                  _resident((t, BRANCH_W)), _resident((t, BRANCH_W)), _resident(wtab.shape)],
        out_specs=(pl.BlockSpec((1, t, BRANCH_W), lambda i: (i, 0, 0)),
                   pl.BlockSpec((1, BRANCH_W, t), lambda i: (i, 0, 0)),
                   pl.BlockSpec((1, BRANCH_W, t), lambda i: (i, 0, 0))),
        scratch_shapes=[pltpu.VMEM((t, BRANCH_W), F32),
                        pltpu.VMEM((t, BRANCH_W), BF16),
                        pltpu.VMEM((n_blk, N_HEADS * ATT_BLK, BRANCH_W), BF16),
                        pltpu.VMEM((n_blk, N_HEADS * ATT_BLK, ATT_BLK), F32)],
        compiler_params=_params(("parallel",)),
        name="attn_prompt",
    )(cols3, cos, sin, wtab)


def _attn_sample_body(t_len, l_cache, qkv_ref, kc_ref, vc_ref, cos_ref, sin_ref, y_ref, k_ref, pad_scr):
    tp = 8
    pad_scr[...] = jnp.zeros(pad_scr.shape, F32)
    pad_scr[pl.ds(0, t_len), :] = qkv_ref[0]
    qkv = pad_scr[...]
    cos = cos_ref[...]
    sin = sin_ref[...]
    q = _rope_rows(qkv[:, 0:BRANCH_W], cos, sin) * (HEAD_W ** -0.5)
    k_new = _rope_rows(qkv[:, BRANCH_W:2 * BRANCH_W], cos, sin)
    v_new = qkv[:, 2 * BRANCH_W:3 * BRANCH_W]
    k_ref[0] = k_new[0:t_len]
    lane_head = lax.broadcasted_iota(jnp.int32, (tp, BRANCH_W), 1) >> HEAD_SHIFT
    q_bd = jnp.concatenate([jnp.where(lane_head == h, q, 0.0) for h in range(N_HEADS)], axis=0).astype(BF16)
    n_row = N_HEADS * tp
    k_t = kc_ref[0, 0].reshape(BRANCH_W, l_cache)
    v_t = vc_ref[0, 0].reshape(BRANCH_W, l_cache)
    s_c = _dot(q_bd, k_t.astype(BF16))
    s_n = _dot_nt(q_bd, k_new.astype(BF16))
    t_row = lax.broadcasted_iota(jnp.int32, (n_row, 1), 0) & (tp - 1)
    col_c = lax.broadcasted_iota(jnp.int32, (n_row, l_cache), 1)
    w_c = _pattern_weight(l_cache + t_row - col_c)
    col_n = lax.broadcasted_iota(jnp.int32, (n_row, tp), 1)
    w_n = jnp.where(col_n < t_len, _pattern_weight(t_row - col_n), 0.0)
    s_c = jnp.where(w_c > 0.0, s_c, NEG_BIG)
    s_n = jnp.where(w_n > 0.0, s_n, NEG_BIG)
    m = jnp.maximum(jnp.max(s_c, axis=-1, keepdims=True), jnp.max(s_n, axis=-1, keepdims=True))
    p_c = w_c * jnp.exp(s_c - m)
    p_n = w_n * jnp.exp(s_n - m)
    l = jnp.sum(p_c, axis=-1, keepdims=True) + jnp.sum(p_n, axis=-1, keepdims=True)
    num = _dot_nt(p_c.astype(BF16), v_t.astype(BF16)) + _dot(p_n.astype(BF16), v_new.astype(BF16))
    ratio = num / l
    out = jnp.zeros((tp, BRANCH_W), F32)
    for h in range(N_HEADS):
        out = jnp.where(lane_head == h, ratio[h * tp:(h + 1) * tp, :], out)
    y_ref[0] = out[0:t_len]


def _attn_sample(cols3, k_cache_t, v_cache_t, layer, cos, sin):
    b, t, _ = cols3.shape
    l_cache = k_cache_t.shape[-1]
    assert t <= 8
    body = functools.partial(_attn_sample_body, t, l_cache)
    cache_spec = pl.BlockSpec((1, 1, N_HEADS, HEAD_W, l_cache), lambda i: (layer, i, 0, 0, 0))
    return pl.pallas_call(
        body,
        out_shape=(jax.ShapeDtypeStruct((b, t, BRANCH_W), F32), jax.ShapeDtypeStruct((b, t, BRANCH_W), F32)),
        grid=(b,),
        in_specs=[pl.BlockSpec((1, t, 3 * BRANCH_W), lambda i: (i, 0, 4)),
                  cache_spec, cache_spec,
                  pl.BlockSpec((8, BRANCH_W), lambda i: (0, 0)),
                  pl.BlockSpec((8, BRANCH_W), lambda i: (0, 0))],
        out_specs=(pl.BlockSpec((1, t, BRANCH_W), lambda i: (i, 0, 0)),
                   pl.BlockSpec((1, t, BRANCH_W), lambda i: (i, 0, 0))),
        scratch_shapes=[pltpu.VMEM((8, 3 * BRANCH_W), F32)],
        compiler_params=_params(("parallel",)),
        name="attn_sample",
    )(cols3, k_cache_t, v_cache_t, cos, sin)


def _merge_body(x_ref, g_ref, wgt_ref, ya_ref, yb_ref, yc_ref, yd_ref, wb_ref, wo_ref, o_ref):
    x = x_ref[...]
    u = _rms_rows(x, g_ref[...]).astype(BF16)
    merged = None
    for n, y_ref in enumerate((ya_ref, yb_ref, yc_ref, yd_ref)):
        z = _dot(y_ref[...].astype(BF16), wb_ref[n])
        gate = jax.nn.sigmoid(_dot(u, wgt_ref[:, n * D_MODEL:(n + 1) * D_MODEL]))
        merged = gate * z if merged is None else merged + gate * z
    o_ref[...] = x + _dot(merged.astype(BF16), wo_ref[...])


def _merge(x2d, g, wgt, ys, wb, wo):
    n, d = x2d.shape
    tm = min(n, 512)
    row = lambda w: pl.BlockSpec((tm, w), lambda i: (i, 0))
    return pl.pallas_call(
        _merge_body,
        out_shape=jax.ShapeDtypeStruct((n, d), F32),
        grid=(n // tm,),
        in_specs=[row(d), pl.BlockSpec((1, d), lambda i: (0, 0)), _resident(wgt.shape),
                  row(BRANCH_W), row(BRANCH_W), row(BRANCH_W), row(BRANCH_W),
                  _resident(wb.shape), _resident(wo.shape)],
        out_specs=row(d),
        compiler_params=_params(("parallel",)),
        name="merge",
    )(x2d, g, wgt, *ys, wb, wo)


FFN_COL = 1024


def _ffn_columns(x, v, prev1, prev2, first, second, wg_ref, wu_ref, wd_ref, cw_ref, cb_ref, keep_gate):
    acc = jnp.zeros(x.shape, F32)
    cw = cw_ref[...]
    for c in range(D_FF // FFN_COL):
        sl = slice(c * FFN_COL, (c + 1) * FFN_COL)
        hg = _dot(v, wg_ref[:, sl])
        hu = _dot(v, wu_ref[:, sl])
        h1 = jnp.where(first, prev1(sl), pltpu.roll(hg, 1, 0))
        h2 = jnp.where(first | second, prev2(sl), pltpu.roll(hg, 2, 0))
        conv = cb_ref[:, sl] + cw[2:3, sl] * hg + cw[1:2, sl] * h1 + cw[0:1, sl] * h2
        hmid = (jax.nn.gelu(conv) * hu).astype(BF16)
        acc = acc + _dot(hmid, wd_ref[sl, :])
        keep_gate(sl, hg)
    return x + acc


def _ffn_body(tm, n_t, has_final, x_ref, g_ref, wg_ref, wu_ref, wd_ref, cw_ref, cb_ref, st_ref, *rest):
    if has_final:
        fg_ref, o_ref, ns_ref, carry_scr = rest
    else:
        o_ref, ns_ref, carry_scr = rest
    ti = pl.program_id(1)

    @pl.when(ti == 0)
    def _():
        carry_scr[...] = st_ref[0]

    x = x_ref[0]
    v = _rms_rows(x, g_ref[...]).astype(BF16)
    row = lax.broadcasted_iota(jnp.int32, (tm, 1), 0)

    def prev2(sl):
        return jnp.where(row == 0, carry_scr[0:1, sl], carry_scr[1:2, sl])

    def keep_gate(sl, hg):
        carry_scr[:, sl] = hg[tm - 2:tm, :]

    y = _ffn_columns(x, v, lambda sl: carry_scr[1:2, sl], prev2, row == 0, row == 1,
                     wg_ref, wu_ref, wd_ref, cw_ref, cb_ref, keep_gate)
    o_ref[0] = _rms_rows(y, fg_ref[...]) if has_final else y

    @pl.when(ti == n_t - 1)
    def _():
        ns_ref[0] = carry_scr[...]


def _ffn_rows_body(seq, has_final, x_ref, g_ref, wg_ref, wu_ref, wd_ref, cw_ref, cb_ref, p1_ref, p2_ref, *rest):
    if has_final:
        fg_ref, o_ref, hg_ref = rest
    else:
        o_ref, hg_ref = rest
    x = x_ref[...]
    v = _rms_rows(x, g_ref[...]).astype(BF16)
    step = lax.broadcasted_iota(jnp.int32, (x.shape[0], 1), 0) & (seq - 1)

    def keep_gate(sl, hg):
        hg_ref[:, sl] = hg

    y = _ffn_columns(x, v, lambda sl: p1_ref[:, sl], lambda sl: p2_ref[:, sl], step == 0, step == 1,
                     wg_ref, wu_ref, wd_ref, cw_ref, cb_ref, keep_gate)
    o_ref[...] = _rms_rows(y, fg_ref[...]) if has_final else y


def _ffn(x3, g, wg, wu, wd, cw, cb, state, final_g):
    b, t, d = x3.shape
    has_final = final_g is not None
    extra_in = [final_g] if has_final else []
    extra_spec2 = [pl.BlockSpec((1, d), lambda i, j: (0, 0))] if has_final else []
    extra_spec1 = [pl.BlockSpec((1, d), lambda i: (0, 0))] if has_final else []
    if t >= 8:
        tm = min(t, 512)
        n_t = t // tm
        assert tm * n_t == t and tm % 8 == 0
        body = functools.partial(_ffn_body, tm, n_t, has_final)
        return pl.pallas_call(
            body,
            out_shape=(jax.ShapeDtypeStruct((b, t, d), F32), jax.ShapeDtypeStruct((b, 2, D_FF), F32)),
            grid=(b, n_t),
            in_specs=[pl.BlockSpec((1, tm, d), lambda i, j: (i, j, 0)),
                      pl.BlockSpec((1, d), lambda i, j: (0, 0)),
                      _resident(wg.shape), _resident(wu.shape), _resident(wd.shape),
                      pl.BlockSpec((3, D_FF), lambda i, j: (0, 0)),
                      pl.BlockSpec((1, D_FF), lambda i, j: (0, 0)),
                      pl.BlockSpec((1, 2, D_FF), lambda i, j: (i, 0, 0))] + extra_spec2,
            out_specs=(pl.BlockSpec((1, tm, d), lambda i, j: (i, j, 0)),
                       pl.BlockSpec((1, 2, D_FF), lambda i, j: (i, 0, 0))),
            scratch_shapes=[pltpu.VMEM((2, D_FF), F32)],
            compiler_params=_params(("parallel", "arbitrary")),
            name="convffn",
        )(x3, g, wg, wu, wd, cw, cb, state, *extra_in)
    assert t >= 2 and t & (t - 1) == 0 and (b * t) % 8 == 0
    n = b * t
    zero = jnp.zeros((b, t - 1, D_FF), F32)
    p1 = jnp.concatenate([state[:, 1:2], zero], axis=1).reshape(n, D_FF)
    p2 = jnp.concatenate([state, zero[:, 1:]], axis=1).reshape(n, D_FF)
    body = functools.partial(_ffn_rows_body, t, has_final)
    full = lambda shape: pl.BlockSpec(shape, lambda i: (0,) * len(shape))
    y, hg = pl.pallas_call(
        body,
        out_shape=(jax.ShapeDtypeStruct((n, d), F32), jax.ShapeDtypeStruct((n, D_FF), F32)),
        grid=(1,),
        in_specs=[full((n, d)), full((1, d)), _resident(wg.shape), _resident(wu.shape), _resident(wd.shape),
                  full((3, D_FF)), full((1, D_FF)), full((n, D_FF)), full((n, D_FF))] + extra_spec1,
        out_specs=(full((n, d)), full((n, D_FF))),
        compiler_params=_params(("arbitrary",)),
        name="convffn_rows",
    )(x3.reshape(n, d), g, wg, wu, wd, cw, cb, p1, p2, *extra_in)
    return y.reshape(b, t, d), hg.reshape(b, t, D_FF)[:, t - 2:, :]


def _block_diag(w):
    out = jnp.zeros((BRANCH_W, BRANCH_W), F32)
    for h in range(N_HEADS):
        out = out.at[h * HEAD_W:(h + 1) * HEAD_W, h * HEAD_W:(h + 1) * HEAD_W].set(w[h])
    return out


def _head_sum_matrix3():
    i = np.arange(BRANCH_W) // HEAD_W
    e = (i[:, None] == i[None, :]).astype(np.float32)
    return jnp.asarray(np.concatenate([e, e, e], axis=0), BF16)


def _rope_tables(pos, rows):
    half = HEAD_W // 2
    inv = ROPE_THETA ** (-jnp.arange(half, dtype=F32) / half)
    ang = pos.astype(F32)[:, None] * inv[None, :]
    cos = jnp.cos(ang)
    sin = jnp.sin(ang)
    cos_t = jnp.tile(jnp.concatenate([cos, cos], axis=1), (1, N_HEADS))
    sin_t = jnp.tile(jnp.concatenate([-sin, sin], axis=1), (1, N_HEADS))
    pad = rows - pos.shape[0]
    if pad:
        cos_t = jnp.pad(cos_t, ((0, pad), (0, 0)))
        sin_t = jnp.pad(sin_t, ((0, pad), (0, 0)))
    return cos_t, sin_t


def _layer_params(l, W):
    w_in = W["w_in"][l]
    o = 0
    parts = {}
    for name, width in (("ax", 256), ("ag", 256), ("cb", 1024), ("cq", 512), ("cf", 512), ("ci", 256),
                        ("cg", 256), ("dq", 256), ("dk", 256), ("dv", 256), ("gt", 4096)):
        parts[name] = w_in[:, o:o + width]
        o += width
    order = ("cb", "ax", "ag", "cq", "cf", "ci", "cg", "dq", "dk", "dv")
    p = {
        "w_mix": jnp.concatenate([parts[n] for n in order], axis=1).astype(BF16),
        "w_gt": parts["gt"].astype(BF16),
        "norm1_g": W["norm1_g"][l][None, :],
        "a_conv_w": W["a_conv_w"][l], "a_conv_b": W["a_conv_b"][l][None, :],
        "a_gx_bd": _split_weight(_block_diag(W["a_gx_w"][l])), "a_gx_b": W["a_gx_b"][l][None, :],
        "a_ga_bd": _split_weight(_block_diag(W["a_ga_w"][l])), "a_ga_b": W["a_ga_b"][l][None, :],
        "a_lambda": W["a_lambda"][l][None, :],
        "b_mu": W["b_mu"][l][None, :], "b_w0": W["b_w0"][l][None, :], "b_a0": W["b_a0"][l][None, :],
        "b_w2p": _split_weight(jnp.concatenate([W["b_w2"][l], jnp.zeros((64, BRANCH_W), F32)], axis=0)),
        "b_a2p": _split_weight(jnp.concatenate([jnp.zeros((64, BRANCH_W), F32), W["b_a2"][l]], axis=0)),
        "b_g2": _split_weight(W["b_g2"][l]),
        "b_k_k": W["b_k_k"][l][None, :], "b_k_a": W["b_k_a"][l][None, :],
        "b_r_k": W["b_r_k"][l].reshape(1, BRANCH_W),
        "b_ln_w": W["b_ln_w"][l][None, :], "b_ln_b": W["b_ln_b"][l][None, :],
        "c_norm_g": jnp.tile(W["c_norm_g"][l], N_HEADS)[None, :],
        "w_branch": W["w_branch"][l].astype(BF16), "w_out": W["w_out"][l].astype(BF16),
        "norm2_g": W["norm2_g"][l][None, :],
        "ffn_w_gate": W["ffn_w_gate"][l].astype(BF16), "ffn_w_up": W["ffn_w_up"][l].astype(BF16),
        "ffn_w_down": W["ffn_w_down"][l].astype(BF16),
        "ffn_conv_w": W["ffn_conv_w"][l], "ffn_conv_b": W["ffn_conv_b"][l][None, :],
        "e3": _head_sum_matrix3(),
    }
    return p


def _trunk(x, pos0, st, W, params, lb_all):
    b, t, d = x.shape
    depth = W["w_in"].shape[0]
    has_cache = st["k"] is not None
    if has_cache:
        cos, sin = _rope_tables(pos0 + jnp.arange(t), 8)
        k_cache_t = jnp.transpose(st["k"], (0, 1, 3, 4, 2))
        v_cache_t = jnp.transpose(st["v"], (0, 1, 3, 4, 2))
    else:
        cos, sin = _rope_tables(pos0 + jnp.arange(t), t)
    outs = {n: [] for n in ("ha", "ca", "wkv", "sh", "sc", "k", "v", "cf")}
    for l in range(depth):
        p = params[l]
        cols = _inproj(x.reshape(b * t, d), p["norm1_g"], p["w_mix"]).reshape(b, t, N_MIX_COLS)
        y_a, h_new, ca_new = _rglru(cols, st["ha"][l][:, None, :], st["ca"][l], p)
        y_b, sh_new, wkv_new = _rwkv(cols, st["sh"][l][:, None, :], st["wkv"][l], p)
        y_c, sc_new = _hgrn(cols, st["sc"][l], lb_all[l][None, :], p["c_norm_g"], p["e3"])
        if has_cache:
            y_d, k_rows = _attn_sample(cols, k_cache_t, v_cache_t, l, cos, sin)
            k_out = k_rows.reshape(b, t, N_HEADS, HEAD_W)
            v_out = cols[:, :, N_MIX_COLS - BRANCH_W:].reshape(b, t, N_HEADS, HEAD_W)
        else:
            y_d, k_t, v_t = _attn_prompt(cols, cos, sin)
            keep = min(D_WIN_MAX, t)
            k_out = jnp.transpose(k_t.reshape(b, N_HEADS, HEAD_W, t)[..., t - keep:], (0, 3, 1, 2))
            v_out = jnp.transpose(v_t.reshape(b, N_HEADS, HEAD_W, t)[..., t - keep:], (0, 3, 1, 2))
        ys = [y.reshape(b * t, BRANCH_W) for y in (y_a, y_b, y_c, y_d)]
        x1 = _merge(x.reshape(b * t, d), p["norm1_g"], p["w_gt"], ys, p["w_branch"], p["w_out"])
        final_g = W["final_norm_g"][None, :] if l == depth - 1 else None
        x, cf_new = _ffn(x1.reshape(b, t, d), p["norm2_g"], p["ffn_w_gate"], p["ffn_w_up"], p["ffn_w_down"],
                         p["ffn_conv_w"], p["ffn_conv_b"], st["cf"][l], final_g)
        outs["ha"].append(h_new[:, 0, :])
        outs["ca"].append(ca_new)
        outs["wkv"].append(wkv_new)
        outs["sh"].append(sh_new[:, 0, :])
        outs["sc"].append(sc_new)
        outs["k"].append(k_out)
        outs["v"].append(v_out)
        outs["cf"].append(cf_new)
    return (x,) + tuple(jnp.stack(outs[n]) for n in ("ha", "ca", "wkv", "sh", "sc", "k", "v", "cf"))


def kernel(x_prompt, x_sample, state_a_h, state_a_conv, state_b_wkv, state_b_shift, state_c_s, cache_d_k, cache_d_v, state_ffn_conv, norm1_g, w_in, a_conv_w, a_conv_b, a_gx_w, a_gx_b, a_ga_w, a_ga_b, a_lambda, b_mu, b_w0, b_w2, b_a0, b_a2, b_g2, b_k_k, b_k_a, b_r_k, b_ln_w, b_ln_b, c_lb, c_norm_g, w_branch, w_out, norm2_g, ffn_w_gate, ffn_w_up, ffn_conv_w, ffn_conv_b, ffn_w_down, final_norm_g):
    W = {"norm1_g": norm1_g, "w_in": w_in, "a_conv_w": a_conv_w, "a_conv_b": a_conv_b, "a_gx_w": a_gx_w,
         "a_gx_b": a_gx_b, "a_ga_w": a_ga_w, "a_ga_b": a_ga_b, "a_lambda": a_lambda, "b_mu": b_mu,
         "b_w0": b_w0, "b_w2": b_w2, "b_a0": b_a0, "b_a2": b_a2, "b_g2": b_g2, "b_k_k": b_k_k,
         "b_k_a": b_k_a, "b_r_k": b_r_k, "b_ln_w": b_ln_w, "b_ln_b": b_ln_b, "c_lb": c_lb,
         "c_norm_g": c_norm_g, "w_branch": w_branch, "w_out": w_out, "norm2_g": norm2_g,
         "ffn_w_gate": ffn_w_gate, "ffn_w_up": ffn_w_up, "ffn_conv_w": ffn_conv_w,
         "ffn_conv_b": ffn_conv_b, "ffn_w_down": ffn_w_down, "final_norm_g": final_norm_g}
    depth = w_in.shape[0]
    params = [_layer_params(l, W) for l in range(depth)]
    lb_sm = jax.nn.softmax(c_lb.astype(F32), axis=0)
    lb_all = jnp.cumsum(lb_sm, axis=0) - lb_sm[0]
    b_p, t_p = x_prompt.shape[:2]

    def zeros(*s):
        return jnp.zeros((depth, b_p) + s, F32)

    st_p = {"ha": zeros(BRANCH_W), "ca": zeros(3, BRANCH_W), "wkv": zeros(N_HEADS, HEAD_W, HEAD_W),
            "sh": zeros(B_COLS), "sc": zeros(N_HEADS, C_EXPAND, HEAD_W), "k": None, "v": None,
            "cf": zeros(2, D_FF)}
    out_p = _trunk(x_prompt, 0, st_p, W, params, lb_all)
    st_s = {"ha": state_a_h, "ca": state_a_conv, "wkv": state_b_wkv, "sh": state_b_shift, "sc": state_c_s,
            "k": cache_d_k, "v": cache_d_v, "cf": state_ffn_conv}
    out_s = _trunk(x_sample, PAST_LEN, st_s, W, params, lb_all)
    return (out_p[0], out_s[0]) + out_p[1:] + out_s[1:]
```

```python
import functools

import numpy as np

import jax
import jax.numpy as jnp
from jax import lax
from jax.experimental import pallas as pl
from jax.experimental.pallas import tpu as pltpu

F32 = jnp.float32
BF16 = jnp.bfloat16
HI = lax.Precision.HIGHEST

D_MODEL = 1024
BRANCH_W = 256
N_HEADS = 4
HEAD_W = 64
HEAD_SHIFT = 6
C_EXPAND = 128
C_EXPAND_SHIFT = 7
C_FDIM = 512
B_COLS = 1024
D_FF = 3072
NORM_EPS = 1e-6
A_C = 8.0
B_GN_EPS = 64e-5
ROPE_THETA = 10000.0
PAST_LEN = 8192
D_WIN_MAX = 2048
NEG_BIG = -1e30
N_MIX_COLS = 3840
ATT_BLK = 128

VMEM_LIMIT = 56 * 1024 * 1024


def _dot(a, b, prec=None):
    return jnp.dot(a, b, preferred_element_type=F32, precision=prec)


def _dot_nt(a, b, prec=None):
    return lax.dot_general(a, b, (((1,), (1,)), ((), ())), preferred_element_type=F32, precision=prec)


def _dot_tn(a, b, prec=None):
    return lax.dot_general(a, b, (((0,), (0,)), ((), ())), preferred_element_type=F32, precision=prec)


def _rms_rows(x, g):
    ms = jnp.mean(x * x, axis=-1, keepdims=True)
    return x * lax.rsqrt(ms + NORM_EPS) * g


def _round_up(n, m):
    return (n + m - 1) // m * m


def _params(sem):
    return pltpu.CompilerParams(dimension_semantics=sem, vmem_limit_bytes=VMEM_LIMIT)


def _resident(shape):
    nd = len(shape)
    return pl.BlockSpec(shape, lambda *_: (0,) * nd, pipeline_mode=pl.Buffered(1))


def _split3(x):
    hi = x.astype(BF16)
    r1 = x - hi.astype(F32)
    mid = r1.astype(BF16)
    lo = (r1 - mid.astype(F32)).astype(BF16)
    return hi, mid, lo


def _head_sum(x, e3):
    return _dot(jnp.concatenate(_split3(x), axis=1), e3)


def _dot_split(x, w3):
    hi = x.astype(BF16)
    lo = (x - hi.astype(F32)).astype(BF16)
    return _dot(jnp.concatenate([hi, lo, hi], axis=1), w3)


def _split_weight(w):
    hi = w.astype(BF16)
    lo = (w - hi.astype(F32)).astype(BF16)
    return jnp.concatenate([hi, hi, lo], axis=0)


def _stack_heads(x, lane_head):
    return jnp.concatenate([jnp.where(lane_head == h, x, jnp.zeros_like(x)) for h in range(N_HEADS)], axis=0)


def _inproj_body(x_ref, g_ref, w_ref, o_ref, u_scr):
    @pl.when(pl.program_id(1) == 0)
    def _():
        u_scr[...] = _rms_rows(x_ref[...], g_ref[...]).astype(BF16)

    o_ref[...] = _dot(u_scr[...], w_ref[...])


def _inproj(x2d, g, w):
    n, d = x2d.shape
    nc = w.shape[1]
    tm = min(n, 1024)
    tn = nc
    return pl.pallas_call(
        _inproj_body,
        out_shape=jax.ShapeDtypeStruct((n, nc), F32),
        grid=(n // tm, nc // tn),
        in_specs=[pl.BlockSpec((tm, d), lambda i, j: (i, 0)),
                  pl.BlockSpec((1, d), lambda i, j: (0, 0)),
                  pl.BlockSpec((d, tn), lambda i, j: (0, j), pipeline_mode=pl.Buffered(1))],
        out_specs=pl.BlockSpec((tm, tn), lambda i, j: (i, j)),
        scratch_shapes=[pltpu.VMEM((tm, d), BF16)],
        compiler_params=_params(("parallel", "arbitrary")),
        name="inproj",
    )(x2d, g, w)


def _rglru_body(t_len, tp, axg_ref, h0_ref, cbuf_ref, cw_ref, cb_ref, wgx_ref, bgx_ref, wga_ref, bga_ref,
                lam_ref, y_ref, hl_ref, nb_ref, xs_scr, a_scr, b_scr):
    if tp != t_len:
        xs_scr[...] = jnp.zeros(xs_scr.shape, F32)
    xs_scr[pl.ds(5, 3), :] = jnp.concatenate([cbuf_ref[0], jnp.zeros((3, BRANCH_W), F32)], axis=1)
    xs_scr[pl.ds(8, t_len), :] = axg_ref[0]
    x0 = xs_scr[pl.ds(8, tp), 0:BRANCH_W]
    x1 = xs_scr[pl.ds(7, tp), 0:BRANCH_W]
    x2 = xs_scr[pl.ds(6, tp), 0:BRANCH_W]
    x3 = xs_scr[pl.ds(5, tp), 0:BRANCH_W]
    cw = cw_ref[...]
    xc = cb_ref[...] + cw[3:4] * x0 + cw[2:3] * x1 + cw[1:2] * x2 + cw[0:1] * x3
    gate_x = jax.nn.sigmoid(_dot_split(xc, wgx_ref[...]) + bgx_ref[...])
    gate_a = jax.nn.sigmoid(_dot_split(xc, wga_ref[...]) + bga_ref[...])
    log_a = -A_C * gate_a * jax.nn.softplus(-lam_ref[...])
    a = jnp.exp(log_a)
    th = jnp.tanh(log_a)
    b_in = jnp.sqrt(-2.0 * th / (1.0 - th)) * (gate_x * xc)
    if tp != t_len:
        row = lax.broadcasted_iota(jnp.int32, (tp, 1), 0)
        a = jnp.where(row < t_len, a, 1.0)
        b_in = jnp.where(row < t_len, b_in, 0.0)
    a_scr[...] = a
    b_scr[...] = b_in

    row8 = lax.broadcasted_iota(jnp.int32, (8, 1), 0)

    def group(g, carry):
        r0 = pl.multiple_of(g * 8, 8)
        ag = a_scr[pl.ds(r0, 8), :]
        bg = b_scr[pl.ds(r0, 8), :]
        for s in (1, 2, 4):
            a_sh = jnp.where(row8 >= s, pltpu.roll(ag, s, 0), 1.0)
            b_sh = jnp.where(row8 >= s, pltpu.roll(bg, s, 0), 0.0)
            bg = ag * b_sh + bg
            ag = ag * a_sh
        h = ag * carry + bg
        b_scr[pl.ds(r0, 8), :] = h
        return h[7:8, :]

    h_last = lax.fori_loop(0, tp // 8, group, h0_ref[0])
    h = b_scr[...]
    gate = xs_scr[pl.ds(8, tp), BRANCH_W:2 * BRANCH_W]
    y = h * jax.nn.gelu(gate)
    y_ref[0] = y[0:t_len]
    hl_ref[0] = h_last
    nb_ref[0] = xs_scr[pl.ds(8 + t_len - 3, 3), 0:BRANCH_W]


def _rglru(cols3, h0, cbuf, p):
    b, t, _ = cols3.shape
    tp = _round_up(t, 8)
    body = functools.partial(_rglru_body, t, tp)
    vec = pl.BlockSpec((1, BRANCH_W), lambda i: (0, 0))
    mat = pl.BlockSpec((3 * BRANCH_W, BRANCH_W), lambda i: (0, 0))
    return pl.pallas_call(
        body,
        out_shape=(jax.ShapeDtypeStruct((b, t, BRANCH_W), F32),
                   jax.ShapeDtypeStruct((b, 1, BRANCH_W), F32),
                   jax.ShapeDtypeStruct((b, 3, BRANCH_W), F32)),
        grid=(b,),
        in_specs=[pl.BlockSpec((1, t, 2 * BRANCH_W), lambda i: (i, 0, 2)),
                  pl.BlockSpec((1, 1, BRANCH_W), lambda i: (i, 0, 0)),
                  pl.BlockSpec((1, 3, BRANCH_W), lambda i: (i, 0, 0)),
                  pl.BlockSpec((4, BRANCH_W), lambda i: (0, 0)),
                  vec, mat, vec, mat, vec, vec],
        out_specs=(pl.BlockSpec((1, t, BRANCH_W), lambda i: (i, 0, 0)),
                   pl.BlockSpec((1, 1, BRANCH_W), lambda i: (i, 0, 0)),
                   pl.BlockSpec((1, 3, BRANCH_W), lambda i: (i, 0, 0))),
        scratch_shapes=[pltpu.VMEM((tp + 8, 2 * BRANCH_W), F32),
                        pltpu.VMEM((tp, BRANCH_W), F32),
                        pltpu.VMEM((tp, BRANCH_W), F32)],
        compiler_params=_params(("parallel",)),
        name="rglru",
    )(cols3, h0, cbuf, p["a_conv_w"], p["a_conv_b"], p["a_gx_bd"], p["a_gx_b"], p["a_ga_bd"], p["a_ga_b"],
      p["a_lambda"])


RWKV_GROUP = 4


def _rwkv_consts(tc, ch):
    t = np.arange(tc)
    same = (t[:, None] // ch) == (t[None, :] // ch)
    tril = same & (t[None, :] <= t[:, None])
    return jnp.asarray(np.concatenate([tril, same], axis=0).astype(np.float32), BF16)


def _rwkv_body(t_len, tc, n_t, ch, cb_ref, shift_ref, s0_ref, mu_ref, w0_ref, w2_ref, a0_ref, a2_ref, g2_ref,
               kk_ref, ka_ref, rk_ref, lnw_ref, lnb_ref, cs_ref, e3_ref, y_ref, last_ref, sn_ref,
               xs_scr, s_scr, at_scr, bt_scr, kt_scr, bh_scr, kh_scr, v_scr, rt_scr, gc_scr, y_scr, bon_scr,
               g_scr):
    ti = pl.program_id(1)
    t_blk = min(t_len, tc)
    cs = N_HEADS * ch

    @pl.when(ti == 0)
    def _():
        if t_blk != tc:
            xs_scr[...] = jnp.zeros(xs_scr.shape, F32)
        xs_scr[pl.ds(7, 1), :] = shift_ref[0]
        s_scr[...] = jnp.zeros(s_scr.shape, F32)
        for h in range(N_HEADS):
            s_scr[h * HEAD_W:(h + 1) * HEAD_W, h * HEAD_W:(h + 1) * HEAD_W] = s0_ref[0, h]

    xs_scr[pl.ds(8, t_blk), :] = cb_ref[0]
    cb = xs_scr[pl.ds(8, tc), :]
    shifted = xs_scr[pl.ds(7, tc), :]
    cm = cb + (shifted - cb) * mu_ref[...]
    r = cm[:, 0:256]
    k = cm[:, 256:512]
    v = cm[:, 512:768]
    lora = cm[:, 768:896]
    w = -jax.nn.softplus(-(w0_ref[...] + _dot_split(jnp.tanh(lora), w2_ref[...]))) - 0.5
    log_w = -jnp.exp(w)
    a = jax.nn.sigmoid(a0_ref[...] + _dot_split(lora, a2_ref[...]))
    g_scr[...] = _dot_split(jax.nn.sigmoid(cm[:, 896:1024]), g2_ref[...])
    e3 = e3_ref[...]
    kk = k * kk_ref[...]
    kk = kk / jnp.maximum(jnp.sqrt(_head_sum(kk * kk, e3)), 1e-12)
    k2 = k * (1.0 + (a - 1.0) * ka_ref[...])
    bon_scr[...] = _head_sum(r * k2 * rk_ref[...], e3) * v
    a_s = -kk
    b_s = kk * a
    if t_blk != tc:
        row = lax.broadcasted_iota(jnp.int32, (tc, 1), 0)
        live = row < t_blk
        log_w = jnp.where(live, log_w, 0.0)
        a_s = jnp.where(live, a_s, 0.0)
        b_s = jnp.where(live, b_s, 0.0)
        k2 = jnp.where(live, k2, 0.0)
    parts = jnp.concatenate(_split3(log_w), axis=1)
    sub = cs_ref.shape[1]
    cum, tot = [], []
    for j in range(tc // sub):
        cc = _dot(cs_ref[...], parts[j * sub:(j + 1) * sub, :])
        cc = cc[:, 0:256] + cc[:, 256:512] + cc[:, 512:768]
        cum.append(cc[0:sub])
        tot.append(cc[sub:2 * sub])
    cum = jnp.concatenate(cum, axis=0)
    tot = jnp.concatenate(tot, axis=0)
    inv = jnp.exp(-cum)
    tail = jnp.exp(tot - cum)
    at_scr[...] = (a_s * jnp.exp(cum - log_w)).astype(BF16)
    bt_scr[...] = (b_s * inv).astype(BF16)
    kt_scr[...] = (k2 * inv).astype(BF16)
    bh_scr[...] = (b_s * tail).astype(BF16)
    kh_scr[...] = (k2 * tail).astype(BF16)
    v_scr[...] = v.astype(BF16)
    rt_scr[...] = r * jnp.exp(cum)
    gc_scr[...] = jnp.exp(tot)

    row_t = lax.broadcasted_iota(jnp.int32, (cs, cs), 0) & (ch - 1)
    col_t = lax.broadcasted_iota(jnp.int32, (cs, cs), 1) & (ch - 1)
    strict = col_t < row_t
    row_t2 = lax.broadcasted_iota(jnp.int32, (cs, 2 * cs), 0) & (ch - 1)
    col_t2 = lax.broadcasted_iota(jnp.int32, (cs, 2 * cs), 1) & (ch - 1)
    incl2 = col_t2 <= row_t2
    eye = (lax.broadcasted_iota(jnp.int32, (cs, cs), 0) == lax.broadcasted_iota(jnp.int32, (cs, cs), 1)).astype(F32)
    lane_head = lax.broadcasted_iota(jnp.int32, (ch, BRANCH_W), 1) >> HEAD_SHIFT
    n_dbl = ch.bit_length() - 1

    def prepare(c):
        r0 = c * ch if isinstance(c, int) else pl.multiple_of(c * ch, ch)
        abd, bbd, kbd, bhd, khd, vbd = (_stack_heads(s[pl.ds(r0, ch), :], lane_head)
                                        for s in (at_scr, bt_scr, kt_scr, bh_scr, kh_scr, v_scr))
        rbd = _stack_heads(rt_scr[pl.ds(r0, ch), :], lane_head)
        s1 = _dot_nt(jnp.concatenate([abd, rbd.astype(BF16)], axis=0), jnp.concatenate([bbd, kbd], axis=0))
        l_ab = jnp.where(strict, s1[0:cs, 0:cs], 0.0)
        l_ak = jnp.where(strict, s1[0:cs, cs:2 * cs], 0.0)
        m_r = jnp.where(incl2, s1[cs:2 * cs, :], 0.0).astype(BF16)
        tinv = eye + l_ab
        lb16 = l_ab.astype(BF16)
        yield
        pw = _dot(lb16, lb16)
        for i in range(1, n_dbl):
            pw16 = pw.astype(BF16)
            yield
            if i < n_dbl - 1:
                both = _dot(jnp.concatenate([tinv.astype(BF16), pw16], axis=0), pw16)
                tinv = tinv + both[0:cs]
                pw = both[cs:2 * cs]
            else:
                tinv = tinv + _dot(tinv.astype(BF16), pw16)
        yield
        w1 = _dot(l_ak.astype(BF16), vbd)
        yield
        x = _dot(tinv.astype(BF16), jnp.concatenate([abd, w1.astype(BF16)], axis=1))
        yield
        z = _dot(m_r[:, 0:cs], x.astype(BF16))
        r_hat = rbd + z[:, 0:BRANCH_W]
        yield
        y_hat = z[:, BRANCH_W:2 * BRANCH_W] + _dot(m_r[:, cs:2 * cs], vbd)
        ar = jnp.concatenate([x[:, 0:BRANCH_W], r_hat], axis=0).astype(BF16)
        g_end = gc_scr[pl.ds(r0, ch), :][0:1, :]
        return ar, x[:, BRANCH_W:2 * BRANCH_W], y_hat, vbd, jnp.concatenate([bhd, khd], axis=0), g_end

    def advance(c0, prepared):
        for n, (ar, u_hat, y_hat, vbd, bk, g_end) in enumerate(prepared):
            c = c0 + n
            r0 = c * ch if isinstance(c, int) else pl.multiple_of(c * ch, ch)
            sb = s_scr[...]
            uy = _dot_nt(ar, sb.astype(BF16))
            u = uy[0:cs] + u_hat
            ys = uy[cs:2 * cs] + y_hat
            y_scr[pl.ds(r0, ch), :] = ys[0:ch] + ys[ch:2 * ch] + ys[2 * ch:3 * ch] + ys[3 * ch:4 * ch]
            yield
            s_scr[...] = sb * g_end + _dot_tn(jnp.concatenate([u.astype(BF16), vbd], axis=0), bk)
            yield

    def interleave(gens):
        done = [None] * len(gens)
        live = list(range(len(gens)))
        while live:
            for i in list(live):
                try:
                    next(gens[i])
                except StopIteration as stop:
                    done[i] = stop.value
                    live.remove(i)
        return done

    n_chunk = tc // ch
    grp = RWKV_GROUP if n_chunk % RWKV_GROUP == 0 else 1
    first = tuple(interleave([prepare(c) for c in range(grp)]))
    if n_chunk > grp:
        def group(i, carry):
            c0 = i * grp
            res = interleave([prepare(c0 + grp + n) for n in range(grp)] + [advance(c0, carry)])
            return tuple(res[0:grp])

        first = lax.fori_loop(0, n_chunk // grp - 1, group, first)
    interleave([advance(n_chunk - grp, first)])

    y = y_scr[...]
    mean = _head_sum(y, e3) * (1.0 / HEAD_W)
    dev = y - mean
    var = _head_sum(dev * dev, e3) * (1.0 / HEAD_W)
    yn = dev * lax.rsqrt(var + B_GN_EPS) * lnw_ref[...] + lnb_ref[...]
    out = (yn + bon_scr[...]) * g_scr[...]
    y_ref[0] = out[0:t_blk]
    xs_scr[pl.ds(7, 1), :] = xs_scr[pl.ds(8 + t_blk - 1, 1), :]

    @pl.when(ti == n_t - 1)
    def _():
        last_ref[0] = xs_scr[pl.ds(7, 1), :]
        for h in range(N_HEADS):
            sn_ref[0, h] = s_scr[h * HEAD_W:(h + 1) * HEAD_W, h * HEAD_W:(h + 1) * HEAD_W]


def _rwkv(cols3, shift, s0, p):
    b, t, _ = cols3.shape
    if t >= 64:
        ch, tc = 64, min(t, 1024)
        assert t % tc == 0
    else:
        ch = 16
        tc = _round_up(t, ch)
    n_t = max(1, t // tc)
    t_blk = min(t, tc)
    body = functools.partial(_rwkv_body, t, tc, n_t, ch)
    consts = _rwkv_consts(min(tc, 256), ch)
    vec = lambda w: pl.BlockSpec((1, w), lambda i, j: (0, 0))
    sc16 = lambda: pltpu.VMEM((tc, BRANCH_W), BF16)
    sc32 = lambda: pltpu.VMEM((tc, BRANCH_W), F32)
    return pl.pallas_call(
        body,
        out_shape=(jax.ShapeDtypeStruct((b, t, BRANCH_W), F32),
                   jax.ShapeDtypeStruct((b, 1, B_COLS), F32),
                   jax.ShapeDtypeStruct((b, N_HEADS, HEAD_W, HEAD_W), F32)),
        grid=(b, n_t),
        in_specs=[pl.BlockSpec((1, t_blk, B_COLS), lambda i, j: (i, j, 0)),
                  pl.BlockSpec((1, 1, B_COLS), lambda i, j: (i, 0, 0)),
                  pl.BlockSpec((1, N_HEADS, HEAD_W, HEAD_W), lambda i, j: (i, 0, 0, 0)),
                  vec(B_COLS), vec(BRANCH_W),
                  pl.BlockSpec((384, BRANCH_W), lambda i, j: (0, 0)),
                  vec(BRANCH_W),
                  pl.BlockSpec((384, BRANCH_W), lambda i, j: (0, 0)),
                  pl.BlockSpec((384, BRANCH_W), lambda i, j: (0, 0)),
                  vec(BRANCH_W), vec(BRANCH_W), vec(BRANCH_W), vec(BRANCH_W), vec(BRANCH_W),
                  _resident(consts.shape), _resident(p["e3"].shape)],
        out_specs=(pl.BlockSpec((1, t_blk, BRANCH_W), lambda i, j: (i, j, 0)),
                   pl.BlockSpec((1, 1, B_COLS), lambda i, j: (i, 0, 0)),
                   pl.BlockSpec((1, N_HEADS, HEAD_W, HEAD_W), lambda i, j: (i, 0, 0, 0))),
        scratch_shapes=[pltpu.VMEM((tc + 8, B_COLS), F32),
                        pltpu.VMEM((BRANCH_W, BRANCH_W), F32),
                        sc16(), sc16(), sc16(), sc16(), sc16(), sc16(),
                        sc32(), sc32(), sc32(), sc32(), sc32()],
        compiler_params=_params(("parallel", "arbitrary")),
        name="rwkv7",
    )(cols3, shift, s0, p["b_mu"], p["b_w0"], p["b_w2p"], p["b_a0"], p["b_a2p"], p["b_g2"],
      p["b_k_k"], p["b_k_a"], p["b_r_k"], p["b_ln_w"], p["b_ln_b"], consts, p["e3"])


def _hgrn_consts(tc):
    nl = tc.bit_length() - 1
    t = np.arange(tc)
    tril = (t[None, :] <= t[:, None]).astype(np.float32)
    x = t[:, None] ^ t[None, :]
    lvl = np.floor(np.log2(np.maximum(x, 1))).astype(np.int32)
    lvl = np.where(t[:, None] > t[None, :], lvl, np.where(t[:, None] == t[None, :], nl, -1))
    return jnp.asarray(tril, BF16), jnp.asarray(lvl, jnp.int32), nl


def _hgrn_body(t_len, tc, n_t, nl, cq_ref, cf_ref, cig_ref, s0_ref, lb_ref, ng_ref, sel_ref, lvl_ref, e3_ref,
               y_ref, sn_ref, in_scr, st_scr):
    ti = pl.program_id(1)
    t_blk = min(t_len, tc)

    @pl.when(ti == 0)
    def _():
        st_scr[...] = jnp.zeros(st_scr.shape, F32)
        for h in range(N_HEADS):
            st_scr[h * HEAD_W:(h + 1) * HEAD_W, h * C_EXPAND:(h + 1) * C_EXPAND] = s0_ref[0, h].T

    if t_blk != tc:
        in_scr[...] = jnp.zeros(in_scr.shape, F32)
        in_scr[0, pl.ds(0, t_blk), :] = cq_ref[0]
        in_scr[1, pl.ds(0, t_blk), :] = cf_ref[0]
        in_scr[2, pl.ds(0, t_blk), :] = cig_ref[0]
        q, f, cig = in_scr[0], in_scr[1], in_scr[2]
    else:
        q, f, cig = cq_ref[0], cf_ref[0], cig_ref[0]
    lb = lb_ref[...]
    fg = lb + (1.0 - lb) * jax.nn.sigmoid(f)
    log_f = jnp.log(fg)
    kk = 1.0 - fg
    fg_live = fg
    if t_blk != tc:
        row = lax.broadcasted_iota(jnp.int32, (tc, 1), 0)
        live = row < t_blk
        log_f = jnp.where(live, log_f, 0.0)
        kk = jnp.where(live, kk, 0.0)
        fg_live = jnp.where(live, fg, 1.0)
    hi = log_f.astype(BF16)
    lo = (log_f - hi.astype(F32)).astype(BF16)
    parts = jnp.concatenate([hi, lo], axis=1)

    d = _dot(sel_ref[...], parts)
    cum = d[:, 0:C_FDIM] + d[:, C_FDIM:2 * C_FDIM]
    odd = (lax.broadcasted_iota(jnp.int32, (tc, 1), 0) & 1) == 1

    lvl = lvl_ref[...]
    att = [None] * N_HEADS
    for l in range(nl + 1):
        if l == 0:
            e = jnp.where(odd, fg_live, 1.0)
        elif l < nl:
            m = 1 << l
            ref_rows = cum.reshape(tc // (2 * m), 2 * m, C_FDIM)[:, m - 1:m, :]
            ref_rows = jnp.broadcast_to(ref_rows, (tc // (2 * m), 2 * m, C_FDIM)).reshape(tc, C_FDIM)
            e = jnp.exp(-jnp.abs(cum - ref_rows))
        if l < nl:
            qs = (q * e).astype(BF16)
            ks = (kk * e).astype(BF16)
        else:
            qs = q.astype(BF16)
            ks = kk.astype(BF16)
        hit = lvl == l
        for h in range(N_HEADS):
            kl = slice(h * C_EXPAND, (h + 1) * C_EXPAND)
            a = _dot_nt(qs[:, kl], ks[:, kl])
            att[h] = jnp.where(hit, a, 0.0 if att[h] is None else att[h])
    c_last = cum[tc - 1:tc, :]
    lane_head = lax.broadcasted_iota(jnp.int32, (tc, BRANCH_W), 1) >> HEAD_SHIFT
    v = cig[:, 0:BRANCH_W]
    v_stack = _stack_heads(v, lane_head).astype(BF16)
    p_cat = jnp.concatenate([a.astype(BF16) for a in att], axis=1)
    st = st_scr[...]
    o = _dot(p_cat, v_stack) + _dot_nt((q * jnp.exp(cum)).astype(BF16), st.astype(BF16))
    ms = _head_sum(o * o, e3_ref[...]) * (1.0 / HEAD_W)
    y = o * lax.rsqrt(ms + NORM_EPS) * ng_ref[...] * jax.nn.silu(cig[:, BRANCH_W:2 * BRANCH_W])
    y_ref[0] = y[0:t_blk]
    upd = _dot_tn(v.astype(BF16), (kk * jnp.exp(c_last - cum)).astype(BF16))
    rh = lax.broadcasted_iota(jnp.int32, (BRANCH_W, C_FDIM), 0) >> HEAD_SHIFT
    ch_ = lax.broadcasted_iota(jnp.int32, (BRANCH_W, C_FDIM), 1) >> C_EXPAND_SHIFT
    st_scr[...] = st * jnp.exp(c_last) + jnp.where(rh == ch_, upd, 0.0)

    @pl.when(ti == n_t - 1)
    def _():
        for h in range(N_HEADS):
            sn_ref[0, h] = st_scr[h * HEAD_W:(h + 1) * HEAD_W, h * C_EXPAND:(h + 1) * C_EXPAND].T


def _hgrn(cols3, s0, lb, ng, e3):
    b, t, _ = cols3.shape
    tc = 256 if t >= 256 else 16
    assert t % tc == 0 or t < tc
    n_t = max(1, t // tc)
    t_blk = min(t, tc)
    sel, lvl, nl = _hgrn_consts(tc)
    body = functools.partial(_hgrn_body, t, tc, n_t, nl)
    blk = lambda idx: pl.BlockSpec((1, t_blk, C_FDIM), lambda i, j: (i, j, idx))
    pad_shape = (3, tc, C_FDIM) if t_blk != tc else (1, 8, 128)
    return pl.pallas_call(
        body,
        out_shape=(jax.ShapeDtypeStruct((b, t, BRANCH_W), F32),
                   jax.ShapeDtypeStruct((b, N_HEADS, C_EXPAND, HEAD_W), F32)),
        grid=(b, n_t),
        in_specs=[blk(3), blk(4), blk(5),
                  pl.BlockSpec((1, N_HEADS, C_EXPAND, HEAD_W), lambda i, j: (i, 0, 0, 0)),
                  pl.BlockSpec((1, C_FDIM), lambda i, j: (0, 0)),
                  pl.BlockSpec((1, BRANCH_W), lambda i, j: (0, 0)),
                  _resident(sel.shape), _resident(lvl.shape), _resident(e3.shape)],
        out_specs=(pl.BlockSpec((1, t_blk, BRANCH_W), lambda i, j: (i, j, 0)),
                   pl.BlockSpec((1, N_HEADS, C_EXPAND, HEAD_W), lambda i, j: (i, 0, 0, 0))),
        scratch_shapes=[pltpu.VMEM(pad_shape, F32),
                        pltpu.VMEM((BRANCH_W, C_FDIM), F32)],
        compiler_params=_params(("parallel", "arbitrary")),
        name="hgrn2",
    )(cols3, cols3, cols3, s0, lb, ng, sel, lvl, e3)


def _rope_rows(x, cos, sin_signed):
    lane = lax.broadcasted_iota(jnp.int32, x.shape, 1)
    swapped = jnp.where((lane & (HEAD_W - 1)) < HEAD_W // 2,
                        pltpu.roll(x, BRANCH_W - HEAD_W // 2, 1), pltpu.roll(x, HEAD_W // 2, 1))
    return x * cos + swapped * sin_signed


def _pattern_weight(d):
    ok = d >= 0
    w = ((d <= 128).astype(F32) + (((d & 3) == 0) & (d <= 512)).astype(F32)
         + (((d & 15) == 0) & (d <= 2048)).astype(F32))
    return jnp.where(ok, w, 0.0)


def _att_group(n_blk):
    return next(g for g in (4, 2, 1) if n_blk % g == 0)


def _att_weight_table(n_blk):
    grp = _att_group(n_blk)
    r = np.arange(ATT_BLK)
    delta = np.arange(-(grp - 1), n_blk)
    d = delta[:, None, None] * ATT_BLK + r[None, None, :] - r[None, :, None]
    w = ((d <= 128).astype(np.float32) + ((d % 4 == 0) & (d <= 512)) + ((d % 16 == 0) & (d <= 2048)))
    return jnp.asarray(np.where(d >= 0, w, 0.0), F32)


def _attn_prompt_body(n_blk, qkv_ref, cos_ref, sin_ref, wt_ref, y_ref, kt_ref, vt_ref, qt_scr, kb_scr, vt_scr,
                      s_scr):
    qkv = qkv_ref[0]
    cos = cos_ref[...]
    sin = sin_ref[...]
    q = _rope_rows(qkv[:, 0:BRANCH_W], cos, sin) * (HEAD_W ** -0.5)
    k = _rope_rows(qkv[:, BRANCH_W:2 * BRANCH_W], cos, sin)
    v_t = qkv[:, 2 * BRANCH_W:3 * BRANCH_W].T
    kb_scr[...] = k.astype(BF16)
    qt_scr[...] = q.T.astype(BF16)
    vt_scr[...] = v_t.astype(BF16)
    kt_ref[0] = k.T
    vt_ref[0] = v_t
    n_lane = N_HEADS * ATT_BLK
    row_head = lax.broadcasted_iota(jnp.int32, (BRANCH_W, n_lane), 0) >> HEAD_SHIFT
    col_head = lax.broadcasted_iota(jnp.int32, (BRANCH_W, n_lane), 1) >> 7
    own_head = row_head == col_head
    grp = _att_group(n_blk)

    def q_block(i, _):
        q0 = pl.multiple_of(i * ATT_BLK, ATT_BLK)
        qt = qt_scr[:, pl.ds(q0, ATT_BLK)]
        q_bd = jnp.where(own_head, jnp.concatenate([qt] * N_HEADS, axis=1), jnp.zeros((), BF16))

        def weights(j):
            w1 = wt_ref[i - j + (grp - 1)]
            return jnp.concatenate([w1] * N_HEADS, axis=1)

        n_grp = (i + grp) // grp

        def scores(gi):
            g_max = None
            for n in range(grp):
                j = gi * grp + n
                s = _dot(kb_scr[pl.ds(pl.multiple_of(j * ATT_BLK, ATT_BLK), ATT_BLK), :], q_bd)
                s = jnp.where(weights(j) > 0.0, s, NEG_BIG)
                s_scr[gi & 1, n] = s
                g_max = s if g_max is None else jnp.maximum(g_max, s)
            return jnp.max(g_max, axis=0, keepdims=True)

        def update(gi, g_max, m_run, l_run, acc):
            m_new = jnp.maximum(m_run, g_max)
            alpha = jnp.exp(m_run - m_new)
            l_new = alpha * l_run
            pv = [None] * N_HEADS
            for n in range(grp):
                j = gi * grp + n
                p = weights(j) * jnp.exp(s_scr[gi & 1, n] - m_new)
                l_new = l_new + jnp.sum(p, axis=0, keepdims=True)
                pb = p.astype(BF16)
                for h in range(N_HEADS):
                    vt = vt_scr[h * HEAD_W:(h + 1) * HEAD_W, pl.ds(pl.multiple_of(j * ATT_BLK, ATT_BLK), ATT_BLK)]
                    term = _dot(vt, pb[:, h * ATT_BLK:(h + 1) * ATT_BLK])
                    pv[h] = term if pv[h] is None else pv[h] + term
            heads = [alpha[:, h * ATT_BLK:(h + 1) * ATT_BLK] * acc[h * HEAD_W:(h + 1) * HEAD_W, :] + pv[h]
                     for h in range(N_HEADS)]
            return m_new, l_new, jnp.concatenate(heads, axis=0)

        def step(gi, carry):
            g_max, m_run, l_run, acc = carry
            nxt = scores(gi + 1)
            return (nxt,) + update(gi, g_max, m_run, l_run, acc)

        init = (scores(0), jnp.full((1, n_lane), NEG_BIG, F32), jnp.zeros((1, n_lane), F32),
                jnp.zeros((BRANCH_W, ATT_BLK), F32))
        g_max, m_run, l_run, acc = lax.fori_loop(0, n_grp - 1, step, init)
        _, l_run, acc = update(n_grp - 1, g_max, m_run, l_run, acc)
        out_t = jnp.concatenate([acc[h * HEAD_W:(h + 1) * HEAD_W, :] / l_run[:, h * ATT_BLK:(h + 1) * ATT_BLK]
                                 for h in range(N_HEADS)], axis=0)
        y_ref[0, pl.ds(q0, ATT_BLK), :] = out_t.T
        return 0

    lax.fori_loop(0, n_blk, q_block, 0)


def _attn_prompt(cols3, cos, sin):
    b, t, _ = cols3.shape
    n_blk = t // ATT_BLK
    assert n_blk * ATT_BLK == t
    body = functools.partial(_attn_prompt_body, n_blk)
    wtab = _att_weight_table(n_blk)
    return pl.pallas_call(
        body,
        out_shape=(jax.ShapeDtypeStruct((b, t, BRANCH_W), F32), jax.ShapeDtypeStruct((b, BRANCH_W, t), F32),
                   jax.ShapeDtypeStruct((b, BRANCH_W, t), F32)),
        grid=(b,),
        in_specs=[pl.BlockSpec((1, t, 3 * BRANCH_W), lambda i: (i, 0, 4)),
                  _resident((t, BRANCH_W)), _resident((t, BRANCH_W)), _resident(wtab.shape)],
        out_specs=(pl.BlockSpec((1, t, BRANCH_W), lambda i: (i, 0, 0)),
                   pl.BlockSpec((1, BRANCH_W, t), lambda i: (i, 0, 0)),
                   pl.BlockSpec((1, BRANCH_W, t), lambda i: (i, 0, 0))),
        scratch_shapes=[pltpu.VMEM((BRANCH_W, t), BF16),
                        pltpu.VMEM((t, BRANCH_W), BF16),
                        pltpu.VMEM((BRANCH_W, t), BF16),
                        pltpu.VMEM((2, _att_group(n_blk), ATT_BLK, N_HEADS * ATT_BLK), F32)],
        compiler_params=_params(("parallel",)),
        name="attn_prompt",
    )(cols3, cos, sin, wtab)


def _attn_sample_body(t_len, l_cache, qkv_ref, kc_ref, vc_ref, cos_ref, sin_ref, y_ref, k_ref, pad_scr):
    tp = 8
    pad_scr[...] = jnp.zeros(pad_scr.shape, F32)
    pad_scr[pl.ds(0, t_len), :] = qkv_ref[0]
    qkv = pad_scr[...]
    cos = cos_ref[...]
    sin = sin_ref[...]
    q = _rope_rows(qkv[:, 0:BRANCH_W], cos, sin) * (HEAD_W ** -0.5)
    k_new = _rope_rows(qkv[:, BRANCH_W:2 * BRANCH_W], cos, sin)
    v_new = qkv[:, 2 * BRANCH_W:3 * BRANCH_W]
    k_ref[0] = k_new[0:t_len]
    lane_head = lax.broadcasted_iota(jnp.int32, (tp, BRANCH_W), 1) >> HEAD_SHIFT
    q_bd = jnp.concatenate([jnp.where(lane_head == h, q, 0.0) for h in range(N_HEADS)], axis=0).astype(BF16)
    n_row = N_HEADS * tp
    k_t = kc_ref[0, 0].reshape(BRANCH_W, l_cache)
    v_t = vc_ref[0, 0].reshape(BRANCH_W, l_cache)
    s_c = _dot(q_bd, k_t.astype(BF16))
    s_n = _dot_nt(q_bd, k_new.astype(BF16))
    t_row = lax.broadcasted_iota(jnp.int32, (n_row, 1), 0) & (tp - 1)
    col_c = lax.broadcasted_iota(jnp.int32, (n_row, l_cache), 1)
    w_c = _pattern_weight(l_cache + t_row - col_c)
    col_n = lax.broadcasted_iota(jnp.int32, (n_row, tp), 1)
    w_n = jnp.where(col_n < t_len, _pattern_weight(t_row - col_n), 0.0)
    s_c = jnp.where(w_c > 0.0, s_c, NEG_BIG)
    s_n = jnp.where(w_n > 0.0, s_n, NEG_BIG)
    m = jnp.maximum(jnp.max(s_c, axis=-1, keepdims=True), jnp.max(s_n, axis=-1, keepdims=True))
    p_c = w_c * jnp.exp(s_c - m)
    p_n = w_n * jnp.exp(s_n - m)
    l = jnp.sum(p_c, axis=-1, keepdims=True) + jnp.sum(p_n, axis=-1, keepdims=True)
    num = _dot_nt(p_c.astype(BF16), v_t.astype(BF16)) + _dot(p_n.astype(BF16), v_new.astype(BF16))
    ratio = num / l
    out = jnp.zeros((tp, BRANCH_W), F32)
    for h in range(N_HEADS):
        out = jnp.where(lane_head == h, ratio[h * tp:(h + 1) * tp, :], out)
    y_ref[0] = out[0:t_len]


def _attn_sample(cols3, k_cache_t, v_cache_t, layer, cos, sin):
    b, t, _ = cols3.shape
    l_cache = k_cache_t.shape[-1]
    assert t <= 8
    body = functools.partial(_attn_sample_body, t, l_cache)
    cache_spec = pl.BlockSpec((1, 1, N_HEADS, HEAD_W, l_cache), lambda i: (layer, i, 0, 0, 0))
    return pl.pallas_call(
        body,
        out_shape=(jax.ShapeDtypeStruct((b, t, BRANCH_W), F32), jax.ShapeDtypeStruct((b, t, BRANCH_W), F32)),
        grid=(b,),
        in_specs=[pl.BlockSpec((1, t, 3 * BRANCH_W), lambda i: (i, 0, 4)),
                  cache_spec, cache_spec,
                  pl.BlockSpec((8, BRANCH_W), lambda i: (0, 0)),
                  pl.BlockSpec((8, BRANCH_W), lambda i: (0, 0))],
        out_specs=(pl.BlockSpec((1, t, BRANCH_W), lambda i: (i, 0, 0)),
                   pl.BlockSpec((1, t, BRANCH_W), lambda i: (i, 0, 0))),
        scratch_shapes=[pltpu.VMEM((8, 3 * BRANCH_W), F32)],
        compiler_params=_params(("parallel",)),
        name="attn_sample",
    )(cols3, k_cache_t, v_cache_t, cos, sin)


def _merge_body(x_ref, g_ref, wgt_ref, ya_ref, yb_ref, yc_ref, yd_ref, wb_ref, wo_ref, o_ref):
    x = x_ref[...]
    u = _rms_rows(x, g_ref[...]).astype(BF16)
    merged = None
    for n, y_ref in enumerate((ya_ref, yb_ref, yc_ref, yd_ref)):
        z = _dot(y_ref[...].astype(BF16), wb_ref[n])
        gate = jax.nn.sigmoid(_dot(u, wgt_ref[:, n * D_MODEL:(n + 1) * D_MODEL]))
        merged = gate * z if merged is None else merged + gate * z
    o_ref[...] = x + _dot(merged.astype(BF16), wo_ref[...])


def _merge(x2d, g, wgt, ys, wb, wo):
    n, d = x2d.shape
    tm = min(n, 512)
    row = lambda w: pl.BlockSpec((tm, w), lambda i: (i, 0))
    return pl.pallas_call(
        _merge_body,
        out_shape=jax.ShapeDtypeStruct((n, d), F32),
        grid=(n // tm,),
        in_specs=[row(d), pl.BlockSpec((1, d), lambda i: (0, 0)), _resident(wgt.shape),
                  row(BRANCH_W), row(BRANCH_W), row(BRANCH_W), row(BRANCH_W),
                  _resident(wb.shape), _resident(wo.shape)],
        out_specs=row(d),
        compiler_params=_params(("parallel",)),
        name="merge",
    )(x2d, g, wgt, *ys, wb, wo)


FFN_COL = 1024


def _ffn_columns(x, v, prev1, prev2, first, second, wg_ref, wu_ref, wd_ref, cw_ref, cb_ref, keep_gate):
    acc = jnp.zeros(x.shape, F32)
    cw = cw_ref[...]
    for c in range(D_FF // FFN_COL):
        sl = slice(c * FFN_COL, (c + 1) * FFN_COL)
        hg = _dot(v, wg_ref[:, sl])
        hu = _dot(v, wu_ref[:, sl])
        h1 = jnp.where(first, prev1(sl), pltpu.roll(hg, 1, 0))
        h2 = jnp.where(first | second, prev2(sl), pltpu.roll(hg, 2, 0))
        conv = cb_ref[:, sl] + cw[2:3, sl] * hg + cw[1:2, sl] * h1 + cw[0:1, sl] * h2
        hmid = (jax.nn.gelu(conv) * hu).astype(BF16)
        acc = acc + _dot(hmid, wd_ref[sl, :])
        keep_gate(sl, hg)
    return x + acc


def _ffn_body(tm, n_t, has_final, x_ref, g_ref, wg_ref, wu_ref, wd_ref, cw_ref, cb_ref, st_ref, *rest):
    if has_final:
        fg_ref, o_ref, ns_ref, carry_scr = rest
    else:
        o_ref, ns_ref, carry_scr = rest
    ti = pl.program_id(1)

    @pl.when(ti == 0)
    def _():
        carry_scr[...] = st_ref[0]

    x = x_ref[0]
    v = _rms_rows(x, g_ref[...]).astype(BF16)
    row = lax.broadcasted_iota(jnp.int32, (tm, 1), 0)

    def prev2(sl):
        return jnp.where(row == 0, carry_scr[0:1, sl], carry_scr[1:2, sl])

    def keep_gate(sl, hg):
        carry_scr[:, sl] = hg[tm - 2:tm, :]

    y = _ffn_columns(x, v, lambda sl: carry_scr[1:2, sl], prev2, row == 0, row == 1,
                     wg_ref, wu_ref, wd_ref, cw_ref, cb_ref, keep_gate)
    o_ref[0] = _rms_rows(y, fg_ref[...]) if has_final else y

    @pl.when(ti == n_t - 1)
    def _():
        ns_ref[0] = carry_scr[...]


def _ffn_rows_body(seq, has_final, x_ref, g_ref, wg_ref, wu_ref, wd_ref, cw_ref, cb_ref, p1_ref, p2_ref, *rest):
    if has_final:
        fg_ref, o_ref, hg_ref = rest
    else:
        o_ref, hg_ref = rest
    x = x_ref[...]
    v = _rms_rows(x, g_ref[...]).astype(BF16)
    step = lax.broadcasted_iota(jnp.int32, (x.shape[0], 1), 0) & (seq - 1)

    def keep_gate(sl, hg):
        hg_ref[:, sl] = hg

    y = _ffn_columns(x, v, lambda sl: p1_ref[:, sl], lambda sl: p2_ref[:, sl], step == 0, step == 1,
                     wg_ref, wu_ref, wd_ref, cw_ref, cb_ref, keep_gate)
    o_ref[...] = _rms_rows(y, fg_ref[...]) if has_final else y


def _ffn(x3, g, wg, wu, wd, cw, cb, state, final_g):
    b, t, d = x3.shape
    has_final = final_g is not None
    extra_in = [final_g] if has_final else []
    extra_spec2 = [pl.BlockSpec((1, d), lambda i, j: (0, 0))] if has_final else []
    extra_spec1 = [pl.BlockSpec((1, d), lambda i: (0, 0))] if has_final else []
    if t >= 8:
        tm = min(t, 512)
        n_t = t // tm
        assert tm * n_t == t and tm % 8 == 0
        body = functools.partial(_ffn_body, tm, n_t, has_final)
        return pl.pallas_call(
            body,
            out_shape=(jax.ShapeDtypeStruct((b, t, d), F32), jax.ShapeDtypeStruct((b, 2, D_FF), F32)),
            grid=(b, n_t),
            in_specs=[pl.BlockSpec((1, tm, d), lambda i, j: (i, j, 0)),
                      pl.BlockSpec((1, d), lambda i, j: (0, 0)),
                      _resident(wg.shape), _resident(wu.shape), _resident(wd.shape),
                      pl.BlockSpec((3, D_FF), lambda i, j: (0, 0)),
                      pl.BlockSpec((1, D_FF), lambda i, j: (0, 0)),
                      pl.BlockSpec((1, 2, D_FF), lambda i, j: (i, 0, 0))] + extra_spec2,
            out_specs=(pl.BlockSpec((1, tm, d), lambda i, j: (i, j, 0)),
                       pl.BlockSpec((1, 2, D_FF), lambda i, j: (i, 0, 0))),
            scratch_shapes=[pltpu.VMEM((2, D_FF), F32)],
            compiler_params=_params(("parallel", "arbitrary")),
            name="convffn",
        )(x3, g, wg, wu, wd, cw, cb, state, *extra_in)
    assert t >= 2 and t & (t - 1) == 0 and (b * t) % 8 == 0
    n = b * t
    zero = jnp.zeros((b, t - 1, D_FF), F32)
    p1 = jnp.concatenate([state[:, 1:2], zero], axis=1).reshape(n, D_FF)
    p2 = jnp.concatenate([state, zero[:, 1:]], axis=1).reshape(n, D_FF)
    body = functools.partial(_ffn_rows_body, t, has_final)
    full = lambda shape: pl.BlockSpec(shape, lambda i: (0,) * len(shape))
    y, hg = pl.pallas_call(
        body,
        out_shape=(jax.ShapeDtypeStruct((n, d), F32), jax.ShapeDtypeStruct((n, D_FF), F32)),
        grid=(1,),
        in_specs=[full((n, d)), full((1, d)), _resident(wg.shape), _resident(wu.shape), _resident(wd.shape),
                  full((3, D_FF)), full((1, D_FF)), full((n, D_FF)), full((n, D_FF))] + extra_spec1,
        out_specs=(full((n, d)), full((n, D_FF))),
        compiler_params=_params(("arbitrary",)),
        name="convffn_rows",
    )(x3.reshape(n, d), g, wg, wu, wd, cw, cb, p1, p2, *extra_in)
    return y.reshape(b, t, d), hg.reshape(b, t, D_FF)[:, t - 2:, :]


def _block_diag(w):
    out = jnp.zeros((BRANCH_W, BRANCH_W), F32)
    for h in range(N_HEADS):
        out = out.at[h * HEAD_W:(h + 1) * HEAD_W, h * HEAD_W:(h + 1) * HEAD_W].set(w[h])
    return out


def _head_sum_matrix3():
    i = np.arange(BRANCH_W) // HEAD_W
    e = (i[:, None] == i[None, :]).astype(np.float32)
    return jnp.asarray(np.concatenate([e, e, e], axis=0), BF16)


def _rope_tables(pos, rows):
    half = HEAD_W // 2
    inv = ROPE_THETA ** (-jnp.arange(half, dtype=F32) / half)
    ang = pos.astype(F32)[:, None] * inv[None, :]
    cos = jnp.cos(ang)
    sin = jnp.sin(ang)
    cos_t = jnp.tile(jnp.concatenate([cos, cos], axis=1), (1, N_HEADS))
    sin_t = jnp.tile(jnp.concatenate([-sin, sin], axis=1), (1, N_HEADS))
    pad = rows - pos.shape[0]
    if pad:
        cos_t = jnp.pad(cos_t, ((0, pad), (0, 0)))
        sin_t = jnp.pad(sin_t, ((0, pad), (0, 0)))
    return cos_t, sin_t


def _layer_params(l, W):
    w_in = W["w_in"][l]
    o = 0
    parts = {}
    for name, width in (("ax", 256), ("ag", 256), ("cb", 1024), ("cq", 512), ("cf", 512), ("ci", 256),
                        ("cg", 256), ("dq", 256), ("dk", 256), ("dv", 256), ("gt", 4096)):
        parts[name] = w_in[:, o:o + width]
        o += width
    order = ("cb", "ax", "ag", "cq", "cf", "ci", "cg", "dq", "dk", "dv")
    p = {
        "w_mix": jnp.concatenate([parts[n] for n in order], axis=1).astype(BF16),
        "w_gt": parts["gt"].astype(BF16),
        "norm1_g": W["norm1_g"][l][None, :],
        "a_conv_w": W["a_conv_w"][l], "a_conv_b": W["a_conv_b"][l][None, :],
        "a_gx_bd": _split_weight(_block_diag(W["a_gx_w"][l])), "a_gx_b": W["a_gx_b"][l][None, :],
        "a_ga_bd": _split_weight(_block_diag(W["a_ga_w"][l])), "a_ga_b": W["a_ga_b"][l][None, :],
        "a_lambda": W["a_lambda"][l][None, :],
        "b_mu": W["b_mu"][l][None, :], "b_w0": W["b_w0"][l][None, :], "b_a0": W["b_a0"][l][None, :],
        "b_w2p": _split_weight(jnp.concatenate([W["b_w2"][l], jnp.zeros((64, BRANCH_W), F32)], axis=0)),
        "b_a2p": _split_weight(jnp.concatenate([jnp.zeros((64, BRANCH_W), F32), W["b_a2"][l]], axis=0)),
        "b_g2": _split_weight(W["b_g2"][l]),
        "b_k_k": W["b_k_k"][l][None, :], "b_k_a": W["b_k_a"][l][None, :],
        "b_r_k": W["b_r_k"][l].reshape(1, BRANCH_W),
        "b_ln_w": W["b_ln_w"][l][None, :], "b_ln_b": W["b_ln_b"][l][None, :],
        "c_norm_g": jnp.tile(W["c_norm_g"][l], N_HEADS)[None, :],
        "w_branch": W["w_branch"][l].astype(BF16), "w_out": W["w_out"][l].astype(BF16),
        "norm2_g": W["norm2_g"][l][None, :],
        "ffn_w_gate": W["ffn_w_gate"][l].astype(BF16), "ffn_w_up": W["ffn_w_up"][l].astype(BF16),
        "ffn_w_down": W["ffn_w_down"][l].astype(BF16),
        "ffn_conv_w": W["ffn_conv_w"][l], "ffn_conv_b": W["ffn_conv_b"][l][None, :],
        "e3": _head_sum_matrix3(),
    }
    return p


def _trunk(x, pos0, st, W, params, lb_all):
    b, t, d = x.shape
    depth = W["w_in"].shape[0]
    has_cache = st["k"] is not None
    if has_cache:
        cos, sin = _rope_tables(pos0 + jnp.arange(t), 8)
        k_cache_t = jnp.transpose(st["k"], (0, 1, 3, 4, 2))
        v_cache_t = jnp.transpose(st["v"], (0, 1, 3, 4, 2))
    else:
        cos, sin = _rope_tables(pos0 + jnp.arange(t), t)
    outs = {n: [] for n in ("ha", "ca", "wkv", "sh", "sc", "k", "v", "cf")}
    for l in range(depth):
        p = params[l]
        cols = _inproj(x.reshape(b * t, d), p["norm1_g"], p["w_mix"]).reshape(b, t, N_MIX_COLS)
        y_a, h_new, ca_new = _rglru(cols, st["ha"][l][:, None, :], st["ca"][l], p)
        y_b, sh_new, wkv_new = _rwkv(cols, st["sh"][l][:, None, :], st["wkv"][l], p)
        y_c, sc_new = _hgrn(cols, st["sc"][l], lb_all[l][None, :], p["c_norm_g"], p["e3"])
        if has_cache:
            y_d, k_rows = _attn_sample(cols, k_cache_t, v_cache_t, l, cos, sin)
            k_out = k_rows.reshape(b, t, N_HEADS, HEAD_W)
            v_out = cols[:, :, N_MIX_COLS - BRANCH_W:].reshape(b, t, N_HEADS, HEAD_W)
        else:
            y_d, k_t, v_t = _attn_prompt(cols, cos, sin)
            keep = min(D_WIN_MAX, t)
            k_out = jnp.transpose(k_t.reshape(b, N_HEADS, HEAD_W, t)[..., t - keep:], (0, 3, 1, 2))
            v_out = jnp.transpose(v_t.reshape(b, N_HEADS, HEAD_W, t)[..., t - keep:], (0, 3, 1, 2))
        ys = [y.reshape(b * t, BRANCH_W) for y in (y_a, y_b, y_c, y_d)]
        x1 = _merge(x.reshape(b * t, d), p["norm1_g"], p["w_gt"], ys, p["w_branch"], p["w_out"])
        final_g = W["final_norm_g"][None, :] if l == depth - 1 else None
        x, cf_new = _ffn(x1.reshape(b, t, d), p["norm2_g"], p["ffn_w_gate"], p["ffn_w_up"], p["ffn_w_down"],
                         p["ffn_conv_w"], p["ffn_conv_b"], st["cf"][l], final_g)
        outs["ha"].append(h_new[:, 0, :])
        outs["ca"].append(ca_new)
        outs["wkv"].append(wkv_new)
        outs["sh"].append(sh_new[:, 0, :])
        outs["sc"].append(sc_new)
        outs["k"].append(k_out)
        outs["v"].append(v_out)
        outs["cf"].append(cf_new)
    return (x,) + tuple(jnp.stack(outs[n]) for n in ("ha", "ca", "wkv", "sh", "sc", "k", "v", "cf"))


def kernel(x_prompt, x_sample, state_a_h, state_a_conv, state_b_wkv, state_b_shift, state_c_s, cache_d_k, cache_d_v, state_ffn_conv, norm1_g, w_in, a_conv_w, a_conv_b, a_gx_w, a_gx_b, a_ga_w, a_ga_b, a_lambda, b_mu, b_w0, b_w2, b_a0, b_a2, b_g2, b_k_k, b_k_a, b_r_k, b_ln_w, b_ln_b, c_lb, c_norm_g, w_branch, w_out, norm2_g, ffn_w_gate, ffn_w_up, ffn_conv_w, ffn_conv_b, ffn_w_down, final_norm_g):
    W = {"norm1_g": norm1_g, "w_in": w_in, "a_conv_w": a_conv_w, "a_conv_b": a_conv_b, "a_gx_w": a_gx_w,
         "a_gx_b": a_gx_b, "a_ga_w": a_ga_w, "a_ga_b": a_ga_b, "a_lambda": a_lambda, "b_mu": b_mu,
         "b_w0": b_w0, "b_w2": b_w2, "b_a0": b_a0, "b_a2": b_a2, "b_g2": b_g2, "b_k_k": b_k_k,
         "b_k_a": b_k_a, "b_r_k": b_r_k, "b_ln_w": b_ln_w, "b_ln_b": b_ln_b, "c_lb": c_lb,
         "c_norm_g": c_norm_g, "w_branch": w_branch, "w_out": w_out, "norm2_g": norm2_g,
         "ffn_w_gate": ffn_w_gate, "ffn_w_up": ffn_w_up, "ffn_conv_w": ffn_conv_w,
         "ffn_conv_b": ffn_conv_b, "ffn_w_down": ffn_w_down, "final_norm_g": final_norm_g}
    depth = w_in.shape[0]
    params = [_layer_params(l, W) for l in range(depth)]
    lb_sm = jax.nn.softmax(c_lb.astype(F32), axis=0)
    lb_all = jnp.cumsum(lb_sm, axis=0) - lb_sm[0]
    b_p, t_p = x_prompt.shape[:2]

    def zeros(*s):
        return jnp.zeros((depth, b_p) + s, F32)

    st_p = {"ha": zeros(BRANCH_W), "ca": zeros(3, BRANCH_W), "wkv": zeros(N_HEADS, HEAD_W, HEAD_W),
            "sh": zeros(B_COLS), "sc": zeros(N_HEADS, C_EXPAND, HEAD_W), "k": None, "v": None,
            "cf": zeros(2, D_FF)}
    out_p = _trunk(x_prompt, 0, st_p, W, params, lb_all)
    st_s = {"ha": state_a_h, "ca": state_a_conv, "wkv": state_b_wkv, "sh": state_b_shift, "sc": state_c_s,
            "k": cache_d_k, "v": cache_d_v, "cf": state_ffn_conv}
    out_s = _trunk(x_sample, PAST_LEN, st_s, W, params, lb_all)
    return (out_p[0], out_s[0]) + out_p[1:] + out_s[1:]
```

```python
import functools

import numpy as np

import jax
import jax.numpy as jnp
from jax import lax
from jax.experimental import pallas as pl
from jax.experimental.pallas import tpu as pltpu

F32 = jnp.float32
BF16 = jnp.bfloat16
HI = lax.Precision.HIGHEST

D_MODEL = 1024
BRANCH_W = 256
N_HEADS = 4
HEAD_W = 64
HEAD_SHIFT = 6
C_EXPAND = 128
C_EXPAND_SHIFT = 7
C_FDIM = 512
B_COLS = 1024
D_FF = 3072
NORM_EPS = 1e-6
A_C = 8.0
B_GN_EPS = 64e-5
ROPE_THETA = 10000.0
PAST_LEN = 8192
D_WIN_MAX = 2048
NEG_BIG = -1e30
N_MIX_COLS = 3840
ATT_BLK = 128

VMEM_LIMIT = 56 * 1024 * 1024


def _dot(a, b, prec=None):
    return jnp.dot(a, b, preferred_element_type=F32, precision=prec)


def _dot_nt(a, b, prec=None):
    return lax.dot_general(a, b, (((1,), (1,)), ((), ())), preferred_element_type=F32, precision=prec)


def _dot_tn(a, b, prec=None):
    return lax.dot_general(a, b, (((0,), (0,)), ((), ())), preferred_element_type=F32, precision=prec)


def _rms_rows(x, g):
    ms = jnp.mean(x * x, axis=-1, keepdims=True)
    return x * lax.rsqrt(ms + NORM_EPS) * g


def _round_up(n, m):
    return (n + m - 1) // m * m


def _params(sem):
    return pltpu.CompilerParams(dimension_semantics=sem, vmem_limit_bytes=VMEM_LIMIT)


def _resident(shape):
    nd = len(shape)
    return pl.BlockSpec(shape, lambda *_: (0,) * nd, pipeline_mode=pl.Buffered(1))


def _split3(x):
    hi = x.astype(BF16)
    r1 = x - hi.astype(F32)
    mid = r1.astype(BF16)
    lo = (r1 - mid.astype(F32)).astype(BF16)
    return hi, mid, lo


def _head_sum(x, e3):
    return _dot(jnp.concatenate(_split3(x), axis=1), e3)


def _dot_split(x, w3):
    hi = x.astype(BF16)
    lo = (x - hi.astype(F32)).astype(BF16)
    return _dot(jnp.concatenate([hi, lo, hi], axis=1), w3)


def _split_weight(w):
    hi = w.astype(BF16)
    lo = (w - hi.astype(F32)).astype(BF16)
    return jnp.concatenate([hi, hi, lo], axis=1)


def _stack_heads(x, lane_head):
    return jnp.concatenate([jnp.where(lane_head == h, x, jnp.zeros_like(x)) for h in range(N_HEADS)], axis=0)


def _layer_block(shape, layer):
    nd = len(shape)
    return pl.BlockSpec((1,) + tuple(shape[1:]), lambda *_: (layer,) + (0,) * (nd - 1),
                        pipeline_mode=pl.Buffered(1))


def _inproj_body(x_ref, g_ref, w_ref, o_ref):
    o_ref[...] = _dot(_rms_rows(x_ref[...], g_ref[0]).astype(BF16), w_ref[0])


def _inproj(x2d, P, layer):
    n, d = x2d.shape
    tm = min(n, 1024)
    return pl.pallas_call(
        _inproj_body,
        out_shape=jax.ShapeDtypeStruct((n, N_MIX_COLS), F32),
        grid=(n // tm,),
        in_specs=[pl.BlockSpec((tm, d), lambda i: (i, 0)),
                  _layer_block(P["norm1_g"].shape, layer),
                  pl.BlockSpec((1, d, N_MIX_COLS), lambda i: (layer, 0, 0), pipeline_mode=pl.Buffered(1))],
        out_specs=pl.BlockSpec((tm, N_MIX_COLS), lambda i: (i, 0)),
        compiler_params=_params(("parallel",)),
        name="inproj",
    )(x2d, P["norm1_g"], P["w_in"])


def _rglru_body(t_len, tp, axg_ref, h0_ref, cbuf_ref, cw_ref, cb_ref, wgx_ref, bgx_ref, wga_ref, bga_ref,
                lam_ref, y_ref, hl_ref, nb_ref, xs_scr, a_scr, b_scr):
    if tp != t_len:
        xs_scr[...] = jnp.zeros(xs_scr.shape, F32)
    xs_scr[pl.ds(5, 3), :] = jnp.concatenate([cbuf_ref[0, 0], jnp.zeros((3, BRANCH_W), F32)], axis=1)
    xs_scr[pl.ds(8, t_len), :] = axg_ref[0]
    x0 = xs_scr[pl.ds(8, tp), 0:BRANCH_W]
    x1 = xs_scr[pl.ds(7, tp), 0:BRANCH_W]
    x2 = xs_scr[pl.ds(6, tp), 0:BRANCH_W]
    x3 = xs_scr[pl.ds(5, tp), 0:BRANCH_W]
    cw = cw_ref[0]
    xc = cb_ref[0] + cw[3:4] * x0 + cw[2:3] * x1 + cw[1:2] * x2 + cw[0:1] * x3
    gate_x = jax.nn.sigmoid(_dot_split(xc, wgx_ref[0]) + bgx_ref[0])
    gate_a = jax.nn.sigmoid(_dot_split(xc, wga_ref[0]) + bga_ref[0])
    log_a = -A_C * gate_a * jax.nn.softplus(-lam_ref[0])
    a = jnp.exp(log_a)
    th = jnp.tanh(log_a)
    b_in = jnp.sqrt(-2.0 * th / (1.0 - th)) * (gate_x * xc)
    if tp != t_len:
        row = lax.broadcasted_iota(jnp.int32, (tp, 1), 0)
        a = jnp.where(row < t_len, a, 1.0)
        b_in = jnp.where(row < t_len, b_in, 0.0)
    a_scr[...] = a
    b_scr[...] = b_in

    row8 = lax.broadcasted_iota(jnp.int32, (8, 1), 0)

    def group(g, carry):
        r0 = pl.multiple_of(g * 8, 8)
        ag = a_scr[pl.ds(r0, 8), :]
        bg = b_scr[pl.ds(r0, 8), :]
        for s in (1, 2, 4):
            a_sh = jnp.where(row8 >= s, pltpu.roll(ag, s, 0), 1.0)
            b_sh = jnp.where(row8 >= s, pltpu.roll(bg, s, 0), 0.0)
            bg = ag * b_sh + bg
            ag = ag * a_sh
        h = ag * carry + bg
        b_scr[pl.ds(r0, 8), :] = h
        return h[7:8, :]

    h_last = lax.fori_loop(0, tp // 8, group, h0_ref[0, 0])
    h = b_scr[...]
    gate = xs_scr[pl.ds(8, tp), BRANCH_W:2 * BRANCH_W]
    y = h * jax.nn.gelu(gate)
    y_ref[0] = y[0:t_len]
    hl_ref[0] = h_last
    nb_ref[0] = xs_scr[pl.ds(8 + t_len - 3, 3), 0:BRANCH_W]


def _rglru(cols3, h0, cbuf, P, layer):
    b, t, _ = cols3.shape
    tp = _round_up(t, 8)
    body = functools.partial(_rglru_body, t, tp)
    names = ("a_conv_w", "a_conv_b", "a_gx_bd", "a_gx_b", "a_ga_bd", "a_ga_b", "a_lambda")
    return pl.pallas_call(
        body,
        out_shape=(jax.ShapeDtypeStruct((b, t, BRANCH_W), F32),
                   jax.ShapeDtypeStruct((b, 1, BRANCH_W), F32),
                   jax.ShapeDtypeStruct((b, 3, BRANCH_W), F32)),
        grid=(b,),
        in_specs=[pl.BlockSpec((1, t, 2 * BRANCH_W), lambda i: (i, 0, 0)),
                  pl.BlockSpec((1, 1, 1, BRANCH_W), lambda i: (layer, i, 0, 0)),
                  pl.BlockSpec((1, 1, 3, BRANCH_W), lambda i: (layer, i, 0, 0))]
                 + [_layer_block(P[n].shape, layer) for n in names],
        out_specs=(pl.BlockSpec((1, t, BRANCH_W), lambda i: (i, 0, 0)),
                   pl.BlockSpec((1, 1, BRANCH_W), lambda i: (i, 0, 0)),
                   pl.BlockSpec((1, 3, BRANCH_W), lambda i: (i, 0, 0))),
        scratch_shapes=[pltpu.VMEM((tp + 8, 2 * BRANCH_W), F32),
                        pltpu.VMEM((tp, BRANCH_W), F32),
                        pltpu.VMEM((tp, BRANCH_W), F32)],
        compiler_params=_params(("parallel",)),
        name="rglru",
    )(cols3, h0, cbuf, *[P[n] for n in names])


RWKV_GROUP = 4


def _rwkv_consts(tc, ch):
    t = np.arange(tc)
    same = (t[:, None] // ch) == (t[None, :] // ch)
    tril = same & (t[None, :] <= t[:, None])
    return jnp.asarray(np.concatenate([tril, same], axis=0).astype(np.float32), BF16)


def _rwkv_body(t_len, tc, n_t, ch, cb1_ref, cb2_ref, shift_ref, s0_ref, mu_ref, w0_ref, w2_ref, a0_ref, a2_ref,
               g2_ref, kk_ref, ka_ref, rk_ref, lnw_ref, lnb_ref, cs_ref, e3_ref, y_ref, last_ref, sn_ref,
               xs_scr, s_scr, at_scr, bt_scr, kt_scr, bh_scr, kh_scr, v_scr, rt_scr, gc_scr, y_scr, bon_scr,
               g_scr):
    ti = pl.program_id(1)
    t_blk = min(t_len, tc)
    cs = N_HEADS * ch
    half = B_COLS // 2

    @pl.when(ti == 0)
    def _():
        if t_blk != tc:
            xs_scr[...] = jnp.zeros(xs_scr.shape, F32)
        xs_scr[pl.ds(7, 1), :] = shift_ref[0, 0]
        s_scr[...] = jnp.zeros(s_scr.shape, F32)
        for h in range(N_HEADS):
            s_scr[h * HEAD_W:(h + 1) * HEAD_W, h * HEAD_W:(h + 1) * HEAD_W] = s0_ref[0, 0, h]

    xs_scr[pl.ds(8, t_blk), 0:half] = cb1_ref[0]
    xs_scr[pl.ds(8, t_blk), half:B_COLS] = cb2_ref[0]
    cb = xs_scr[pl.ds(8, tc), :]
    shifted = xs_scr[pl.ds(7, tc), :]
    cm = cb + (shifted - cb) * mu_ref[0]
    r = cm[:, 0:256]
    k = cm[:, 256:512]
    v = cm[:, 512:768]
    lora = cm[:, 768:896]
    w = -jax.nn.softplus(-(w0_ref[0] + _dot_split(jnp.tanh(lora), w2_ref[0]))) - 0.5
    log_w = -jnp.exp(w)
    a = jax.nn.sigmoid(a0_ref[0] + _dot_split(lora, a2_ref[0]))
    g_scr[...] = _dot_split(jax.nn.sigmoid(cm[:, 896:1024]), g2_ref[0])
    e3 = e3_ref[...]
    kk = k * kk_ref[0]
    kk = kk / jnp.maximum(jnp.sqrt(_head_sum(kk * kk, e3)), 1e-12)
    k2 = k * (1.0 + (a - 1.0) * ka_ref[0])
    bon_scr[...] = _head_sum(r * k2 * rk_ref[0], e3) * v
    a_s = -kk
    b_s = kk * a
    if t_blk != tc:
        row = lax.broadcasted_iota(jnp.int32, (tc, 1), 0)
        live = row < t_blk
        log_w = jnp.where(live, log_w, 0.0)
        a_s = jnp.where(live, a_s, 0.0)
        b_s = jnp.where(live, b_s, 0.0)
        k2 = jnp.where(live, k2, 0.0)
    parts = jnp.concatenate(_split3(log_w), axis=1)
    sub = cs_ref.shape[1]
    cum, tot = [], []
    for j in range(tc // sub):
        cc = _dot(cs_ref[...], parts[j * sub:(j + 1) * sub, :])
        cc = cc[:, 0:256] + cc[:, 256:512] + cc[:, 512:768]
        cum.append(cc[0:sub])
        tot.append(cc[sub:2 * sub])
    cum = jnp.concatenate(cum, axis=0)
    tot = jnp.concatenate(tot, axis=0)
    inv = jnp.exp(-cum)
    tail = jnp.exp(tot - cum)
    at_scr[...] = (a_s * jnp.exp(cum - log_w)).astype(BF16)
    bt_scr[...] = (b_s * inv).astype(BF16)
    kt_scr[...] = (k2 * inv).astype(BF16)
    bh_scr[...] = (b_s * tail).astype(BF16)
    kh_scr[...] = (k2 * tail).astype(BF16)
    v_scr[...] = v.astype(BF16)
    rt_scr[...] = r * jnp.exp(cum)
    gc_scr[...] = jnp.exp(tot)

    row_t = lax.broadcasted_iota(jnp.int32, (cs, cs), 0) & (ch - 1)
    col_t = lax.broadcasted_iota(jnp.int32, (cs, cs), 1) & (ch - 1)
    strict = col_t < row_t
    row_t2 = lax.broadcasted_iota(jnp.int32, (cs, 2 * cs), 0) & (ch - 1)
    col_t2 = lax.broadcasted_iota(jnp.int32, (cs, 2 * cs), 1) & (ch - 1)
    incl2 = col_t2 <= row_t2
    eye = (lax.broadcasted_iota(jnp.int32, (cs, cs), 0) == lax.broadcasted_iota(jnp.int32, (cs, cs), 1)).astype(F32)
    lane_head = lax.broadcasted_iota(jnp.int32, (ch, BRANCH_W), 1) >> HEAD_SHIFT
    n_dbl = ch.bit_length() - 1

    def prepare(c):
        r0 = c * ch if isinstance(c, int) else pl.multiple_of(c * ch, ch)
        abd, bbd, kbd, bhd, khd, vbd = (_stack_heads(s[pl.ds(r0, ch), :], lane_head)
                                        for s in (at_scr, bt_scr, kt_scr, bh_scr, kh_scr, v_scr))
        rbd = _stack_heads(rt_scr[pl.ds(r0, ch), :], lane_head)
        s1 = _dot_nt(jnp.concatenate([abd, rbd.astype(BF16)], axis=0), jnp.concatenate([bbd, kbd], axis=0))
        l_ab = jnp.where(strict, s1[0:cs, 0:cs], 0.0)
        l_ak = jnp.where(strict, s1[0:cs, cs:2 * cs], 0.0)
        m_r = jnp.where(incl2, s1[cs:2 * cs, :], 0.0).astype(BF16)
        tinv = eye + l_ab
        lb16 = l_ab.astype(BF16)
        yield
        pw = _dot(lb16, lb16)
        for i in range(1, n_dbl):
            pw16 = pw.astype(BF16)
            yield
            if i < n_dbl - 1:
                both = _dot(jnp.concatenate([tinv.astype(BF16), pw16], axis=0), pw16)
                tinv = tinv + both[0:cs]
                pw = both[cs:2 * cs]
            else:
                tinv = tinv + _dot(tinv.astype(BF16), pw16)
        yield
        w1 = _dot(l_ak.astype(BF16), vbd)
        yield
        x = _dot(tinv.astype(BF16), jnp.concatenate([abd, w1.astype(BF16)], axis=1))
        yield
        z = _dot(m_r[:, 0:cs], x.astype(BF16))
        r_hat = rbd + z[:, 0:BRANCH_W]
        yield
        y_hat = z[:, BRANCH_W:2 * BRANCH_W] + _dot(m_r[:, cs:2 * cs], vbd)
        ar = jnp.concatenate([x[:, 0:BRANCH_W], r_hat], axis=0).astype(BF16)
        g_end = gc_scr[pl.ds(r0, ch), :][0:1, :]
        return ar, x[:, BRANCH_W:2 * BRANCH_W], y_hat, vbd, jnp.concatenate([bhd, khd], axis=0), g_end

    def advance(c0, prepared):
        for n, (ar, u_hat, y_hat, vbd, bk, g_end) in enumerate(prepared):
            c = c0 + n
            r0 = c * ch if isinstance(c, int) else pl.multiple_of(c * ch, ch)
            sb = s_scr[...]
            uy = _dot_nt(ar, sb.astype(BF16))
            u = uy[0:cs] + u_hat
            ys = uy[cs:2 * cs] + y_hat
            y_scr[pl.ds(r0, ch), :] = ys[0:ch] + ys[ch:2 * ch] + ys[2 * ch:3 * ch] + ys[3 * ch:4 * ch]
            yield
            s_scr[...] = sb * g_end + _dot_tn(jnp.concatenate([u.astype(BF16), vbd], axis=0), bk)
            yield

    def interleave(gens):
        done = [None] * len(gens)
        live = list(range(len(gens)))
        while live:
            for i in list(live):
                try:
                    next(gens[i])
                except StopIteration as stop:
                    done[i] = stop.value
                    live.remove(i)
        return done

    n_chunk = tc // ch
    grp = RWKV_GROUP if n_chunk % RWKV_GROUP == 0 else 1
    first = tuple(interleave([prepare(c) for c in range(grp)]))
    if n_chunk > grp:
        def group(i, carry):
            c0 = i * grp
            res = interleave([prepare(c0 + grp + n) for n in range(grp)] + [advance(c0, carry)])
            return tuple(res[0:grp])

        first = lax.fori_loop(0, n_chunk // grp - 1, group, first)
    interleave([advance(n_chunk - grp, first)])

    y = y_scr[...]
    mean = _head_sum(y, e3) * (1.0 / HEAD_W)
    dev = y - mean
    var = _head_sum(dev * dev, e3) * (1.0 / HEAD_W)
    yn = dev * lax.rsqrt(var + B_GN_EPS) * lnw_ref[0] + lnb_ref[0]
    out = (yn + bon_scr[...]) * g_scr[...]
    y_ref[0] = out[0:t_blk]
    xs_scr[pl.ds(7, 1), :] = xs_scr[pl.ds(8 + t_blk - 1, 1), :]

    @pl.when(ti == n_t - 1)
    def _():
        last_ref[0] = xs_scr[pl.ds(7, 1), :]
        for h in range(N_HEADS):
            sn_ref[0, h] = s_scr[h * HEAD_W:(h + 1) * HEAD_W, h * HEAD_W:(h + 1) * HEAD_W]


def _rwkv(cols3, shift, s0, P, layer):
    b, t, _ = cols3.shape
    if t >= 64:
        ch, tc = 64, min(t, 1024)
        assert t % tc == 0
    else:
        ch = 16
        tc = _round_up(t, ch)
    n_t = max(1, t // tc)
    t_blk = min(t, tc)
    body = functools.partial(_rwkv_body, t, tc, n_t, ch)
    consts = _rwkv_consts(min(tc, 256), ch)
    names = ("b_mu", "b_w0", "b_w2p", "b_a0", "b_a2p", "b_g2", "b_k_k", "b_k_a", "b_r_k", "b_ln_w", "b_ln_b")
    half = B_COLS // 2
    sc16 = lambda: pltpu.VMEM((tc, BRANCH_W), BF16)
    sc32 = lambda: pltpu.VMEM((tc, BRANCH_W), F32)
    return pl.pallas_call(
        body,
        out_shape=(jax.ShapeDtypeStruct((b, t, BRANCH_W), F32),
                   jax.ShapeDtypeStruct((b, 1, B_COLS), F32),
                   jax.ShapeDtypeStruct((b, N_HEADS, HEAD_W, HEAD_W), F32)),
        grid=(b, n_t),
        in_specs=[pl.BlockSpec((1, t_blk, half), lambda i, j: (i, j, 1)),
                  pl.BlockSpec((1, t_blk, half), lambda i, j: (i, j, 2)),
                  pl.BlockSpec((1, 1, 1, B_COLS), lambda i, j: (layer, i, 0, 0)),
                  pl.BlockSpec((1, 1, N_HEADS, HEAD_W, HEAD_W), lambda i, j: (layer, i, 0, 0, 0))]
                 + [_layer_block(P[n].shape, layer) for n in names]
                 + [_resident(consts.shape), _resident(P["e3"].shape)],
        out_specs=(pl.BlockSpec((1, t_blk, BRANCH_W), lambda i, j: (i, j, 0)),
                   pl.BlockSpec((1, 1, B_COLS), lambda i, j: (i, 0, 0)),
                   pl.BlockSpec((1, N_HEADS, HEAD_W, HEAD_W), lambda i, j: (i, 0, 0, 0))),
        scratch_shapes=[pltpu.VMEM((tc + 8, B_COLS), F32),
                        pltpu.VMEM((BRANCH_W, BRANCH_W), F32),
                        sc16(), sc16(), sc16(), sc16(), sc16(), sc16(),
                        sc32(), sc32(), sc32(), sc32(), sc32()],
        compiler_params=_params(("parallel", "arbitrary")),
        name="rwkv7",
    )(cols3, cols3, shift, s0, *[P[n] for n in names], consts, P["e3"])


def _hgrn_consts(tc):
    nl = tc.bit_length() - 1
    t = np.arange(tc)
    tril = (t[None, :] <= t[:, None]).astype(np.float32)
    x = t[:, None] ^ t[None, :]
    lvl = np.floor(np.log2(np.maximum(x, 1))).astype(np.int32)
    lvl = np.where(t[:, None] > t[None, :], lvl, np.where(t[:, None] == t[None, :], nl, -1))
    return jnp.asarray(tril, BF16), jnp.asarray(lvl, jnp.int32), nl


def _hgrn_body(t_len, tc, n_t, nl, cq_ref, cf_ref, cig_ref, s0_ref, lb_ref, ng_ref, sel_ref, lvl_ref, e3_ref,
               y_ref, sn_ref, in_scr, st_scr):
    ti = pl.program_id(1)
    t_blk = min(t_len, tc)

    @pl.when(ti == 0)
    def _():
        st_scr[...] = jnp.zeros(st_scr.shape, F32)
        for h in range(N_HEADS):
            st_scr[h * HEAD_W:(h + 1) * HEAD_W, h * C_EXPAND:(h + 1) * C_EXPAND] = s0_ref[0, 0, h]

    if t_blk != tc:
        in_scr[...] = jnp.zeros(in_scr.shape, F32)
        in_scr[0, pl.ds(0, t_blk), :] = cq_ref[0]
        in_scr[1, pl.ds(0, t_blk), :] = cf_ref[0]
        in_scr[2, pl.ds(0, t_blk), :] = cig_ref[0]
        q, f, cig = in_scr[0], in_scr[1], in_scr[2]
    else:
        q, f, cig = cq_ref[0], cf_ref[0], cig_ref[0]
    lb = lb_ref[0]
    fg = lb + (1.0 - lb) * jax.nn.sigmoid(f)
    log_f = jnp.log(fg)
    kk = 1.0 - fg
    fg_live = fg
    if t_blk != tc:
        row = lax.broadcasted_iota(jnp.int32, (tc, 1), 0)
        live = row < t_blk
        log_f = jnp.where(live, log_f, 0.0)
        kk = jnp.where(live, kk, 0.0)
        fg_live = jnp.where(live, fg, 1.0)
    hi = log_f.astype(BF16)
    lo = (log_f - hi.astype(F32)).astype(BF16)
    parts = jnp.concatenate([hi, lo], axis=1)

    d = _dot(sel_ref[...], parts)
    cum = d[:, 0:C_FDIM] + d[:, C_FDIM:2 * C_FDIM]
    odd = (lax.broadcasted_iota(jnp.int32, (tc, 1), 0) & 1) == 1

    lvl = lvl_ref[...]
    att = [None] * N_HEADS
    for l in range(nl + 1):
        if l == 0:
            e = jnp.where(odd, fg_live, 1.0)
        elif l < nl:
            m = 1 << l
            ref_rows = cum.reshape(tc // (2 * m), 2 * m, C_FDIM)[:, m - 1:m, :]
            ref_rows = jnp.broadcast_to(ref_rows, (tc // (2 * m), 2 * m, C_FDIM)).reshape(tc, C_FDIM)
            e = jnp.exp(-jnp.abs(cum - ref_rows))
        if l < nl:
            qs = (q * e).astype(BF16)
            ks = (kk * e).astype(BF16)
        else:
            qs = q.astype(BF16)
            ks = kk.astype(BF16)
        hit = lvl == l
        for h in range(N_HEADS):
            kl = slice(h * C_EXPAND, (h + 1) * C_EXPAND)
            a = _dot_nt(qs[:, kl], ks[:, kl])
            att[h] = jnp.where(hit, a, 0.0 if att[h] is None else att[h])
    c_last = cum[tc - 1:tc, :]
    lane_head = lax.broadcasted_iota(jnp.int32, (tc, BRANCH_W), 1) >> HEAD_SHIFT
    v = cig[:, 0:BRANCH_W]
    v_stack = _stack_heads(v, lane_head).astype(BF16)
    p_cat = jnp.concatenate([a.astype(BF16) for a in att], axis=1)
    st = st_scr[...]
    o = _dot(p_cat, v_stack) + _dot_nt((q * jnp.exp(cum)).astype(BF16), st.astype(BF16))
    ms = _head_sum(o * o, e3_ref[...]) * (1.0 / HEAD_W)
    y = o * lax.rsqrt(ms + NORM_EPS) * ng_ref[0] * jax.nn.silu(cig[:, BRANCH_W:2 * BRANCH_W])
    y_ref[0] = y[0:t_blk]
    upd = _dot_tn(v.astype(BF16), (kk * jnp.exp(c_last - cum)).astype(BF16))
    rh = lax.broadcasted_iota(jnp.int32, (BRANCH_W, C_FDIM), 0) >> HEAD_SHIFT
    ch_ = lax.broadcasted_iota(jnp.int32, (BRANCH_W, C_FDIM), 1) >> C_EXPAND_SHIFT
    st_scr[...] = st * jnp.exp(c_last) + jnp.where(rh == ch_, upd, 0.0)

    @pl.when(ti == n_t - 1)
    def _():
        for h in range(N_HEADS):
            sn_ref[0, h] = st_scr[h * HEAD_W:(h + 1) * HEAD_W, h * C_EXPAND:(h + 1) * C_EXPAND]


def _hgrn(cols3, s0_t, P, layer):
    b, t, _ = cols3.shape
    e3 = P["e3"]
    tc = 256 if t >= 256 else 16
    assert t % tc == 0 or t < tc
    n_t = max(1, t // tc)
    t_blk = min(t, tc)
    sel, lvl, nl = _hgrn_consts(tc)
    body = functools.partial(_hgrn_body, t, tc, n_t, nl)
    blk = lambda idx: pl.BlockSpec((1, t_blk, C_FDIM), lambda i, j: (i, j, idx))
    pad_shape = (3, tc, C_FDIM) if t_blk != tc else (1, 8, 128)
    return pl.pallas_call(
        body,
        out_shape=(jax.ShapeDtypeStruct((b, t, BRANCH_W), F32),
                   jax.ShapeDtypeStruct((b, N_HEADS, HEAD_W, C_EXPAND), F32)),
        grid=(b, n_t),
        in_specs=[blk(3), blk(4), blk(5),
                  pl.BlockSpec((1, 1, N_HEADS, HEAD_W, C_EXPAND), lambda i, j: (layer, i, 0, 0, 0)),
                  _layer_block(P["c_lb"].shape, layer),
                  _layer_block(P["c_norm_g"].shape, layer),
                  _resident(sel.shape), _resident(lvl.shape), _resident(e3.shape)],
        out_specs=(pl.BlockSpec((1, t_blk, BRANCH_W), lambda i, j: (i, j, 0)),
                   pl.BlockSpec((1, N_HEADS, HEAD_W, C_EXPAND), lambda i, j: (i, 0, 0, 0))),
        scratch_shapes=[pltpu.VMEM(pad_shape, F32),
                        pltpu.VMEM((BRANCH_W, C_FDIM), F32)],
        compiler_params=_params(("parallel", "arbitrary")),
        name="hgrn2",
    )(cols3, cols3, cols3, s0_t, P["c_lb"], P["c_norm_g"], sel, lvl, e3)


def _rope_rows(x, cos, sin_signed):
    lane = lax.broadcasted_iota(jnp.int32, x.shape, 1)
    swapped = jnp.where((lane & (HEAD_W - 1)) < HEAD_W // 2,
                        pltpu.roll(x, BRANCH_W - HEAD_W // 2, 1), pltpu.roll(x, HEAD_W // 2, 1))
    return x * cos + swapped * sin_signed


def _pattern_weight(d):
    ok = d >= 0
    w = ((d <= 128).astype(F32) + (((d & 3) == 0) & (d <= 512)).astype(F32)
         + (((d & 15) == 0) & (d <= 2048)).astype(F32))
    return jnp.where(ok, w, 0.0)


def _att_group(n_blk):
    return next(g for g in (4, 2, 1) if n_blk % g == 0)


def _att_weight_table(n_blk):
    grp = _att_group(n_blk)
    r = np.arange(ATT_BLK)
    delta = np.arange(-(grp - 1), n_blk)
    d = delta[:, None, None] * ATT_BLK + r[None, None, :] - r[None, :, None]
    w = ((d <= 128).astype(np.float32) + ((d % 4 == 0) & (d <= 512)) + ((d % 16 == 0) & (d <= 2048)))
    return jnp.asarray(np.where(d >= 0, w, 0.0), F32)


def _attn_prompt_body(n_blk, qkv_ref, cos_ref, sin_ref, wt_ref, y_ref, kt_ref, vt_ref, qt_scr, kb_scr, vt_scr,
                      s_scr):
    qkv = qkv_ref[0]
    cos = cos_ref[...]
    sin = sin_ref[...]
    q = _rope_rows(qkv[:, 0:BRANCH_W], cos, sin) * (HEAD_W ** -0.5)
    k = _rope_rows(qkv[:, BRANCH_W:2 * BRANCH_W], cos, sin)
    v_t = qkv[:, 2 * BRANCH_W:3 * BRANCH_W].T
    kb_scr[...] = k.astype(BF16)
    qt_scr[...] = q.T.astype(BF16)
    vt_scr[...] = v_t.astype(BF16)
    kt_ref[0] = k.T
    vt_ref[0] = v_t
    n_lane = N_HEADS * ATT_BLK
    row_head = lax.broadcasted_iota(jnp.int32, (BRANCH_W, n_lane), 0) >> HEAD_SHIFT
    col_head = lax.broadcasted_iota(jnp.int32, (BRANCH_W, n_lane), 1) >> 7
    own_head = row_head == col_head
    grp = _att_group(n_blk)

    def q_block(i, _):
        q0 = pl.multiple_of(i * ATT_BLK, ATT_BLK)
        qt = qt_scr[:, pl.ds(q0, ATT_BLK)]
        q_bd = jnp.where(own_head, jnp.concatenate([qt] * N_HEADS, axis=1), jnp.zeros((), BF16))

        def weights(j):
            w1 = wt_ref[i - j + (grp - 1)]
            return jnp.concatenate([w1] * N_HEADS, axis=1)

        n_grp = (i + grp) // grp

        def scores(gi):
            g_max = None
            for n in range(grp):
                j = gi * grp + n
                s = _dot(kb_scr[pl.ds(pl.multiple_of(j * ATT_BLK, ATT_BLK), ATT_BLK), :], q_bd)
                s = jnp.where(weights(j) > 0.0, s, NEG_BIG)
                s_scr[gi & 1, n] = s
                g_max = s if g_max is None else jnp.maximum(g_max, s)
            return jnp.max(g_max, axis=0, keepdims=True)

        def update(gi, g_max, m_run, l_run, acc):
            m_new = jnp.maximum(m_run, g_max)
            alpha = jnp.exp(m_run - m_new)
            l_new = alpha * l_run
            pv = [None] * N_HEADS
            for n in range(grp):
                j = gi * grp + n
                p = weights(j) * jnp.exp(s_scr[gi & 1, n] - m_new)
                l_new = l_new + jnp.sum(p, axis=0, keepdims=True)
                pb = p.astype(BF16)
                for h in range(N_HEADS):
                    vt = vt_scr[h * HEAD_W:(h + 1) * HEAD_W, pl.ds(pl.multiple_of(j * ATT_BLK, ATT_BLK), ATT_BLK)]
                    term = _dot(vt, pb[:, h * ATT_BLK:(h + 1) * ATT_BLK])
                    pv[h] = term if pv[h] is None else pv[h] + term
            heads = [alpha[:, h * ATT_BLK:(h + 1) * ATT_BLK] * acc[h * HEAD_W:(h + 1) * HEAD_W, :] + pv[h]
                     for h in range(N_HEADS)]
            return m_new, l_new, jnp.concatenate(heads, axis=0)

        def step(gi, carry):
            g_max, m_run, l_run, acc = carry
            nxt = scores(gi + 1)
            return (nxt,) + update(gi, g_max, m_run, l_run, acc)

        init = (scores(0), jnp.full((1, n_lane), NEG_BIG, F32), jnp.zeros((1, n_lane), F32),
                jnp.zeros((BRANCH_W, ATT_BLK), F32))
        g_max, m_run, l_run, acc = lax.fori_loop(0, n_grp - 1, step, init)
        _, l_run, acc = update(n_grp - 1, g_max, m_run, l_run, acc)
        out_t = jnp.concatenate([acc[h * HEAD_W:(h + 1) * HEAD_W, :] / l_run[:, h * ATT_BLK:(h + 1) * ATT_BLK]
                                 for h in range(N_HEADS)], axis=0)
        y_ref[0, pl.ds(q0, ATT_BLK), :] = out_t.T
        return 0

    lax.fori_loop(0, n_blk, q_block, 0)


def _attn_prompt(cols3, cos, sin):
    b, t, _ = cols3.shape
    n_blk = t // ATT_BLK
    assert n_blk * ATT_BLK == t
    body = functools.partial(_attn_prompt_body, n_blk)
    wtab = _att_weight_table(n_blk)
    return pl.pallas_call(
        body,
        out_shape=(jax.ShapeDtypeStruct((b, t, BRANCH_W), F32), jax.ShapeDtypeStruct((b, BRANCH_W, t), F32),
                   jax.ShapeDtypeStruct((b, BRANCH_W, t), F32)),
        grid=(b,),
        in_specs=[pl.BlockSpec((1, t, 3 * BRANCH_W), lambda i: (i, 0, 4)),
                  _resident((t, BRANCH_W)), _resident((t, BRANCH_W)), _resident(wtab.shape)],
        out_specs=(pl.BlockSpec((1, t, BRANCH_W), lambda i: (i, 0, 0)),
                   pl.BlockSpec((1, BRANCH_W, t), lambda i: (i, 0, 0)),
                   pl.BlockSpec((1, BRANCH_W, t), lambda i: (i, 0, 0))),
        scratch_shapes=[pltpu.VMEM((BRANCH_W, t), BF16),
                        pltpu.VMEM((t, BRANCH_W), BF16),
                        pltpu.VMEM((BRANCH_W, t), BF16),
                        pltpu.VMEM((2, _att_group(n_blk), ATT_BLK, N_HEADS * ATT_BLK), F32)],
        compiler_params=_params(("parallel",)),
        name="attn_prompt",
    )(cols3, cos, sin, wtab)


def _attn_sample_body(t_len, l_cache, qkv_ref, kc_ref, vc_ref, cos_ref, sin_ref, y_ref, k_ref, pad_scr):
    tp = 8
    pad_scr[...] = jnp.zeros(pad_scr.shape, F32)
    pad_scr[pl.ds(0, t_len), :] = qkv_ref[0]
    qkv = pad_scr[...]
    cos = cos_ref[...]
    sin = sin_ref[...]
    q = _rope_rows(qkv[:, 0:BRANCH_W], cos, sin) * (HEAD_W ** -0.5)
    k_new = _rope_rows(qkv[:, BRANCH_W:2 * BRANCH_W], cos, sin)
    v_new = qkv[:, 2 * BRANCH_W:3 * BRANCH_W]
    k_ref[0] = k_new[0:t_len]
    lane_head = lax.broadcasted_iota(jnp.int32, (tp, BRANCH_W), 1) >> HEAD_SHIFT
    q_bd = jnp.concatenate([jnp.where(lane_head == h, q, 0.0) for h in range(N_HEADS)], axis=0).astype(BF16)
    n_row = N_HEADS * tp
    k_t = kc_ref[0, 0].reshape(BRANCH_W, l_cache)
    v_t = vc_ref[0, 0].reshape(BRANCH_W, l_cache)
    s_c = _dot(q_bd, k_t.astype(BF16))
    s_n = _dot_nt(q_bd, k_new.astype(BF16))
    t_row = lax.broadcasted_iota(jnp.int32, (n_row, 1), 0) & (tp - 1)
    col_c = lax.broadcasted_iota(jnp.int32, (n_row, l_cache), 1)
    w_c = _pattern_weight(l_cache + t_row - col_c)
    col_n = lax.broadcasted_iota(jnp.int32, (n_row, tp), 1)
    w_n = jnp.where(col_n < t_len, _pattern_weight(t_row - col_n), 0.0)
    s_c = jnp.where(w_c > 0.0, s_c, NEG_BIG)
    s_n = jnp.where(w_n > 0.0, s_n, NEG_BIG)
    m = jnp.maximum(jnp.max(s_c, axis=-1, keepdims=True), jnp.max(s_n, axis=-1, keepdims=True))
    p_c = w_c * jnp.exp(s_c - m)
    p_n = w_n * jnp.exp(s_n - m)
    l = jnp.sum(p_c, axis=-1, keepdims=True) + jnp.sum(p_n, axis=-1, keepdims=True)
    num = _dot_nt(p_c.astype(BF16), v_t.astype(BF16)) + _dot(p_n.astype(BF16), v_new.astype(BF16))
    ratio = num / l
    out = jnp.zeros((tp, BRANCH_W), F32)
    for h in range(N_HEADS):
        out = jnp.where(lane_head == h, ratio[h * tp:(h + 1) * tp, :], out)
    y_ref[0] = out[0:t_len]


def _attn_sample(cols3, k_cache_t, v_cache_t, layer, cos, sin):
    b, t, _ = cols3.shape
    l_cache = k_cache_t.shape[-1]
    assert t <= 8
    body = functools.partial(_attn_sample_body, t, l_cache)
    cache_spec = pl.BlockSpec((1, 1, N_HEADS, HEAD_W, l_cache), lambda i: (layer, i, 0, 0, 0))
    return pl.pallas_call(
        body,
        out_shape=(jax.ShapeDtypeStruct((b, t, BRANCH_W), F32), jax.ShapeDtypeStruct((b, t, BRANCH_W), F32)),
        grid=(b,),
        in_specs=[pl.BlockSpec((1, t, 3 * BRANCH_W), lambda i: (i, 0, 4)),
                  cache_spec, cache_spec,
                  pl.BlockSpec((8, BRANCH_W), lambda i: (0, 0)),
                  pl.BlockSpec((8, BRANCH_W), lambda i: (0, 0))],
        out_specs=(pl.BlockSpec((1, t, BRANCH_W), lambda i: (i, 0, 0)),
                   pl.BlockSpec((1, t, BRANCH_W), lambda i: (i, 0, 0))),
        scratch_shapes=[pltpu.VMEM((8, 3 * BRANCH_W), F32)],
        compiler_params=_params(("parallel",)),
        name="attn_sample",
    )(cols3, k_cache_t, v_cache_t, cos, sin)


def _merge_body(x_ref, g_ref, win_ref, ya_ref, yb_ref, yc_ref, yd_ref, wb_ref, wo_ref, o_ref):
    x = x_ref[...]
    u = _rms_rows(x, g_ref[0]).astype(BF16)
    merged = None
    for n, y_ref in enumerate((ya_ref, yb_ref, yc_ref, yd_ref)):
        z = _dot(y_ref[...].astype(BF16), wb_ref[0, n])
        gate_cols = slice(N_MIX_COLS + n * D_MODEL, N_MIX_COLS + (n + 1) * D_MODEL)
        gate = jax.nn.sigmoid(_dot(u, win_ref[0, :, gate_cols]))
        merged = gate * z if merged is None else merged + gate * z
    o_ref[...] = x + _dot(merged.astype(BF16), wo_ref[0])


def _merge(x2d, ys, P, layer):
    n, d = x2d.shape
    tm = min(n, 512)
    row = lambda w: pl.BlockSpec((tm, w), lambda i: (i, 0))
    return pl.pallas_call(
        _merge_body,
        out_shape=jax.ShapeDtypeStruct((n, d), F32),
        grid=(n // tm,),
        in_specs=[row(d), _layer_block(P["norm1_g"].shape, layer), _layer_block(P["w_in"].shape, layer),
                  row(BRANCH_W), row(BRANCH_W), row(BRANCH_W), row(BRANCH_W),
                  _layer_block(P["w_branch"].shape, layer), _layer_block(P["w_out"].shape, layer)],
        out_specs=row(d),
        compiler_params=_params(("parallel",)),
        name="merge",
    )(x2d, P["norm1_g"], P["w_in"], *ys, P["w_branch"], P["w_out"])


FFN_COL = 1024


def _ffn_columns(x, v, prev1, prev2, first, second, wg_ref, wu_ref, wd_ref, cw_ref, cb_ref, keep_gate):
    acc = jnp.zeros(x.shape, F32)
    cw = cw_ref[0]
    for c in range(D_FF // FFN_COL):
        sl = slice(c * FFN_COL, (c + 1) * FFN_COL)
        hg = _dot(v, wg_ref[0, :, sl])
        hu = _dot(v, wu_ref[0, :, sl])
        h1 = jnp.where(first, prev1(sl), pltpu.roll(hg, 1, 0))
        h2 = jnp.where(first | second, prev2(sl), pltpu.roll(hg, 2, 0))
        conv = cb_ref[0, :, sl] + cw[2:3, sl] * hg + cw[1:2, sl] * h1 + cw[0:1, sl] * h2
        hmid = (jax.nn.gelu(conv) * hu).astype(BF16)
        acc = acc + _dot(hmid, wd_ref[0, sl, :])
        keep_gate(sl, hg)
    return x + acc


def _ffn_body(tm, n_t, has_final, x_ref, g_ref, wg_ref, wu_ref, wd_ref, cw_ref, cb_ref, st_ref, *rest):
    if has_final:
        fg_ref, o_ref, ns_ref, carry_scr = rest
    else:
        o_ref, ns_ref, carry_scr = rest
    ti = pl.program_id(1)

    @pl.when(ti == 0)
    def _():
        carry_scr[...] = st_ref[0, 0]

    x = x_ref[0]
    v = _rms_rows(x, g_ref[0]).astype(BF16)
    row = lax.broadcasted_iota(jnp.int32, (tm, 1), 0)

    def prev2(sl):
        return jnp.where(row == 0, carry_scr[0:1, sl], carry_scr[1:2, sl])

    def keep_gate(sl, hg):
        carry_scr[:, sl] = hg[tm - 2:tm, :]

    y = _ffn_columns(x, v, lambda sl: carry_scr[1:2, sl], prev2, row == 0, row == 1,
                     wg_ref, wu_ref, wd_ref, cw_ref, cb_ref, keep_gate)
    o_ref[0] = _rms_rows(y, fg_ref[...]) if has_final else y

    @pl.when(ti == n_t - 1)
    def _():
        ns_ref[0] = carry_scr[...]


def _ffn_rows_body(seq, has_final, x_ref, g_ref, wg_ref, wu_ref, wd_ref, cw_ref, cb_ref, p1_ref, p2_ref, *rest):
    if has_final:
        fg_ref, o_ref, hg_ref = rest
    else:
        o_ref, hg_ref = rest
    x = x_ref[...]
    v = _rms_rows(x, g_ref[0]).astype(BF16)
    step = lax.broadcasted_iota(jnp.int32, (x.shape[0], 1), 0) & (seq - 1)

    def keep_gate(sl, hg):
        hg_ref[:, sl] = hg

    y = _ffn_columns(x, v, lambda sl: p1_ref[:, sl], lambda sl: p2_ref[:, sl], step == 0, step == 1,
                     wg_ref, wu_ref, wd_ref, cw_ref, cb_ref, keep_gate)
    o_ref[...] = _rms_rows(y, fg_ref[...]) if has_final else y


def _ffn(x3, state, P, layer, final_g):
    b, t, d = x3.shape
    has_final = final_g is not None
    extra_in = [final_g] if has_final else []
    extra_spec = [_resident((1, d))] if has_final else []
    names = ("norm2_g", "ffn_w_gate", "ffn_w_up", "ffn_w_down", "ffn_conv_w", "ffn_conv_b")
    weights = [P[n] for n in names]
    weight_specs = [_layer_block(w.shape, layer) for w in weights]
    if t >= 8:
        tm = min(t, 512)
        n_t = t // tm
        assert tm * n_t == t and tm % 8 == 0
        body = functools.partial(_ffn_body, tm, n_t, has_final)
        return pl.pallas_call(
            body,
            out_shape=(jax.ShapeDtypeStruct((b, t, d), F32), jax.ShapeDtypeStruct((b, 2, D_FF), F32)),
            grid=(b, n_t),
            in_specs=[pl.BlockSpec((1, tm, d), lambda i, j: (i, j, 0))] + weight_specs
                     + [pl.BlockSpec((1, 1, 2, D_FF), lambda i, j: (layer, i, 0, 0))] + extra_spec,
            out_specs=(pl.BlockSpec((1, tm, d), lambda i, j: (i, j, 0)),
                       pl.BlockSpec((1, 2, D_FF), lambda i, j: (i, 0, 0))),
            scratch_shapes=[pltpu.VMEM((2, D_FF), F32)],
            compiler_params=_params(("parallel", "arbitrary")),
            name="convffn",
        )(x3, *weights, state, *extra_in)
    assert t >= 2 and t & (t - 1) == 0 and (b * t) % 8 == 0
    n = b * t
    st = state[layer]
    zero = jnp.zeros((b, t - 1, D_FF), F32)
    p1 = jnp.concatenate([st[:, 1:2], zero], axis=1).reshape(n, D_FF)
    p2 = jnp.concatenate([st, zero[:, 1:]], axis=1).reshape(n, D_FF)
    body = functools.partial(_ffn_rows_body, t, has_final)
    full = lambda shape: pl.BlockSpec(shape, lambda i: (0,) * len(shape))
    y, hg = pl.pallas_call(
        body,
        out_shape=(jax.ShapeDtypeStruct((n, d), F32), jax.ShapeDtypeStruct((n, D_FF), F32)),
        grid=(1,),
        in_specs=[full((n, d))] + weight_specs + [full((n, D_FF)), full((n, D_FF))] + extra_spec,
        out_specs=(full((n, d)), full((n, D_FF))),
        compiler_params=_params(("arbitrary",)),
        name="convffn_rows",
    )(x3.reshape(n, d), *weights, p1, p2, *extra_in)
    return y.reshape(b, t, d), hg.reshape(b, t, D_FF)[:, t - 2:, :]


def _block_diag(w):
    eye = jnp.asarray(np.eye(N_HEADS, dtype=np.float32))
    return jnp.einsum("lhij,hg->lhigj", w, eye).reshape(w.shape[0], BRANCH_W, BRANCH_W)


def _head_sum_matrix3():
    i = np.arange(BRANCH_W) // HEAD_W
    e = (i[:, None] == i[None, :]).astype(np.float32)
    return jnp.asarray(np.concatenate([e, e, e], axis=0), BF16)


def _rope_tables(pos, rows):
    half = HEAD_W // 2
    inv = ROPE_THETA ** (-jnp.arange(half, dtype=F32) / half)
    ang = pos.astype(F32)[:, None] * inv[None, :]
    cos = jnp.cos(ang)
    sin = jnp.sin(ang)
    cos_t = jnp.tile(jnp.concatenate([cos, cos], axis=1), (1, N_HEADS))
    sin_t = jnp.tile(jnp.concatenate([-sin, sin], axis=1), (1, N_HEADS))
    pad = rows - pos.shape[0]
    if pad:
        cos_t = jnp.pad(cos_t, ((0, pad), (0, 0)))
        sin_t = jnp.pad(sin_t, ((0, pad), (0, 0)))
    return cos_t, sin_t


def _all_layer_params(W):
    row = lambda a: a.reshape(a.shape[0], 1, -1)
    depth = W["w_in"].shape[0]
    zpad = jnp.zeros((depth, 64, BRANCH_W), F32)
    lb_sm = jax.nn.softmax(W["c_lb"].astype(F32), axis=0)
    P = {
        "w_in": W["w_in"].astype(BF16),
        "a_gx_bd": _split_weight(_block_diag(W["a_gx_w"])),
        "a_ga_bd": _split_weight(_block_diag(W["a_ga_w"])),
        "b_w2p": _split_weight(jnp.concatenate([W["b_w2"], zpad], axis=1)),
        "b_a2p": _split_weight(jnp.concatenate([zpad, W["b_a2"]], axis=1)),
        "b_g2": _split_weight(W["b_g2"]),
        "c_lb": row(jnp.cumsum(lb_sm, axis=0) - lb_sm[0]),
        "c_norm_g": row(jnp.tile(W["c_norm_g"], (1, N_HEADS))),
        "w_branch": W["w_branch"].astype(BF16), "w_out": W["w_out"].astype(BF16),
        "ffn_w_gate": W["ffn_w_gate"].astype(BF16), "ffn_w_up": W["ffn_w_up"].astype(BF16),
        "ffn_w_down": W["ffn_w_down"].astype(BF16),
        "a_conv_w": W["a_conv_w"], "ffn_conv_w": W["ffn_conv_w"],
        "e3": _head_sum_matrix3(),
    }
    for n in ("norm1_g", "norm2_g", "a_conv_b", "a_gx_b", "a_ga_b", "a_lambda", "b_mu", "b_w0", "b_a0", "b_k_k",
              "b_k_a", "b_r_k", "b_ln_w", "b_ln_b", "ffn_conv_b"):
        P[n] = row(W[n])
    return P


def _trunk(x, pos0, st, W, P):
    b, t, d = x.shape
    depth = W["w_in"].shape[0]
    has_cache = st["k"] is not None
    if has_cache:
        cos, sin = _rope_tables(pos0 + jnp.arange(t), 8)
        k_cache_t = jnp.transpose(st["k"], (0, 1, 3, 4, 2))
        v_cache_t = jnp.transpose(st["v"], (0, 1, 3, 4, 2))
    else:
        cos, sin = _rope_tables(pos0 + jnp.arange(t), t)
    ha = st["ha"][:, :, None, :]
    sh = st["sh"][:, :, None, :]
    sc_t = jnp.transpose(st["sc"], (0, 1, 2, 4, 3))
    outs = {n: [] for n in ("ha", "ca", "wkv", "sh", "sc", "k", "v", "cf")}
    for l in range(depth):
        cols = _inproj(x.reshape(b * t, d), P, l).reshape(b, t, N_MIX_COLS)
        y_a, h_new, ca_new = _rglru(cols, ha, st["ca"], P, l)
        y_b, sh_new, wkv_new = _rwkv(cols, sh, st["wkv"], P, l)
        y_c, sc_new = _hgrn(cols, sc_t, P, l)
        if has_cache:
            y_d, k_rows = _attn_sample(cols, k_cache_t, v_cache_t, l, cos, sin)
            k_out = k_rows.reshape(b, t, N_HEADS, HEAD_W)
            v_out = cols[:, :, N_MIX_COLS - BRANCH_W:].reshape(b, t, N_HEADS, HEAD_W)
        else:
            y_d, k_t, v_t = _attn_prompt(cols, cos, sin)
            keep = min(D_WIN_MAX, t)
            k_out = jnp.transpose(k_t.reshape(b, N_HEADS, HEAD_W, t)[..., t - keep:], (0, 3, 1, 2))
            v_out = jnp.transpose(v_t.reshape(b, N_HEADS, HEAD_W, t)[..., t - keep:], (0, 3, 1, 2))
        ys = [y.reshape(b * t, BRANCH_W) for y in (y_a, y_b, y_c, y_d)]
        x1 = _merge(x.reshape(b * t, d), ys, P, l)
        final_g = W["final_norm_g"][None, :] if l == depth - 1 else None
        x, cf_new = _ffn(x1.reshape(b, t, d), st["cf"], P, l, final_g)
        outs["ha"].append(h_new[:, 0, :])
        outs["ca"].append(ca_new)
        outs["wkv"].append(wkv_new)
        outs["sh"].append(sh_new[:, 0, :])
        outs["sc"].append(sc_new)
        outs["k"].append(k_out)
        outs["v"].append(v_out)
        outs["cf"].append(cf_new)
    res = {n: jnp.stack(outs[n]) for n in outs}
    res["sc"] = jnp.transpose(res["sc"], (0, 1, 2, 4, 3))
    return (x,) + tuple(res[n] for n in ("ha", "ca", "wkv", "sh", "sc", "k", "v", "cf"))


def kernel(x_prompt, x_sample, state_a_h, state_a_conv, state_b_wkv, state_b_shift, state_c_s, cache_d_k, cache_d_v, state_ffn_conv, norm1_g, w_in, a_conv_w, a_conv_b, a_gx_w, a_gx_b, a_ga_w, a_ga_b, a_lambda, b_mu, b_w0, b_w2, b_a0, b_a2, b_g2, b_k_k, b_k_a, b_r_k, b_ln_w, b_ln_b, c_lb, c_norm_g, w_branch, w_out, norm2_g, ffn_w_gate, ffn_w_up, ffn_conv_w, ffn_conv_b, ffn_w_down, final_norm_g):
    W = {"norm1_g": norm1_g, "w_in": w_in, "a_conv_w": a_conv_w, "a_conv_b": a_conv_b, "a_gx_w": a_gx_w,
         "a_gx_b": a_gx_b, "a_ga_w": a_ga_w, "a_ga_b": a_ga_b, "a_lambda": a_lambda, "b_mu": b_mu,
         "b_w0": b_w0, "b_w2": b_w2, "b_a0": b_a0, "b_a2": b_a2, "b_g2": b_g2, "b_k_k": b_k_k,
         "b_k_a": b_k_a, "b_r_k": b_r_k, "b_ln_w": b_ln_w, "b_ln_b": b_ln_b, "c_lb": c_lb,
         "c_norm_g": c_norm_g, "w_branch": w_branch, "w_out": w_out, "norm2_g": norm2_g,
         "ffn_w_gate": ffn_w_gate, "ffn_w_up": ffn_w_up, "ffn_conv_w": ffn_conv_w,
         "ffn_conv_b": ffn_conv_b, "ffn_w_down": ffn_w_down, "final_norm_g": final_norm_g}
    depth = w_in.shape[0]
    P = _all_layer_params(W)
    b_p, t_p = x_prompt.shape[:2]

    def zeros(*s):
        return jnp.zeros((depth, b_p) + s, F32)

    st_p = {"ha": zeros(BRANCH_W), "ca": zeros(3, BRANCH_W), "wkv": zeros(N_HEADS, HEAD_W, HEAD_W),
            "sh": zeros(B_COLS), "sc": zeros(N_HEADS, C_EXPAND, HEAD_W), "k": None, "v": None,
            "cf": zeros(2, D_FF)}
    out_p = _trunk(x_prompt, 0, st_p, W, P)
    st_s = {"ha": state_a_h, "ca": state_a_conv, "wkv": state_b_wkv, "sh": state_b_shift, "sc": state_c_s,
            "k": cache_d_k, "v": cache_d_v, "cf": state_ffn_conv}
    out_s = _trunk(x_sample, PAST_LEN, st_s, W, P)
    return (out_p[0], out_s[0]) + out_p[1:] + out_s[1:]
```

```python
import functools

import numpy as np

import jax
import jax.numpy as jnp
from jax import lax
from jax.experimental import pallas as pl
from jax.experimental.pallas import tpu as pltpu

F32 = jnp.float32
BF16 = jnp.bfloat16
HI = lax.Precision.HIGHEST

D_MODEL = 1024
BRANCH_W = 256
N_HEADS = 4
HEAD_W = 64
HEAD_SHIFT = 6
C_EXPAND = 128
C_EXPAND_SHIFT = 7
C_FDIM = 512
B_COLS = 1024
D_FF = 3072
NORM_EPS = 1e-6
A_C = 8.0
B_GN_EPS = 64e-5
ROPE_THETA = 10000.0
PAST_LEN = 8192
D_WIN_MAX = 2048
NEG_BIG = -1e30
N_MIX_COLS = 3840
ATT_BLK = 128

VMEM_LIMIT = 56 * 1024 * 1024


def _dot(a, b, prec=None):
    return jnp.dot(a, b, preferred_element_type=F32, precision=prec)


def _dot_nt(a, b, prec=None):
    return lax.dot_general(a, b, (((1,), (1,)), ((), ())), preferred_element_type=F32, precision=prec)


def _dot_tn(a, b, prec=None):
    return lax.dot_general(a, b, (((0,), (0,)), ((), ())), preferred_element_type=F32, precision=prec)


def _rms_rows(x, g):
    ms = jnp.mean(x * x, axis=-1, keepdims=True)
    return x * lax.rsqrt(ms + NORM_EPS) * g


def _round_up(n, m):
    return (n + m - 1) // m * m


def _params(sem):
    return pltpu.CompilerParams(dimension_semantics=sem, vmem_limit_bytes=VMEM_LIMIT)


def _resident(shape):
    nd = len(shape)
    return pl.BlockSpec(shape, lambda *_: (0,) * nd, pipeline_mode=pl.Buffered(1))


def _split3(x):
    hi = x.astype(BF16)
    r1 = x - hi.astype(F32)
    mid = r1.astype(BF16)
    lo = (r1 - mid.astype(F32)).astype(BF16)
    return hi, mid, lo


def _head_sum(x, e3):
    return _dot(jnp.concatenate(_split3(x), axis=1), e3)


def _dot_split(x, w3):
    hi = x.astype(BF16)
    lo = (x - hi.astype(F32)).astype(BF16)
    return _dot(jnp.concatenate([hi, lo, hi], axis=1), w3)


def _split_weight(w):
    hi = w.astype(BF16)
    lo = (w - hi.astype(F32)).astype(BF16)
    return jnp.concatenate([hi, hi, lo], axis=1)


def _stack_heads(x, lane_head):
    return jnp.concatenate([jnp.where(lane_head == h, x, jnp.zeros_like(x)) for h in range(N_HEADS)], axis=0)


def _layer_block(shape, layer):
    nd = len(shape)
    return pl.BlockSpec((1,) + tuple(shape[1:]), lambda *_: (layer,) + (0,) * (nd - 1),
                        pipeline_mode=pl.Buffered(1))


def _inproj_body(x_ref, g_ref, w_ref, o_ref):
    o_ref[...] = _dot(_rms_rows(x_ref[...], g_ref[0]).astype(BF16), w_ref[0])


def _inproj(x2d, P, layer):
    n, d = x2d.shape
    tm = min(n, 1024)
    return pl.pallas_call(
        _inproj_body,
        out_shape=jax.ShapeDtypeStruct((n, N_MIX_COLS), F32),
        grid=(n // tm,),
        in_specs=[pl.BlockSpec((tm, d), lambda i: (i, 0)),
                  _layer_block(P["norm1_g"].shape, layer),
                  pl.BlockSpec((1, d, N_MIX_COLS), lambda i: (layer, 0, 0), pipeline_mode=pl.Buffered(1))],
        out_specs=pl.BlockSpec((tm, N_MIX_COLS), lambda i: (i, 0)),
        compiler_params=_params(("parallel",)),
        name="inproj",
    )(x2d, P["norm1_g"], P["w_in"])


def _rglru_body(t_len, tp, axg_ref, h0_ref, cbuf_ref, cw_ref, cb_ref, wgx_ref, bgx_ref, wga_ref, bga_ref,
                lam_ref, y_ref, hl_ref, nb_ref, xs_scr, a_scr, b_scr):
    if tp != t_len:
        xs_scr[...] = jnp.zeros(xs_scr.shape, F32)
    xs_scr[pl.ds(5, 3), :] = jnp.concatenate([cbuf_ref[0, 0], jnp.zeros((3, BRANCH_W), F32)], axis=1)
    xs_scr[pl.ds(8, t_len), :] = axg_ref[0]
    x0 = xs_scr[pl.ds(8, tp), 0:BRANCH_W]
    x1 = xs_scr[pl.ds(7, tp), 0:BRANCH_W]
    x2 = xs_scr[pl.ds(6, tp), 0:BRANCH_W]
    x3 = xs_scr[pl.ds(5, tp), 0:BRANCH_W]
    cw = cw_ref[0]
    xc = cb_ref[0] + cw[3:4] * x0 + cw[2:3] * x1 + cw[1:2] * x2 + cw[0:1] * x3
    gate_x = jax.nn.sigmoid(_dot_split(xc, wgx_ref[0]) + bgx_ref[0])
    gate_a = jax.nn.sigmoid(_dot_split(xc, wga_ref[0]) + bga_ref[0])
    log_a = -A_C * gate_a * jax.nn.softplus(-lam_ref[0])
    a = jnp.exp(log_a)
    th = jnp.tanh(log_a)
    b_in = jnp.sqrt(-2.0 * th / (1.0 - th)) * (gate_x * xc)
    if tp != t_len:
        row = lax.broadcasted_iota(jnp.int32, (tp, 1), 0)
        a = jnp.where(row < t_len, a, 1.0)
        b_in = jnp.where(row < t_len, b_in, 0.0)
    a_scr[...] = a
    b_scr[...] = b_in

    row8 = lax.broadcasted_iota(jnp.int32, (8, 1), 0)

    def group(g, carry):
        r0 = pl.multiple_of(g * 8, 8)
        ag = a_scr[pl.ds(r0, 8), :]
        bg = b_scr[pl.ds(r0, 8), :]
        for s in (1, 2, 4):
            a_sh = jnp.where(row8 >= s, pltpu.roll(ag, s, 0), 1.0)
            b_sh = jnp.where(row8 >= s, pltpu.roll(bg, s, 0), 0.0)
            bg = ag * b_sh + bg
            ag = ag * a_sh
        h = ag * carry + bg
        b_scr[pl.ds(r0, 8), :] = h
        return h[7:8, :]

    h_last = lax.fori_loop(0, tp // 8, group, h0_ref[0, 0])
    h = b_scr[...]
    gate = xs_scr[pl.ds(8, tp), BRANCH_W:2 * BRANCH_W]
    y = h * jax.nn.gelu(gate)
    y_ref[0] = y[0:t_len]
    hl_ref[0] = h_last
    nb_ref[0] = xs_scr[pl.ds(8 + t_len - 3, 3), 0:BRANCH_W]


def _rglru(cols3, h0, cbuf, P, layer):
    b, t, _ = cols3.shape
    tp = _round_up(t, 8)
    body = functools.partial(_rglru_body, t, tp)
    names = ("a_conv_w", "a_conv_b", "a_gx_bd", "a_gx_b", "a_ga_bd", "a_ga_b", "a_lambda")
    return pl.pallas_call(
        body,
        out_shape=(jax.ShapeDtypeStruct((b, t, BRANCH_W), F32),
                   jax.ShapeDtypeStruct((b, 1, BRANCH_W), F32),
                   jax.ShapeDtypeStruct((b, 3, BRANCH_W), F32)),
        grid=(b,),
        in_specs=[pl.BlockSpec((1, t, 2 * BRANCH_W), lambda i: (i, 0, 0)),
                  pl.BlockSpec((1, 1, 1, BRANCH_W), lambda i: (layer, i, 0, 0)),
                  pl.BlockSpec((1, 1, 3, BRANCH_W), lambda i: (layer, i, 0, 0))]
                 + [_layer_block(P[n].shape, layer) for n in names],
        out_specs=(pl.BlockSpec((1, t, BRANCH_W), lambda i: (i, 0, 0)),
                   pl.BlockSpec((1, 1, BRANCH_W), lambda i: (i, 0, 0)),
                   pl.BlockSpec((1, 3, BRANCH_W), lambda i: (i, 0, 0))),
        scratch_shapes=[pltpu.VMEM((tp + 8, 2 * BRANCH_W), F32),
                        pltpu.VMEM((tp, BRANCH_W), F32),
                        pltpu.VMEM((tp, BRANCH_W), F32)],
        compiler_params=_params(("parallel",)),
        name="rglru",
    )(cols3, h0, cbuf, *[P[n] for n in names])


RWKV_GROUP = 4


def _rwkv_consts(tc, ch):
    t = np.arange(tc)
    same = (t[:, None] // ch) == (t[None, :] // ch)
    tril = same & (t[None, :] <= t[:, None])
    return jnp.asarray(np.concatenate([tril, same], axis=0).astype(np.float32), BF16)


def _rwkv_body(t_len, tc, n_t, ch, cb1_ref, cb2_ref, shift_ref, s0_ref, mu_ref, w0_ref, w2_ref, a0_ref, a2_ref,
               g2_ref, kk_ref, ka_ref, rk_ref, lnw_ref, lnb_ref, cs_ref, e3_ref, y_ref, last_ref, sn_ref,
               xs_scr, s_scr, at_scr, bt_scr, kt_scr, bh_scr, kh_scr, v_scr, rt_scr, gc_scr, y_scr, bon_scr,
               g_scr):
    ti = pl.program_id(1)
    t_blk = min(t_len, tc)
    cs = N_HEADS * ch
    half = B_COLS // 2

    @pl.when(ti == 0)
    def _():
        if t_blk != tc:
            xs_scr[...] = jnp.zeros(xs_scr.shape, F32)
        xs_scr[pl.ds(7, 1), :] = shift_ref[0, 0]
        s_scr[...] = jnp.zeros(s_scr.shape, F32)
        for h in range(N_HEADS):
            s_scr[h * HEAD_W:(h + 1) * HEAD_W, h * HEAD_W:(h + 1) * HEAD_W] = s0_ref[0, 0, h]

    xs_scr[pl.ds(8, t_blk), 0:half] = cb1_ref[0]
    xs_scr[pl.ds(8, t_blk), half:B_COLS] = cb2_ref[0]
    cb = xs_scr[pl.ds(8, tc), :]
    shifted = xs_scr[pl.ds(7, tc), :]
    cm = cb + (shifted - cb) * mu_ref[0]
    r = cm[:, 0:256]
    k = cm[:, 256:512]
    v = cm[:, 512:768]
    lora = cm[:, 768:896]
    w = -jax.nn.softplus(-(w0_ref[0] + _dot_split(jnp.tanh(lora), w2_ref[0]))) - 0.5
    log_w = -jnp.exp(w)
    a = jax.nn.sigmoid(a0_ref[0] + _dot_split(lora, a2_ref[0]))
    g_scr[...] = _dot_split(jax.nn.sigmoid(cm[:, 896:1024]), g2_ref[0])
    e3 = e3_ref[...]
    kk = k * kk_ref[0]
    kk = kk / jnp.maximum(jnp.sqrt(_head_sum(kk * kk, e3)), 1e-12)
    k2 = k * (1.0 + (a - 1.0) * ka_ref[0])
    bon_scr[...] = _head_sum(r * k2 * rk_ref[0], e3) * v
    a_s = -kk
    b_s = kk * a
    if t_blk != tc:
        row = lax.broadcasted_iota(jnp.int32, (tc, 1), 0)
        live = row < t_blk
        log_w = jnp.where(live, log_w, 0.0)
        a_s = jnp.where(live, a_s, 0.0)
        b_s = jnp.where(live, b_s, 0.0)
        k2 = jnp.where(live, k2, 0.0)
    parts = jnp.concatenate(_split3(log_w), axis=1)
    sub = cs_ref.shape[1]
    cum, tot = [], []
    for j in range(tc // sub):
        cc = _dot(cs_ref[...], parts[j * sub:(j + 1) * sub, :])
        cc = cc[:, 0:256] + cc[:, 256:512] + cc[:, 512:768]
        cum.append(cc[0:sub])
        tot.append(cc[sub:2 * sub])
    cum = jnp.concatenate(cum, axis=0)
    tot = jnp.concatenate(tot, axis=0)
    inv = jnp.exp(-cum)
    tail = jnp.exp(tot - cum)
    at_scr[...] = (a_s * jnp.exp(cum - log_w)).astype(BF16)
    bt_scr[...] = (b_s * inv).astype(BF16)
    kt_scr[...] = (k2 * inv).astype(BF16)
    bh_scr[...] = (b_s * tail).astype(BF16)
    kh_scr[...] = (k2 * tail).astype(BF16)
    v_scr[...] = v.astype(BF16)
    rt_scr[...] = r * jnp.exp(cum)
    gc_scr[...] = jnp.exp(tot)

    row_t = lax.broadcasted_iota(jnp.int32, (cs, cs), 0) & (ch - 1)
    col_t = lax.broadcasted_iota(jnp.int32, (cs, cs), 1) & (ch - 1)
    strict = col_t < row_t
    row_t2 = lax.broadcasted_iota(jnp.int32, (cs, 2 * cs), 0) & (ch - 1)
    col_t2 = lax.broadcasted_iota(jnp.int32, (cs, 2 * cs), 1) & (ch - 1)
    incl2 = col_t2 <= row_t2
    eye = (lax.broadcasted_iota(jnp.int32, (cs, cs), 0) == lax.broadcasted_iota(jnp.int32, (cs, cs), 1)).astype(F32)
    lane_head = lax.broadcasted_iota(jnp.int32, (ch, BRANCH_W), 1) >> HEAD_SHIFT
    n_dbl = ch.bit_length() - 1

    def prepare(c):
        r0 = c * ch if isinstance(c, int) else pl.multiple_of(c * ch, ch)
        abd, bbd, kbd, bhd, khd, vbd = (_stack_heads(s[pl.ds(r0, ch), :], lane_head)
                                        for s in (at_scr, bt_scr, kt_scr, bh_scr, kh_scr, v_scr))
        rbd = _stack_heads(rt_scr[pl.ds(r0, ch), :], lane_head)
        s1 = _dot_nt(jnp.concatenate([abd, rbd.astype(BF16)], axis=0), jnp.concatenate([bbd, kbd], axis=0))
        l_ab = jnp.where(strict, s1[0:cs, 0:cs], 0.0)
        l_ak = jnp.where(strict, s1[0:cs, cs:2 * cs], 0.0)
        m_r = jnp.where(incl2, s1[cs:2 * cs, :], 0.0).astype(BF16)
        tinv = eye + l_ab
        lb16 = l_ab.astype(BF16)
        yield
        pw = _dot(lb16, lb16)
        for i in range(1, n_dbl):
            pw16 = pw.astype(BF16)
            yield
            if i < n_dbl - 1:
                both = _dot(jnp.concatenate([tinv.astype(BF16), pw16], axis=0), pw16)
                tinv = tinv + both[0:cs]
                pw = both[cs:2 * cs]
            else:
                tinv = tinv + _dot(tinv.astype(BF16), pw16)
        yield
        w1 = _dot(l_ak.astype(BF16), vbd)
        yield
        x = _dot(tinv.astype(BF16), jnp.concatenate([abd, w1.astype(BF16)], axis=1))
        yield
        z = _dot(m_r[:, 0:cs], x.astype(BF16))
        r_hat = rbd + z[:, 0:BRANCH_W]
        yield
        y_hat = z[:, BRANCH_W:2 * BRANCH_W] + _dot(m_r[:, cs:2 * cs], vbd)
        ar = jnp.concatenate([x[:, 0:BRANCH_W], r_hat], axis=0).astype(BF16)
        g_end = gc_scr[pl.ds(r0, ch), :][0:1, :]
        return ar, x[:, BRANCH_W:2 * BRANCH_W], y_hat, vbd, jnp.concatenate([bhd, khd], axis=0), g_end

    def advance(c0, prepared):
        for n, (ar, u_hat, y_hat, vbd, bk, g_end) in enumerate(prepared):
            c = c0 + n
            r0 = c * ch if isinstance(c, int) else pl.multiple_of(c * ch, ch)
            sb = s_scr[...]
            uy = _dot_nt(ar, sb.astype(BF16))
            u = uy[0:cs] + u_hat
            ys = uy[cs:2 * cs] + y_hat
            y_scr[pl.ds(r0, ch), :] = ys[0:ch] + ys[ch:2 * ch] + ys[2 * ch:3 * ch] + ys[3 * ch:4 * ch]
            yield
            s_scr[...] = sb * g_end + _dot_tn(jnp.concatenate([u.astype(BF16), vbd], axis=0), bk)
            yield

    def interleave(gens):
        done = [None] * len(gens)
        live = list(range(len(gens)))
        while live:
            for i in list(live):
                try:
                    next(gens[i])
                except StopIteration as stop:
                    done[i] = stop.value
                    live.remove(i)
        return done

    n_chunk = tc // ch
    grp = RWKV_GROUP if n_chunk % RWKV_GROUP == 0 else 1
    first = tuple(interleave([prepare(c) for c in range(grp)]))
    if n_chunk > grp:
        def group(i, carry):
            c0 = i * grp
            res = interleave([prepare(c0 + grp + n) for n in range(grp)] + [advance(c0, carry)])
            return tuple(res[0:grp])

        first = lax.fori_loop(0, n_chunk // grp - 1, group, first)

    def finish(r0, rows, live_rows):
        y = y_scr[r0:r0 + rows, :]
        yield
        mean = _head_sum(y, e3) * (1.0 / HEAD_W)
        dev = y - mean
        yield
        var = _head_sum(dev * dev, e3) * (1.0 / HEAD_W)
        yn = dev * lax.rsqrt(var + B_GN_EPS) * lnw_ref[0] + lnb_ref[0]
        out = (yn + bon_scr[r0:r0 + rows, :]) * g_scr[r0:r0 + rows, :]
        y_ref[0, r0:r0 + live_rows, :] = out[0:live_rows]

    done_rows = (n_chunk - grp) * ch
    interleave([advance(n_chunk - grp, first)] + ([finish(0, done_rows, done_rows)] if done_rows else []))
    interleave([finish(done_rows, tc - done_rows, t_blk - done_rows)])
    xs_scr[pl.ds(7, 1), :] = xs_scr[pl.ds(8 + t_blk - 1, 1), :]

    @pl.when(ti == n_t - 1)
    def _():
        last_ref[0] = xs_scr[pl.ds(7, 1), :]
        for h in range(N_HEADS):
            sn_ref[0, h] = s_scr[h * HEAD_W:(h + 1) * HEAD_W, h * HEAD_W:(h + 1) * HEAD_W]


def _rwkv(cols3, shift, s0, P, layer):
    b, t, _ = cols3.shape
    if t >= 64:
        ch, tc = 64, min(t, 1024)
        assert t % tc == 0
    else:
        ch = 16
        tc = _round_up(t, ch)
    n_t = max(1, t // tc)
    t_blk = min(t, tc)
    body = functools.partial(_rwkv_body, t, tc, n_t, ch)
    consts = _rwkv_consts(min(tc, 256), ch)
    names = ("b_mu", "b_w0", "b_w2p", "b_a0", "b_a2p", "b_g2", "b_k_k", "b_k_a", "b_r_k", "b_ln_w", "b_ln_b")
    half = B_COLS // 2
    sc16 = lambda: pltpu.VMEM((tc, BRANCH_W), BF16)
    sc32 = lambda: pltpu.VMEM((tc, BRANCH_W), F32)
    return pl.pallas_call(
        body,
        out_shape=(jax.ShapeDtypeStruct((b, t, BRANCH_W), F32),
                   jax.ShapeDtypeStruct((b, 1, B_COLS), F32),
                   jax.ShapeDtypeStruct((b, N_HEADS, HEAD_W, HEAD_W), F32)),
        grid=(b, n_t),
        in_specs=[pl.BlockSpec((1, t_blk, half), lambda i, j: (i, j, 1)),
                  pl.BlockSpec((1, t_blk, half), lambda i, j: (i, j, 2)),
                  pl.BlockSpec((1, 1, 1, B_COLS), lambda i, j: (layer, i, 0, 0)),
                  pl.BlockSpec((1, 1, N_HEADS, HEAD_W, HEAD_W), lambda i, j: (layer, i, 0, 0, 0))]
                 + [_layer_block(P[n].shape, layer) for n in names]
                 + [_resident(consts.shape), _resident(P["e3"].shape)],
        out_specs=(pl.BlockSpec((1, t_blk, BRANCH_W), lambda i, j: (i, j, 0)),
                   pl.BlockSpec((1, 1, B_COLS), lambda i, j: (i, 0, 0)),
                   pl.BlockSpec((1, N_HEADS, HEAD_W, HEAD_W), lambda i, j: (i, 0, 0, 0))),
        scratch_shapes=[pltpu.VMEM((tc + 8, B_COLS), F32),
                        pltpu.VMEM((BRANCH_W, BRANCH_W), F32),
                        sc16(), sc16(), sc16(), sc16(), sc16(), sc16(),
                        sc32(), sc32(), sc32(), sc32(), sc32()],
        compiler_params=_params(("parallel", "arbitrary")),
        name="rwkv7",
    )(cols3, cols3, shift, s0, *[P[n] for n in names], consts, P["e3"])


HGRN_TILES = 2


def _hgrn_consts(tc):
    nl = tc.bit_length() - 1
    t = np.arange(tc)
    tril = (t[None, :] <= t[:, None]).astype(np.float32)
    x = t[:, None] ^ t[None, :]
    lvl = np.floor(np.log2(np.maximum(x, 1))).astype(np.int32)
    lvl = np.where(t[:, None] > t[None, :], lvl, np.where(t[:, None] == t[None, :], nl, -1))
    return jnp.asarray(tril, BF16), jnp.asarray(lvl, jnp.int32), nl


def _hgrn_body(t_len, tc, n_sub, n_t, nl, cq_ref, cf_ref, cig_ref, s0_ref, lb_ref, ng_ref, sel_ref, lvl_ref,
               e3_ref, y_ref, sn_ref, in_scr, st_scr):
    ti = pl.program_id(1)
    t_blk = min(t_len, tc)

    @pl.when(ti == 0)
    def _():
        st_scr[...] = jnp.zeros(st_scr.shape, F32)
        for h in range(N_HEADS):
            st_scr[h * HEAD_W:(h + 1) * HEAD_W, h * C_EXPAND:(h + 1) * C_EXPAND] = s0_ref[0, 0, h]

    if t_blk != tc:
        in_scr[...] = jnp.zeros(in_scr.shape, F32)
        in_scr[0, pl.ds(0, t_blk), :] = cq_ref[0]
        in_scr[1, pl.ds(0, t_blk), :] = cf_ref[0]
        in_scr[2, pl.ds(0, t_blk), :] = cig_ref[0]

    gens = [_hgrn_tile(n * tc, t_blk, tc, nl, cq_ref, cf_ref, cig_ref, lb_ref, ng_ref, sel_ref, lvl_ref, e3_ref,
                       y_ref, in_scr, st_scr) for n in range(n_sub)]
    live = list(gens)
    while live:
        for g in list(live):
            try:
                next(g)
            except StopIteration:
                live.remove(g)

    @pl.when(ti == n_t - 1)
    def _():
        for h in range(N_HEADS):
            sn_ref[0, h] = st_scr[h * HEAD_W:(h + 1) * HEAD_W, h * C_EXPAND:(h + 1) * C_EXPAND]


def _hgrn_tile(r0, t_blk, tc, nl, cq_ref, cf_ref, cig_ref, lb_ref, ng_ref, sel_ref, lvl_ref, e3_ref, y_ref,
               in_scr, st_scr):
    if t_blk != tc:
        q, f, cig = in_scr[0], in_scr[1], in_scr[2]
    else:
        q, f, cig = cq_ref[0, r0:r0 + tc, :], cf_ref[0, r0:r0 + tc, :], cig_ref[0, r0:r0 + tc, :]
    lb = lb_ref[0]
    fg = lb + (1.0 - lb) * jax.nn.sigmoid(f)
    log_f = jnp.log(fg)
    kk = 1.0 - fg
    fg_live = fg
    if t_blk != tc:
        row = lax.broadcasted_iota(jnp.int32, (tc, 1), 0)
        live = row < t_blk
        log_f = jnp.where(live, log_f, 0.0)
        kk = jnp.where(live, kk, 0.0)
        fg_live = jnp.where(live, fg, 1.0)
    hi = log_f.astype(BF16)
    lo = (log_f - hi.astype(F32)).astype(BF16)
    parts = jnp.concatenate([hi, lo], axis=1)
    yield
    d = _dot(sel_ref[...], parts)
    cum = d[:, 0:C_FDIM] + d[:, C_FDIM:2 * C_FDIM]
    odd = (lax.broadcasted_iota(jnp.int32, (tc, 1), 0) & 1) == 1

    lvl = lvl_ref[...]
    att = [None] * N_HEADS
    for l in range(nl + 1):
        if l == 0:
            e = jnp.where(odd, fg_live, 1.0)
        elif l < nl:
            m = 1 << l
            ref_rows = cum.reshape(tc // (2 * m), 2 * m, C_FDIM)[:, m - 1:m, :]
            ref_rows = jnp.broadcast_to(ref_rows, (tc // (2 * m), 2 * m, C_FDIM)).reshape(tc, C_FDIM)
            e = jnp.exp(-jnp.abs(cum - ref_rows))
        if l < nl:
            qs = (q * e).astype(BF16)
            ks = (kk * e).astype(BF16)
        else:
            qs = q.astype(BF16)
            ks = kk.astype(BF16)
        hit = lvl == l
        yield
        for h in range(N_HEADS):
            kl = slice(h * C_EXPAND, (h + 1) * C_EXPAND)
            a = _dot_nt(qs[:, kl], ks[:, kl])
            att[h] = jnp.where(hit, a, 0.0 if att[h] is None else att[h])
    c_last = cum[tc - 1:tc, :]
    lane_head = lax.broadcasted_iota(jnp.int32, (tc, BRANCH_W), 1) >> HEAD_SHIFT
    v = cig[:, 0:BRANCH_W]
    v_stack = _stack_heads(v, lane_head).astype(BF16)
    p_cat = jnp.concatenate([a.astype(BF16) for a in att], axis=1)
    q_dec = (q * jnp.exp(cum)).astype(BF16)
    k_tail = (kk * jnp.exp(c_last - cum)).astype(BF16)
    yield
    o_intra = _dot(p_cat, v_stack)
    yield
    st = st_scr[...]
    o = o_intra + _dot_nt(q_dec, st.astype(BF16))
    upd = _dot_tn(v.astype(BF16), k_tail)
    rh = lax.broadcasted_iota(jnp.int32, (BRANCH_W, C_FDIM), 0) >> HEAD_SHIFT
    ch_ = lax.broadcasted_iota(jnp.int32, (BRANCH_W, C_FDIM), 1) >> C_EXPAND_SHIFT
    st_scr[...] = st * jnp.exp(c_last) + jnp.where(rh == ch_, upd, 0.0)
    yield
    ms = _head_sum(o * o, e3_ref[...]) * (1.0 / HEAD_W)
    y = o * lax.rsqrt(ms + NORM_EPS) * ng_ref[0] * jax.nn.silu(cig[:, BRANCH_W:2 * BRANCH_W])
    y_ref[0, r0:r0 + t_blk, :] = y[0:t_blk]


def _hgrn(cols3, s0_t, P, layer):
    b, t, _ = cols3.shape
    e3 = P["e3"]
    tc = 256 if t >= 256 else 16
    assert t % tc == 0 or t < tc
    t_blk = min(t, tc)
    n_sub = HGRN_TILES if (t_blk == tc and (t // tc) % HGRN_TILES == 0) else 1
    n_t = max(1, t // (n_sub * tc))
    sel, lvl, nl = _hgrn_consts(tc)
    body = functools.partial(_hgrn_body, t, tc, n_sub, n_t, nl)
    blk = lambda idx: pl.BlockSpec((1, n_sub * t_blk, C_FDIM), lambda i, j: (i, j, idx))
    pad_shape = (3, tc, C_FDIM) if t_blk != tc else (1, 8, 128)
    return pl.pallas_call(
        body,
        out_shape=(jax.ShapeDtypeStruct((b, t, BRANCH_W), F32),
                   jax.ShapeDtypeStruct((b, N_HEADS, HEAD_W, C_EXPAND), F32)),
        grid=(b, n_t),
        in_specs=[blk(3), blk(4), blk(5),
                  pl.BlockSpec((1, 1, N_HEADS, HEAD_W, C_EXPAND), lambda i, j: (layer, i, 0, 0, 0)),
                  _layer_block(P["c_lb"].shape, layer),
                  _layer_block(P["c_norm_g"].shape, layer),
                  _resident(sel.shape), _resident(lvl.shape), _resident(e3.shape)],
        out_specs=(pl.BlockSpec((1, n_sub * t_blk, BRANCH_W), lambda i, j: (i, j, 0)),
                   pl.BlockSpec((1, N_HEADS, HEAD_W, C_EXPAND), lambda i, j: (i, 0, 0, 0))),
        scratch_shapes=[pltpu.VMEM(pad_shape, F32),
                        pltpu.VMEM((BRANCH_W, C_FDIM), F32)],
        compiler_params=_params(("parallel", "arbitrary")),
        name="hgrn2",
    )(cols3, cols3, cols3, s0_t, P["c_lb"], P["c_norm_g"], sel, lvl, e3)


def _rope_rows(x, cos, sin_signed):
    lane = lax.broadcasted_iota(jnp.int32, x.shape, 1)
    swapped = jnp.where((lane & (HEAD_W - 1)) < HEAD_W // 2,
                        pltpu.roll(x, BRANCH_W - HEAD_W // 2, 1), pltpu.roll(x, HEAD_W // 2, 1))
    return x * cos + swapped * sin_signed


def _pattern_weight(d):
    ok = d >= 0
    w = ((d <= 128).astype(F32) + (((d & 3) == 0) & (d <= 512)).astype(F32)
         + (((d & 15) == 0) & (d <= 2048)).astype(F32))
    return jnp.where(ok, w, 0.0)


def _att_group(n_blk):
    return next(g for g in (4, 2, 1) if n_blk % g == 0)


def _att_weight_table(n_blk):
    grp = _att_group(n_blk)
    r = np.arange(ATT_BLK)
    delta = np.arange(-(grp - 1), n_blk)
    d = delta[:, None, None] * ATT_BLK + r[None, None, :] - r[None, :, None]
    w = ((d <= 128).astype(np.float32) + ((d % 4 == 0) & (d <= 512)) + ((d % 16 == 0) & (d <= 2048)))
    return jnp.asarray(np.where(d >= 0, w, 0.0), F32)


def _attn_prompt_body(n_blk, has_prev, qkv_ref, cos_ref, sin_ref, wt_ref, *rest):
    y_ref, kt_ref, vt_ref, qt_scr, kb_scr, vt_scr, s_scr = rest[2:] if has_prev else rest
    qkv = qkv_ref[0]
    cos = cos_ref[...]
    sin = sin_ref[...]
    q = _rope_rows(qkv[:, 0:BRANCH_W], cos, sin) * (HEAD_W ** -0.5)
    k = _rope_rows(qkv[:, BRANCH_W:2 * BRANCH_W], cos, sin)
    v_t = qkv[:, 2 * BRANCH_W:3 * BRANCH_W].T
    kb_scr[...] = k.astype(BF16)
    qt_scr[...] = q.T.astype(BF16)
    vt_scr[...] = v_t.astype(BF16)
    kt_ref[0, 0] = k.T
    vt_ref[0, 0] = v_t
    n_lane = N_HEADS * ATT_BLK
    row_head = lax.broadcasted_iota(jnp.int32, (BRANCH_W, n_lane), 0) >> HEAD_SHIFT
    col_head = lax.broadcasted_iota(jnp.int32, (BRANCH_W, n_lane), 1) >> 7
    own_head = row_head == col_head
    grp = _att_group(n_blk)

    def q_block(i, _):
        q0 = pl.multiple_of(i * ATT_BLK, ATT_BLK)
        qt = qt_scr[:, pl.ds(q0, ATT_BLK)]
        q_bd = jnp.where(own_head, jnp.concatenate([qt] * N_HEADS, axis=1), jnp.zeros((), BF16))

        def weights(j):
            w1 = wt_ref[i - j + (grp - 1)]
            return jnp.concatenate([w1] * N_HEADS, axis=1)

        n_grp = (i + grp) // grp

        def scores(gi):
            g_max = None
            for n in range(grp):
                j = gi * grp + n
                s = _dot(kb_scr[pl.ds(pl.multiple_of(j * ATT_BLK, ATT_BLK), ATT_BLK), :], q_bd)
                s = jnp.where(weights(j) > 0.0, s, NEG_BIG)
                s_scr[gi & 1, n] = s
                g_max = s if g_max is None else jnp.maximum(g_max, s)
            return jnp.max(g_max, axis=0, keepdims=True)

        def update(gi, g_max, m_run, l_run, acc):
            m_new = jnp.maximum(m_run, g_max)
            alpha = jnp.exp(m_run - m_new)
            l_new = alpha * l_run
            pv = [None] * N_HEADS
            for n in range(grp):
                j = gi * grp + n
                p = weights(j) * jnp.exp(s_scr[gi & 1, n] - m_new)
                l_new = l_new + jnp.sum(p, axis=0, keepdims=True)
                pb = p.astype(BF16)
                for h in range(N_HEADS):
                    vt = vt_scr[h * HEAD_W:(h + 1) * HEAD_W, pl.ds(pl.multiple_of(j * ATT_BLK, ATT_BLK), ATT_BLK)]
                    term = _dot(vt, pb[:, h * ATT_BLK:(h + 1) * ATT_BLK])
                    pv[h] = term if pv[h] is None else pv[h] + term
            heads = [alpha[:, h * ATT_BLK:(h + 1) * ATT_BLK] * acc[h * HEAD_W:(h + 1) * HEAD_W, :] + pv[h]
                     for h in range(N_HEADS)]
            return m_new, l_new, jnp.concatenate(heads, axis=0)

        def step(gi, carry):
            g_max, m_run, l_run, acc = carry
            nxt = scores(gi + 1)
            return (nxt,) + update(gi, g_max, m_run, l_run, acc)

        init = (scores(0), jnp.full((1, n_lane), NEG_BIG, F32), jnp.zeros((1, n_lane), F32),
                jnp.zeros((BRANCH_W, ATT_BLK), F32))
        g_max, m_run, l_run, acc = lax.fori_loop(0, n_grp - 1, step, init)
        _, l_run, acc = update(n_grp - 1, g_max, m_run, l_run, acc)
        out_t = jnp.concatenate([acc[h * HEAD_W:(h + 1) * HEAD_W, :] / l_run[:, h * ATT_BLK:(h + 1) * ATT_BLK]
                                 for h in range(N_HEADS)], axis=0)
        y_ref[0, pl.ds(q0, ATT_BLK), :] = out_t.T
        return 0

    lax.fori_loop(0, n_blk, q_block, 0)


def _attn_prompt(cols3, cos, sin, layer, depth, kv_prev):
    b, t, _ = cols3.shape
    n_blk = t // ATT_BLK
    assert n_blk * ATT_BLK == t
    has_prev = kv_prev is not None
    body = functools.partial(_attn_prompt_body, n_blk, has_prev)
    wtab = _att_weight_table(n_blk)
    kv_shape = jax.ShapeDtypeStruct((depth, b, BRANCH_W, t), F32)
    kv_spec = pl.BlockSpec((1, 1, BRANCH_W, t), lambda i: (layer, i, 0, 0))
    prev_in = list(kv_prev) if has_prev else []
    return pl.pallas_call(
        body,
        out_shape=(jax.ShapeDtypeStruct((b, t, BRANCH_W), F32), kv_shape, kv_shape),
        grid=(b,),
        in_specs=[pl.BlockSpec((1, t, 3 * BRANCH_W), lambda i: (i, 0, 4)),
                  _resident((t, BRANCH_W)), _resident((t, BRANCH_W)), _resident(wtab.shape)]
                 + [pl.BlockSpec(memory_space=pl.ANY)] * len(prev_in),
        out_specs=(pl.BlockSpec((1, t, BRANCH_W), lambda i: (i, 0, 0)), kv_spec, kv_spec),
        input_output_aliases={4: 1, 5: 2} if has_prev else {},
        scratch_shapes=[pltpu.VMEM((BRANCH_W, t), BF16),
                        pltpu.VMEM((t, BRANCH_W), BF16),
                        pltpu.VMEM((BRANCH_W, t), BF16),
                        pltpu.VMEM((2, _att_group(n_blk), ATT_BLK, N_HEADS * ATT_BLK), F32)],
        compiler_params=_params(("parallel",)),
        name="attn_prompt",
    )(cols3, cos, sin, wtab, *prev_in)


def _attn_sample_body(t_len, l_cache, qkv_ref, kc_ref, vc_ref, cos_ref, sin_ref, y_ref, k_ref, pad_scr):
    tp = 8
    pad_scr[...] = jnp.zeros(pad_scr.shape, F32)
    pad_scr[pl.ds(0, t_len), :] = qkv_ref[0]
    qkv = pad_scr[...]
    cos = cos_ref[...]
    sin = sin_ref[...]
    q = _rope_rows(qkv[:, 0:BRANCH_W], cos, sin) * (HEAD_W ** -0.5)
    k_new = _rope_rows(qkv[:, BRANCH_W:2 * BRANCH_W], cos, sin)
    v_new = qkv[:, 2 * BRANCH_W:3 * BRANCH_W]
    k_ref[0] = k_new[0:t_len]
    lane_head = lax.broadcasted_iota(jnp.int32, (tp, BRANCH_W), 1) >> HEAD_SHIFT
    q_bd = jnp.concatenate([jnp.where(lane_head == h, q, 0.0) for h in range(N_HEADS)], axis=0).astype(BF16)
    n_row = N_HEADS * tp
    k_t = kc_ref[0, 0].reshape(BRANCH_W, l_cache)
    v_t = vc_ref[0, 0].reshape(BRANCH_W, l_cache)
    s_c = _dot(q_bd, k_t.astype(BF16))
    s_n = _dot_nt(q_bd, k_new.astype(BF16))
    t_row = lax.broadcasted_iota(jnp.int32, (n_row, 1), 0) & (tp - 1)
    col_c = lax.broadcasted_iota(jnp.int32, (n_row, l_cache), 1)
    w_c = _pattern_weight(l_cache + t_row - col_c)
    col_n = lax.broadcasted_iota(jnp.int32, (n_row, tp), 1)
    w_n = jnp.where(col_n < t_len, _pattern_weight(t_row - col_n), 0.0)
    s_c = jnp.where(w_c > 0.0, s_c, NEG_BIG)
    s_n = jnp.where(w_n > 0.0, s_n, NEG_BIG)
    m = jnp.maximum(jnp.max(s_c, axis=-1, keepdims=True), jnp.max(s_n, axis=-1, keepdims=True))
    p_c = w_c * jnp.exp(s_c - m)
    p_n = w_n * jnp.exp(s_n - m)
    l = jnp.sum(p_c, axis=-1, keepdims=True) + jnp.sum(p_n, axis=-1, keepdims=True)
    num = _dot_nt(p_c.astype(BF16), v_t.astype(BF16)) + _dot(p_n.astype(BF16), v_new.astype(BF16))
    ratio = num / l
    out = jnp.zeros((tp, BRANCH_W), F32)
    for h in range(N_HEADS):
        out = jnp.where(lane_head == h, ratio[h * tp:(h + 1) * tp, :], out)
    y_ref[0] = out[0:t_len]


def _attn_sample(cols3, k_cache_t, v_cache_t, layer, cos, sin):
    b, t, _ = cols3.shape
    l_cache = k_cache_t.shape[-1]
    assert t <= 8
    body = functools.partial(_attn_sample_body, t, l_cache)
    cache_spec = pl.BlockSpec((1, 1, N_HEADS, HEAD_W, l_cache), lambda i: (layer, i, 0, 0, 0))
    return pl.pallas_call(
        body,
        out_shape=(jax.ShapeDtypeStruct((b, t, BRANCH_W), F32), jax.ShapeDtypeStruct((b, t, BRANCH_W), F32)),
        grid=(b,),
        in_specs=[pl.BlockSpec((1, t, 3 * BRANCH_W), lambda i: (i, 0, 4)),
                  cache_spec, cache_spec,
                  pl.BlockSpec((8, BRANCH_W), lambda i: (0, 0)),
                  pl.BlockSpec((8, BRANCH_W), lambda i: (0, 0))],
        out_specs=(pl.BlockSpec((1, t, BRANCH_W), lambda i: (i, 0, 0)),
                   pl.BlockSpec((1, t, BRANCH_W), lambda i: (i, 0, 0))),
        scratch_shapes=[pltpu.VMEM((8, 3 * BRANCH_W), F32)],
        compiler_params=_params(("parallel",)),
        name="attn_sample",
    )(cols3, k_cache_t, v_cache_t, cos, sin)


def _merge_body(x_ref, g_ref, win_ref, ya_ref, yb_ref, yc_ref, yd_ref, wb_ref, wo_ref, o_ref):
    x = x_ref[...]
    u = _rms_rows(x, g_ref[0]).astype(BF16)
    merged = None
    for n, y_ref in enumerate((ya_ref, yb_ref, yc_ref, yd_ref)):
        z = _dot(y_ref[...].astype(BF16), wb_ref[0, n])
        gate_cols = slice(N_MIX_COLS + n * D_MODEL, N_MIX_COLS + (n + 1) * D_MODEL)
        gate = jax.nn.sigmoid(_dot(u, win_ref[0, :, gate_cols]))
        merged = gate * z if merged is None else merged + gate * z
    o_ref[...] = x + _dot(merged.astype(BF16), wo_ref[0])


def _merge(x2d, ys, P, layer):
    n, d = x2d.shape
    tm = min(n, 512)
    row = lambda w: pl.BlockSpec((tm, w), lambda i: (i, 0))
    return pl.pallas_call(
        _merge_body,
        out_shape=jax.ShapeDtypeStruct((n, d), F32),
        grid=(n // tm,),
        in_specs=[row(d), _layer_block(P["norm1_g"].shape, layer), _layer_block(P["w_in"].shape, layer),
                  row(BRANCH_W), row(BRANCH_W), row(BRANCH_W), row(BRANCH_W),
                  _layer_block(P["w_branch"].shape, layer), _layer_block(P["w_out"].shape, layer)],
        out_specs=row(d),
        compiler_params=_params(("parallel",)),
        name="merge",
    )(x2d, P["norm1_g"], P["w_in"], *ys, P["w_branch"], P["w_out"])


FFN_COL = 1024


def _ffn_columns(x, v, prev1, prev2, first, second, wg_ref, wu_ref, wd_ref, cw_ref, cb_ref, keep_gate):
    acc = jnp.zeros(x.shape, F32)
    cw = cw_ref[0]
    for c in range(D_FF // FFN_COL):
        sl = slice(c * FFN_COL, (c + 1) * FFN_COL)
        hg = _dot(v, wg_ref[0, :, sl])
        hu = _dot(v, wu_ref[0, :, sl])
        h1 = jnp.where(first, prev1(sl), pltpu.roll(hg, 1, 0))
        h2 = jnp.where(first | second, prev2(sl), pltpu.roll(hg, 2, 0))
        conv = cb_ref[0, :, sl] + cw[2:3, sl] * hg + cw[1:2, sl] * h1 + cw[0:1, sl] * h2
        hmid = (jax.nn.gelu(conv) * hu).astype(BF16)
        acc = acc + _dot(hmid, wd_ref[0, sl, :])
        keep_gate(sl, hg)
    return x + acc


def _ffn_body(tm, n_t, has_final, x_ref, g_ref, wg_ref, wu_ref, wd_ref, cw_ref, cb_ref, st_ref, *rest):
    if has_final:
        fg_ref, o_ref, ns_ref, carry_scr = rest
    else:
        o_ref, ns_ref, carry_scr = rest
    ti = pl.program_id(1)

    @pl.when(ti == 0)
    def _():
        carry_scr[...] = st_ref[0, 0]

    x = x_ref[0]
    v = _rms_rows(x, g_ref[0]).astype(BF16)
    row = lax.broadcasted_iota(jnp.int32, (tm, 1), 0)

    def prev2(sl):
        return jnp.where(row == 0, carry_scr[0:1, sl], carry_scr[1:2, sl])

    def keep_gate(sl, hg):
        carry_scr[:, sl] = hg[tm - 2:tm, :]

    y = _ffn_columns(x, v, lambda sl: carry_scr[1:2, sl], prev2, row == 0, row == 1,
                     wg_ref, wu_ref, wd_ref, cw_ref, cb_ref, keep_gate)
    o_ref[0] = _rms_rows(y, fg_ref[...]) if has_final else y

    @pl.when(ti == n_t - 1)
    def _():
        ns_ref[0] = carry_scr[...]


def _ffn_rows_body(seq, has_final, x_ref, g_ref, wg_ref, wu_ref, wd_ref, cw_ref, cb_ref, p1_ref, p2_ref, *rest):
    if has_final:
        fg_ref, o_ref, hg_ref = rest
    else:
        o_ref, hg_ref = rest
    x = x_ref[...]
    v = _rms_rows(x, g_ref[0]).astype(BF16)
    step = lax.broadcasted_iota(jnp.int32, (x.shape[0], 1), 0) & (seq - 1)

    def keep_gate(sl, hg):
        hg_ref[:, sl] = hg

    y = _ffn_columns(x, v, lambda sl: p1_ref[:, sl], lambda sl: p2_ref[:, sl], step == 0, step == 1,
                     wg_ref, wu_ref, wd_ref, cw_ref, cb_ref, keep_gate)
    o_ref[...] = _rms_rows(y, fg_ref[...]) if has_final else y


def _ffn(x3, state, P, layer, final_g):
    b, t, d = x3.shape
    has_final = final_g is not None
    extra_in = [final_g] if has_final else []
    extra_spec = [_resident((1, d))] if has_final else []
    names = ("norm2_g", "ffn_w_gate", "ffn_w_up", "ffn_w_down", "ffn_conv_w", "ffn_conv_b")
    weights = [P[n] for n in names]
    weight_specs = [_layer_block(w.shape, layer) for w in weights]
    if t >= 8:
        tm = min(t, 512)
        n_t = t // tm
        assert tm * n_t == t and tm % 8 == 0
        body = functools.partial(_ffn_body, tm, n_t, has_final)
        return pl.pallas_call(
            body,
            out_shape=(jax.ShapeDtypeStruct((b, t, d), F32), jax.ShapeDtypeStruct((b, 2, D_FF), F32)),
            grid=(b, n_t),
            in_specs=[pl.BlockSpec((1, tm, d), lambda i, j: (i, j, 0))] + weight_specs
                     + [pl.BlockSpec((1, 1, 2, D_FF), lambda i, j: (layer, i, 0, 0))] + extra_spec,
            out_specs=(pl.BlockSpec((1, tm, d), lambda i, j: (i, j, 0)),
                       pl.BlockSpec((1, 2, D_FF), lambda i, j: (i, 0, 0))),
            scratch_shapes=[pltpu.VMEM((2, D_FF), F32)],
            compiler_params=_params(("parallel", "arbitrary")),
            name="convffn",
        )(x3, *weights, state, *extra_in)
    assert t >= 2 and t & (t - 1) == 0 and (b * t) % 8 == 0
    n = b * t
    st = state[layer]
    zero = jnp.zeros((b, t - 1, D_FF), F32)
    p1 = jnp.concatenate([st[:, 1:2], zero], axis=1).reshape(n, D_FF)
    p2 = jnp.concatenate([st, zero[:, 1:]], axis=1).reshape(n, D_FF)
    body = functools.partial(_ffn_rows_body, t, has_final)
    full = lambda shape: pl.BlockSpec(shape, lambda i: (0,) * len(shape))
    y, hg = pl.pallas_call(
        body,
        out_shape=(jax.ShapeDtypeStruct((n, d), F32), jax.ShapeDtypeStruct((n, D_FF), F32)),
        grid=(1,),
        in_specs=[full((n, d))] + weight_specs + [full((n, D_FF)), full((n, D_FF))] + extra_spec,
        out_specs=(full((n, d)), full((n, D_FF))),
        compiler_params=_params(("arbitrary",)),
        name="convffn_rows",
    )(x3.reshape(n, d), *weights, p1, p2, *extra_in)
    return y.reshape(b, t, d), hg.reshape(b, t, D_FF)[:, t - 2:, :]


def _block_diag(w):
    eye = jnp.asarray(np.eye(N_HEADS, dtype=np.float32))
    return jnp.einsum("lhij,hg->lhigj", w, eye).reshape(w.shape[0], BRANCH_W, BRANCH_W)


def _head_sum_matrix3():
    i = np.arange(BRANCH_W) // HEAD_W
    e = (i[:, None] == i[None, :]).astype(np.float32)
    return jnp.asarray(np.concatenate([e, e, e], axis=0), BF16)


def _rope_tables(pos, rows):
    half = HEAD_W // 2
    inv = ROPE_THETA ** (-jnp.arange(half, dtype=F32) / half)
    ang = pos.astype(F32)[:, None] * inv[None, :]
    cos = jnp.cos(ang)
    sin = jnp.sin(ang)
    cos_t = jnp.tile(jnp.concatenate([cos, cos], axis=1), (1, N_HEADS))
    sin_t = jnp.tile(jnp.concatenate([-sin, sin], axis=1), (1, N_HEADS))
    pad = rows - pos.shape[0]
    if pad:
        cos_t = jnp.pad(cos_t, ((0, pad), (0, 0)))
        sin_t = jnp.pad(sin_t, ((0, pad), (0, 0)))
    return cos_t, sin_t


def _all_layer_params(W):
    row = lambda a: a.reshape(a.shape[0], 1, -1)
    depth = W["w_in"].shape[0]
    zpad = jnp.zeros((depth, 64, BRANCH_W), F32)
    lb_sm = jax.nn.softmax(W["c_lb"].astype(F32), axis=0)
    P = {
        "w_in": W["w_in"].astype(BF16),
        "a_gx_bd": _split_weight(_block_diag(W["a_gx_w"])),
        "a_ga_bd": _split_weight(_block_diag(W["a_ga_w"])),
        "b_w2p": _split_weight(jnp.concatenate([W["b_w2"], zpad], axis=1)),
        "b_a2p": _split_weight(jnp.concatenate([zpad, W["b_a2"]], axis=1)),
        "b_g2": _split_weight(W["b_g2"]),
        "c_lb": row(jnp.cumsum(lb_sm, axis=0) - lb_sm[0]),
        "c_norm_g": row(jnp.tile(W["c_norm_g"], (1, N_HEADS))),
        "w_branch": W["w_branch"].astype(BF16), "w_out": W["w_out"].astype(BF16),
        "ffn_w_gate": W["ffn_w_gate"].astype(BF16), "ffn_w_up": W["ffn_w_up"].astype(BF16),
        "ffn_w_down": W["ffn_w_down"].astype(BF16),
        "a_conv_w": W["a_conv_w"], "ffn_conv_w": W["ffn_conv_w"],
        "e3": _head_sum_matrix3(),
    }
    for n in ("norm1_g", "norm2_g", "a_conv_b", "a_gx_b", "a_ga_b", "a_lambda", "b_mu", "b_w0", "b_a0", "b_k_k",
              "b_k_a", "b_r_k", "b_ln_w", "b_ln_b", "ffn_conv_b"):
        P[n] = row(W[n])
    return P


def _trunk(x, pos0, st, W, P):
    b, t, d = x.shape
    depth = W["w_in"].shape[0]
    has_cache = st["k"] is not None
    if has_cache:
        cos, sin = _rope_tables(pos0 + jnp.arange(t), 8)
        k_cache_t = jnp.transpose(st["k"], (0, 1, 3, 4, 2))
        v_cache_t = jnp.transpose(st["v"], (0, 1, 3, 4, 2))
    else:
        cos, sin = _rope_tables(pos0 + jnp.arange(t), t)
    ha = st["ha"][:, :, None, :]
    sh = st["sh"][:, :, None, :]
    sc_t = jnp.transpose(st["sc"], (0, 1, 2, 4, 3))
    outs = {n: [] for n in ("ha", "ca", "wkv", "sh", "sc", "k", "v", "cf")}
    kv_prev = None
    for l in range(depth):
        cols = _inproj(x.reshape(b * t, d), P, l).reshape(b, t, N_MIX_COLS)
        y_a, h_new, ca_new = _rglru(cols, ha, st["ca"], P, l)
        y_b, sh_new, wkv_new = _rwkv(cols, sh, st["wkv"], P, l)
        y_c, sc_new = _hgrn(cols, sc_t, P, l)
        if has_cache:
            y_d, k_rows = _attn_sample(cols, k_cache_t, v_cache_t, l, cos, sin)
            k_out = k_rows.reshape(b, t, N_HEADS, HEAD_W)
            v_out = cols[:, :, N_MIX_COLS - BRANCH_W:].reshape(b, t, N_HEADS, HEAD_W)
        else:
            y_d, k_t, v_t = _attn_prompt(cols, cos, sin, l, depth, kv_prev)
            kv_prev = (k_t, v_t)
        ys = [y.reshape(b * t, BRANCH_W) for y in (y_a, y_b, y_c, y_d)]
        x1 = _merge(x.reshape(b * t, d), ys, P, l)
        final_g = W["final_norm_g"][None, :] if l == depth - 1 else None
        x, cf_new = _ffn(x1.reshape(b, t, d), st["cf"], P, l, final_g)
        outs["ha"].append(h_new[:, 0, :])
        outs["ca"].append(ca_new)
        outs["wkv"].append(wkv_new)
        outs["sh"].append(sh_new[:, 0, :])
        outs["sc"].append(sc_new)
        outs["cf"].append(cf_new)
        if has_cache:
            outs["k"].append(k_out)
            outs["v"].append(v_out)
    if not has_cache:
        del outs["k"], outs["v"]
    res = {n: jnp.stack(outs[n]) for n in outs}
    res["sc"] = jnp.transpose(res["sc"], (0, 1, 2, 4, 3))
    if not has_cache:
        keep = min(D_WIN_MAX, t)
        for n, buf in zip(("k", "v"), kv_prev):
            res[n] = jnp.transpose(buf.reshape(depth, b, N_HEADS, HEAD_W, t)[..., t - keep:], (0, 1, 4, 2, 3))
    return (x,) + tuple(res[n] for n in ("ha", "ca", "wkv", "sh", "sc", "k", "v", "cf"))


def kernel(x_prompt, x_sample, state_a_h, state_a_conv, state_b_wkv, state_b_shift, state_c_s, cache_d_k, cache_d_v, state_ffn_conv, norm1_g, w_in, a_conv_w, a_conv_b, a_gx_w, a_gx_b, a_ga_w, a_ga_b, a_lambda, b_mu, b_w0, b_w2, b_a0, b_a2, b_g2, b_k_k, b_k_a, b_r_k, b_ln_w, b_ln_b, c_lb, c_norm_g, w_branch, w_out, norm2_g, ffn_w_gate, ffn_w_up, ffn_conv_w, ffn_conv_b, ffn_w_down, final_norm_g):
    W = {"norm1_g": norm1_g, "w_in": w_in, "a_conv_w": a_conv_w, "a_conv_b": a_conv_b, "a_gx_w": a_gx_w,
         "a_gx_b": a_gx_b, "a_ga_w": a_ga_w, "a_ga_b": a_ga_b, "a_lambda": a_lambda, "b_mu": b_mu,
         "b_w0": b_w0, "b_w2": b_w2, "b_a0": b_a0, "b_a2": b_a2, "b_g2": b_g2, "b_k_k": b_k_k,
         "b_k_a": b_k_a, "b_r_k": b_r_k, "b_ln_w": b_ln_w, "b_ln_b": b_ln_b, "c_lb": c_lb,
         "c_norm_g": c_norm_g, "w_branch": w_branch, "w_out": w_out, "norm2_g": norm2_g,
         "ffn_w_gate": ffn_w_gate, "ffn_w_up": ffn_w_up, "ffn_conv_w": ffn_conv_w,
         "ffn_conv_b": ffn_conv_b, "ffn_w_down": ffn_w_down, "final_norm_g": final_norm_g}
    depth = w_in.shape[0]
    P = _all_layer_params(W)
    b_p, t_p = x_prompt.shape[:2]

    def zeros(*s):
        return jnp.zeros((depth, b_p) + s, F32)

    st_p = {"ha": zeros(BRANCH_W), "ca": zeros(3, BRANCH_W), "wkv": zeros(N_HEADS, HEAD_W, HEAD_W),
            "sh": zeros(B_COLS), "sc": zeros(N_HEADS, C_EXPAND, HEAD_W), "k": None, "v": None,
            "cf": zeros(2, D_FF)}
    out_p = _trunk(x_prompt, 0, st_p, W, P)
    st_s = {"ha": state_a_h, "ca": state_a_conv, "wkv": state_b_wkv, "sh": state_b_shift, "sc": state_c_s,
            "k": cache_d_k, "v": cache_d_v, "cf": state_ffn_conv}
    out_s = _trunk(x_sample, PAST_LEN, st_s, W, P)
    return (out_p[0], out_s[0]) + out_p[1:] + out_s[1:]
```

```python
import functools

import numpy as np

import jax
import jax.numpy as jnp
from jax import lax
from jax.experimental import pallas as pl
from jax.experimental.pallas import tpu as pltpu

F32 = jnp.float32
BF16 = jnp.bfloat16
HI = lax.Precision.HIGHEST

D_MODEL = 1024
BRANCH_W = 256
N_HEADS = 4
HEAD_W = 64
HEAD_SHIFT = 6
C_EXPAND = 128
C_EXPAND_SHIFT = 7
C_FDIM = 512
B_COLS = 1024
D_FF = 3072
NORM_EPS = 1e-6
A_C = 8.0
B_GN_EPS = 64e-5
ROPE_THETA = 10000.0
PAST_LEN = 8192
D_WIN_MAX = 2048
NEG_BIG = -1e30
N_MIX_COLS = 3840
ATT_BLK = 128

VMEM_LIMIT = 56 * 1024 * 1024


def _dot(a, b, prec=None):
    return jnp.dot(a, b, preferred_element_type=F32, precision=prec)


def _dot_nt(a, b, prec=None):
    return lax.dot_general(a, b, (((1,), (1,)), ((), ())), preferred_element_type=F32, precision=prec)


def _dot_tn(a, b, prec=None):
    return lax.dot_general(a, b, (((0,), (0,)), ((), ())), preferred_element_type=F32, precision=prec)


def _rms_rows(x, g):
    ms = jnp.mean(x * x, axis=-1, keepdims=True)
    return x * lax.rsqrt(ms + NORM_EPS) * g


def _round_up(n, m):
    return (n + m - 1) // m * m


def _params(sem):
    return pltpu.CompilerParams(dimension_semantics=sem, vmem_limit_bytes=VMEM_LIMIT)


def _resident(shape):
    nd = len(shape)
    return pl.BlockSpec(shape, lambda *_: (0,) * nd, pipeline_mode=pl.Buffered(1))


def _split3(x):
    hi = x.astype(BF16)
    r1 = x - hi.astype(F32)
    mid = r1.astype(BF16)
    lo = (r1 - mid.astype(F32)).astype(BF16)
    return hi, mid, lo


def _head_sum(x, e3):
    return _dot(jnp.concatenate(_split3(x), axis=1), e3)


def _dot_split(x, w3):
    hi = x.astype(BF16)
    lo = (x - hi.astype(F32)).astype(BF16)
    return _dot(jnp.concatenate([hi, lo, hi], axis=1), w3)


def _split_weight(w):
    hi = w.astype(BF16)
    lo = (w - hi.astype(F32)).astype(BF16)
    return jnp.concatenate([hi, hi, lo], axis=1)


def _stack_heads(x, lane_head):
    return jnp.concatenate([jnp.where(lane_head == h, x, jnp.zeros_like(x)) for h in range(N_HEADS)], axis=0)


def _layer_block(shape, layer):
    nd = len(shape)
    return pl.BlockSpec((1,) + tuple(shape[1:]), lambda *_: (layer,) + (0,) * (nd - 1),
                        pipeline_mode=pl.Buffered(1))


def _inproj_body(x_ref, g_ref, w_ref, o_ref):
    o_ref[...] = _dot(_rms_rows(x_ref[...], g_ref[0]).astype(BF16), w_ref[0])


def _inproj(x2d, P, layer):
    n, d = x2d.shape
    tm = min(n, 1024)
    return pl.pallas_call(
        _inproj_body,
        out_shape=jax.ShapeDtypeStruct((n, N_MIX_COLS), F32),
        grid=(n // tm,),
        in_specs=[pl.BlockSpec((tm, d), lambda i: (i, 0)),
                  _layer_block(P["norm1_g"].shape, layer),
                  pl.BlockSpec((1, d, N_MIX_COLS), lambda i: (layer, 0, 0), pipeline_mode=pl.Buffered(1))],
        out_specs=pl.BlockSpec((tm, N_MIX_COLS), lambda i: (i, 0)),
        compiler_params=_params(("parallel",)),
        name="inproj",
    )(x2d, P["norm1_g"], P["w_in"])


def _rglru_body(t_len, tp, axg_ref, h0_ref, cbuf_ref, cw_ref, cb_ref, wgx_ref, bgx_ref, wga_ref, bga_ref,
                lam_ref, y_ref, hl_ref, nb_ref, xs_scr, a_scr, b_scr):
    if tp != t_len:
        xs_scr[...] = jnp.zeros(xs_scr.shape, F32)
    xs_scr[pl.ds(5, 3), :] = jnp.concatenate([cbuf_ref[0, 0], jnp.zeros((3, BRANCH_W), F32)], axis=1)
    xs_scr[pl.ds(8, t_len), :] = axg_ref[0]
    x0 = xs_scr[pl.ds(8, tp), 0:BRANCH_W]
    x1 = xs_scr[pl.ds(7, tp), 0:BRANCH_W]
    x2 = xs_scr[pl.ds(6, tp), 0:BRANCH_W]
    x3 = xs_scr[pl.ds(5, tp), 0:BRANCH_W]
    cw = cw_ref[0]
    xc = cb_ref[0] + cw[3:4] * x0 + cw[2:3] * x1 + cw[1:2] * x2 + cw[0:1] * x3
    gate_x = jax.nn.sigmoid(_dot_split(xc, wgx_ref[0]) + bgx_ref[0])
    gate_a = jax.nn.sigmoid(_dot_split(xc, wga_ref[0]) + bga_ref[0])
    log_a = -A_C * gate_a * jax.nn.softplus(-lam_ref[0])
    a = jnp.exp(log_a)
    th = jnp.tanh(log_a)
    b_in = jnp.sqrt(-2.0 * th / (1.0 - th)) * (gate_x * xc)
    if tp != t_len:
        row = lax.broadcasted_iota(jnp.int32, (tp, 1), 0)
        a = jnp.where(row < t_len, a, 1.0)
        b_in = jnp.where(row < t_len, b_in, 0.0)
    a_scr[...] = a
    b_scr[...] = b_in

    row8 = lax.broadcasted_iota(jnp.int32, (8, 1), 0)

    def group(g, carry):
        r0 = pl.multiple_of(g * 8, 8)
        ag = a_scr[pl.ds(r0, 8), :]
        bg = b_scr[pl.ds(r0, 8), :]
        for s in (1, 2, 4):
            a_sh = jnp.where(row8 >= s, pltpu.roll(ag, s, 0), 1.0)
            b_sh = jnp.where(row8 >= s, pltpu.roll(bg, s, 0), 0.0)
            bg = ag * b_sh + bg
            ag = ag * a_sh
        h = ag * carry + bg
        b_scr[pl.ds(r0, 8), :] = h
        return h[7:8, :]

    h_last = lax.fori_loop(0, tp // 8, group, h0_ref[0, 0])
    h = b_scr[...]
    gate = xs_scr[pl.ds(8, tp), BRANCH_W:2 * BRANCH_W]
    y = h * jax.nn.gelu(gate)
    y_ref[0] = y[0:t_len]
    hl_ref[0] = h_last
    nb_ref[0] = xs_scr[pl.ds(8 + t_len - 3, 3), 0:BRANCH_W]


def _rglru(cols3, h0, cbuf, P, layer):
    b, t, _ = cols3.shape
    tp = _round_up(t, 8)
    body = functools.partial(_rglru_body, t, tp)
    names = ("a_conv_w", "a_conv_b", "a_gx_bd", "a_gx_b", "a_ga_bd", "a_ga_b", "a_lambda")
    return pl.pallas_call(
        body,
        out_shape=(jax.ShapeDtypeStruct((b, t, BRANCH_W), F32),
                   jax.ShapeDtypeStruct((b, 1, BRANCH_W), F32),
                   jax.ShapeDtypeStruct((b, 3, BRANCH_W), F32)),
        grid=(b,),
        in_specs=[pl.BlockSpec((1, t, 2 * BRANCH_W), lambda i: (i, 0, 0)),
                  pl.BlockSpec((1, 1, 1, BRANCH_W), lambda i: (layer, i, 0, 0)),
                  pl.BlockSpec((1, 1, 3, BRANCH_W), lambda i: (layer, i, 0, 0))]
                 + [_layer_block(P[n].shape, layer) for n in names],
        out_specs=(pl.BlockSpec((1, t, BRANCH_W), lambda i: (i, 0, 0)),
                   pl.BlockSpec((1, 1, BRANCH_W), lambda i: (i, 0, 0)),
                   pl.BlockSpec((1, 3, BRANCH_W), lambda i: (i, 0, 0))),
        scratch_shapes=[pltpu.VMEM((tp + 8, 2 * BRANCH_W), F32),
                        pltpu.VMEM((tp, BRANCH_W), F32),
                        pltpu.VMEM((tp, BRANCH_W), F32)],
        compiler_params=_params(("parallel",)),
        name="rglru",
    )(cols3, h0, cbuf, *[P[n] for n in names])


RWKV_GROUP = 4


def _rwkv_consts(tc, ch):
    t = np.arange(tc)
    same = (t[:, None] // ch) == (t[None, :] // ch)
    tril = same & (t[None, :] <= t[:, None])
    return jnp.asarray(np.concatenate([tril, same], axis=0).astype(np.float32), BF16)


def _rwkv_body(t_len, tc, n_t, ch, cb1_ref, cb2_ref, shift_ref, s0_ref, mu_ref, w0_ref, w2_ref, a0_ref, a2_ref,
               g2_ref, kk_ref, ka_ref, rk_ref, lnw_ref, lnb_ref, cs_ref, e3_ref, y_ref, last_ref, sn_ref,
               xs_scr, s_scr, at_scr, bt_scr, kt_scr, bh_scr, kh_scr, v_scr, rt_scr, gc_scr, y_scr, bon_scr,
               g_scr):
    ti = pl.program_id(1)
    t_blk = min(t_len, tc)
    cs = N_HEADS * ch
    half = B_COLS // 2

    @pl.when(ti == 0)
    def _():
        if t_blk != tc:
            xs_scr[...] = jnp.zeros(xs_scr.shape, F32)
        xs_scr[pl.ds(7, 1), :] = shift_ref[0, 0]
        s_scr[...] = jnp.zeros(s_scr.shape, F32)
        for h in range(N_HEADS):
            s_scr[h * HEAD_W:(h + 1) * HEAD_W, h * HEAD_W:(h + 1) * HEAD_W] = s0_ref[0, 0, h]

    xs_scr[pl.ds(8, t_blk), 0:half] = cb1_ref[0]
    xs_scr[pl.ds(8, t_blk), half:B_COLS] = cb2_ref[0]
    cb = xs_scr[pl.ds(8, tc), :]
    shifted = xs_scr[pl.ds(7, tc), :]
    cm = cb + (shifted - cb) * mu_ref[0]
    r = cm[:, 0:256]
    k = cm[:, 256:512]
    v = cm[:, 512:768]
    lora = cm[:, 768:896]
    w = -jax.nn.softplus(-(w0_ref[0] + _dot_split(jnp.tanh(lora), w2_ref[0]))) - 0.5
    log_w = -jnp.exp(w)
    a = jax.nn.sigmoid(a0_ref[0] + _dot_split(lora, a2_ref[0]))
    g_scr[...] = _dot_split(jax.nn.sigmoid(cm[:, 896:1024]), g2_ref[0])
    e3 = e3_ref[...]
    kk = k * kk_ref[0]
    kk = kk / jnp.maximum(jnp.sqrt(_head_sum(kk * kk, e3)), 1e-12)
    k2 = k * (1.0 + (a - 1.0) * ka_ref[0])
    bon_scr[...] = _head_sum(r * k2 * rk_ref[0], e3) * v
    a_s = -kk
    b_s = kk * a
    if t_blk != tc:
        row = lax.broadcasted_iota(jnp.int32, (tc, 1), 0)
        live = row < t_blk
        log_w = jnp.where(live, log_w, 0.0)
        a_s = jnp.where(live, a_s, 0.0)
        b_s = jnp.where(live, b_s, 0.0)
        k2 = jnp.where(live, k2, 0.0)
    parts = jnp.concatenate(_split3(log_w), axis=1)
    sub = cs_ref.shape[1]
    cum, tot = [], []
    for j in range(tc // sub):
        cc = _dot(cs_ref[...], parts[j * sub:(j + 1) * sub, :])
        cc = cc[:, 0:256] + cc[:, 256:512] + cc[:, 512:768]
        cum.append(cc[0:sub])
        tot.append(cc[sub:2 * sub])
    cum = jnp.concatenate(cum, axis=0)
    tot = jnp.concatenate(tot, axis=0)
    inv = jnp.exp(-cum)
    tail = jnp.exp(tot - cum)
    at_scr[...] = (a_s * jnp.exp(cum - log_w)).astype(BF16)
    bt_scr[...] = (b_s * inv).astype(BF16)
    kt_scr[...] = (k2 * inv).astype(BF16)
    bh_scr[...] = (b_s * tail).astype(BF16)
    kh_scr[...] = (k2 * tail).astype(BF16)
    v_scr[...] = v.astype(BF16)
    rt_scr[...] = r * jnp.exp(cum)
    gc_scr[...] = jnp.exp(tot)

    row_t = lax.broadcasted_iota(jnp.int32, (cs, cs), 0) & (ch - 1)
    col_t = lax.broadcasted_iota(jnp.int32, (cs, cs), 1) & (ch - 1)
    strict = col_t < row_t
    row_t2 = lax.broadcasted_iota(jnp.int32, (cs, 2 * cs), 0) & (ch - 1)
    col_t2 = lax.broadcasted_iota(jnp.int32, (cs, 2 * cs), 1) & (ch - 1)
    incl2 = col_t2 <= row_t2
    eye = (lax.broadcasted_iota(jnp.int32, (cs, cs), 0) == lax.broadcasted_iota(jnp.int32, (cs, cs), 1)).astype(F32)
    lane_head = lax.broadcasted_iota(jnp.int32, (ch, BRANCH_W), 1) >> HEAD_SHIFT
    n_dbl = ch.bit_length() - 1

    def prepare(c):
        r0 = c * ch if isinstance(c, int) else pl.multiple_of(c * ch, ch)
        abd, bbd, kbd, bhd, khd, vbd = (_stack_heads(s[pl.ds(r0, ch), :], lane_head)
                                        for s in (at_scr, bt_scr, kt_scr, bh_scr, kh_scr, v_scr))
        rbd = _stack_heads(rt_scr[pl.ds(r0, ch), :], lane_head)
        s1 = _dot_nt(jnp.concatenate([abd, rbd.astype(BF16)], axis=0), jnp.concatenate([bbd, kbd], axis=0))
        l_ab = jnp.where(strict, s1[0:cs, 0:cs], 0.0)
        l_ak = jnp.where(strict, s1[0:cs, cs:2 * cs], 0.0)
        m_r = jnp.where(incl2, s1[cs:2 * cs, :], 0.0).astype(BF16)
        tinv = eye + l_ab
        lb16 = l_ab.astype(BF16)
        yield
        pw = _dot(lb16, lb16)
        for i in range(1, n_dbl):
            pw16 = pw.astype(BF16)
            yield
            if i < n_dbl - 1:
                both = _dot(jnp.concatenate([tinv.astype(BF16), pw16], axis=0), pw16)
                tinv = tinv + both[0:cs]
                pw = both[cs:2 * cs]
            else:
                tinv = tinv + _dot(tinv.astype(BF16), pw16)
        yield
        w1 = _dot(l_ak.astype(BF16), vbd)
        yield
        x = _dot(tinv.astype(BF16), jnp.concatenate([abd, w1.astype(BF16)], axis=1))
        yield
        z = _dot(m_r[:, 0:cs], x.astype(BF16))
        r_hat = rbd + z[:, 0:BRANCH_W]
        yield
        y_hat = z[:, BRANCH_W:2 * BRANCH_W] + _dot(m_r[:, cs:2 * cs], vbd)
        ar = jnp.concatenate([x[:, 0:BRANCH_W], r_hat], axis=0).astype(BF16)
        g_end = gc_scr[pl.ds(r0, ch), :][0:1, :]
        return ar, x[:, BRANCH_W:2 * BRANCH_W], y_hat, vbd, jnp.concatenate([bhd, khd], axis=0), g_end

    def advance(c0, prepared):
        for n, (ar, u_hat, y_hat, vbd, bk, g_end) in enumerate(prepared):
            c = c0 + n
            r0 = c * ch if isinstance(c, int) else pl.multiple_of(c * ch, ch)
            sb = s_scr[...]
            uy = _dot_nt(ar, sb.astype(BF16))
            u = uy[0:cs] + u_hat
            ys = uy[cs:2 * cs] + y_hat
            y_scr[pl.ds(r0, ch), :] = ys[0:ch] + ys[ch:2 * ch] + ys[2 * ch:3 * ch] + ys[3 * ch:4 * ch]
            yield
            s_scr[...] = sb * g_end + _dot_tn(jnp.concatenate([u.astype(BF16), vbd], axis=0), bk)
            yield

    def interleave(gens):
        done = [None] * len(gens)
        live = list(range(len(gens)))
        while live:
            for i in list(live):
                try:
                    next(gens[i])
                except StopIteration as stop:
                    done[i] = stop.value
                    live.remove(i)
        return done

    n_chunk = tc // ch
    grp = RWKV_GROUP if n_chunk % RWKV_GROUP == 0 else 1
    first = tuple(interleave([prepare(c) for c in range(grp)]))
    if n_chunk > grp:
        def group(i, carry):
            c0 = i * grp
            res = interleave([prepare(c0 + grp + n) for n in range(grp)] + [advance(c0, carry)])
            return tuple(res[0:grp])

        first = lax.fori_loop(0, n_chunk // grp - 1, group, first)

    def finish(r0, rows, live_rows):
        y = y_scr[r0:r0 + rows, :]
        yield
        mean = _head_sum(y, e3) * (1.0 / HEAD_W)
        dev = y - mean
        yield
        var = _head_sum(dev * dev, e3) * (1.0 / HEAD_W)
        yn = dev * lax.rsqrt(var + B_GN_EPS) * lnw_ref[0] + lnb_ref[0]
        out = (yn + bon_scr[r0:r0 + rows, :]) * g_scr[r0:r0 + rows, :]
        y_ref[0, r0:r0 + live_rows, :] = out[0:live_rows]

    done_rows = (n_chunk - grp) * ch
    interleave([advance(n_chunk - grp, first)] + ([finish(0, done_rows, done_rows)] if done_rows else []))
    interleave([finish(done_rows, tc - done_rows, t_blk - done_rows)])
    xs_scr[pl.ds(7, 1), :] = xs_scr[pl.ds(8 + t_blk - 1, 1), :]

    @pl.when(ti == n_t - 1)
    def _():
        last_ref[0] = xs_scr[pl.ds(7, 1), :]
        for h in range(N_HEADS):
            sn_ref[0, h] = s_scr[h * HEAD_W:(h + 1) * HEAD_W, h * HEAD_W:(h + 1) * HEAD_W]


def _rwkv(cols3, shift, s0, P, layer):
    b, t, _ = cols3.shape
    if t >= 64:
        ch, tc = 64, min(t, 1024)
        assert t % tc == 0
    else:
        ch = 16
        tc = _round_up(t, ch)
    n_t = max(1, t // tc)
    t_blk = min(t, tc)
    body = functools.partial(_rwkv_body, t, tc, n_t, ch)
    consts = _rwkv_consts(min(tc, 256), ch)
    names = ("b_mu", "b_w0", "b_w2p", "b_a0", "b_a2p", "b_g2", "b_k_k", "b_k_a", "b_r_k", "b_ln_w", "b_ln_b")
    half = B_COLS // 2
    sc16 = lambda: pltpu.VMEM((tc, BRANCH_W), BF16)
    sc32 = lambda: pltpu.VMEM((tc, BRANCH_W), F32)
    return pl.pallas_call(
        body,
        out_shape=(jax.ShapeDtypeStruct((b, t, BRANCH_W), F32),
                   jax.ShapeDtypeStruct((b, 1, B_COLS), F32),
                   jax.ShapeDtypeStruct((b, N_HEADS, HEAD_W, HEAD_W), F32)),
        grid=(b, n_t),
        in_specs=[pl.BlockSpec((1, t_blk, half), lambda i, j: (i, j, 1)),
                  pl.BlockSpec((1, t_blk, half), lambda i, j: (i, j, 2)),
                  pl.BlockSpec((1, 1, 1, B_COLS), lambda i, j: (layer, i, 0, 0)),
                  pl.BlockSpec((1, 1, N_HEADS, HEAD_W, HEAD_W), lambda i, j: (layer, i, 0, 0, 0))]
                 + [_layer_block(P[n].shape, layer) for n in names]
                 + [_resident(consts.shape), _resident(P["e3"].shape)],
        out_specs=(pl.BlockSpec((1, t_blk, BRANCH_W), lambda i, j: (i, j, 0)),
                   pl.BlockSpec((1, 1, B_COLS), lambda i, j: (i, 0, 0)),
                   pl.BlockSpec((1, N_HEADS, HEAD_W, HEAD_W), lambda i, j: (i, 0, 0, 0))),
        scratch_shapes=[pltpu.VMEM((tc + 8, B_COLS), F32),
                        pltpu.VMEM((BRANCH_W, BRANCH_W), F32),
                        sc16(), sc16(), sc16(), sc16(), sc16(), sc16(),
                        sc32(), sc32(), sc32(), sc32(), sc32()],
        compiler_params=_params(("parallel", "arbitrary")),
        name="rwkv7",
    )(cols3, cols3, shift, s0, *[P[n] for n in names], consts, P["e3"])


HGRN_TILES = 2


def _hgrn_consts(tc):
    nl = tc.bit_length() - 1
    t = np.arange(tc)
    tril = (t[None, :] <= t[:, None]).astype(np.float32)
    x = t[:, None] ^ t[None, :]
    lvl = np.floor(np.log2(np.maximum(x, 1))).astype(np.int32)
    lvl = np.where(t[:, None] > t[None, :], lvl, np.where(t[:, None] == t[None, :], nl, -1))
    return jnp.asarray(tril, BF16), jnp.asarray(lvl, jnp.int32), nl


def _hgrn_body(t_len, tc, n_sub, n_t, nl, cq_ref, cf_ref, cig_ref, s0_ref, lb_ref, ng_ref, sel_ref, lvl_ref,
               e3_ref, y_ref, sn_ref, in_scr, st_scr):
    ti = pl.program_id(1)
    t_blk = min(t_len, tc)

    @pl.when(ti == 0)
    def _():
        st_scr[...] = jnp.zeros(st_scr.shape, F32)
        for h in range(N_HEADS):
            st_scr[h * HEAD_W:(h + 1) * HEAD_W, h * C_EXPAND:(h + 1) * C_EXPAND] = s0_ref[0, 0, h]

    if t_blk != tc:
        in_scr[...] = jnp.zeros(in_scr.shape, F32)
        in_scr[0, pl.ds(0, t_blk), :] = cq_ref[0]
        in_scr[1, pl.ds(0, t_blk), :] = cf_ref[0]
        in_scr[2, pl.ds(0, t_blk), :] = cig_ref[0]

    gens = [_hgrn_tile(n * tc, t_blk, tc, nl, cq_ref, cf_ref, cig_ref, lb_ref, ng_ref, sel_ref, lvl_ref, e3_ref,
                       y_ref, in_scr, st_scr) for n in range(n_sub)]
    live = list(gens)
    while live:
        for g in list(live):
            try:
                next(g)
            except StopIteration:
                live.remove(g)

    @pl.when(ti == n_t - 1)
    def _():
        for h in range(N_HEADS):
            sn_ref[0, h] = st_scr[h * HEAD_W:(h + 1) * HEAD_W, h * C_EXPAND:(h + 1) * C_EXPAND]


def _hgrn_tile(r0, t_blk, tc, nl, cq_ref, cf_ref, cig_ref, lb_ref, ng_ref, sel_ref, lvl_ref, e3_ref, y_ref,
               in_scr, st_scr):
    if t_blk != tc:
        q, f, cig = in_scr[0], in_scr[1], in_scr[2]
    else:
        q, f, cig = cq_ref[0, r0:r0 + tc, :], cf_ref[0, r0:r0 + tc, :], cig_ref[0, r0:r0 + tc, :]
    lb = lb_ref[0]
    fg = lb + (1.0 - lb) * jax.nn.sigmoid(f)
    log_f = jnp.log(fg)
    kk = 1.0 - fg
    fg_live = fg
    if t_blk != tc:
        row = lax.broadcasted_iota(jnp.int32, (tc, 1), 0)
        live = row < t_blk
        log_f = jnp.where(live, log_f, 0.0)
        kk = jnp.where(live, kk, 0.0)
        fg_live = jnp.where(live, fg, 1.0)
    hi = log_f.astype(BF16)
    lo = (log_f - hi.astype(F32)).astype(BF16)
    parts = jnp.concatenate([hi, lo], axis=1)
    yield
    d = _dot(sel_ref[...], parts)
    cum = d[:, 0:C_FDIM] + d[:, C_FDIM:2 * C_FDIM]
    odd = (lax.broadcasted_iota(jnp.int32, (tc, 1), 0) & 1) == 1

    lvl = lvl_ref[...]
    att = [None] * N_HEADS
    for l in range(nl + 1):
        if l == 0:
            e = jnp.where(odd, fg_live, 1.0)
        elif l < nl:
            m = 1 << l
            ref_rows = cum.reshape(tc // (2 * m), 2 * m, C_FDIM)[:, m - 1:m, :]
            ref_rows = jnp.broadcast_to(ref_rows, (tc // (2 * m), 2 * m, C_FDIM)).reshape(tc, C_FDIM)
            e = jnp.exp(-jnp.abs(cum - ref_rows))
        if l < nl:
            qs = (q * e).astype(BF16)
            ks = (kk * e).astype(BF16)
        else:
            qs = q.astype(BF16)
            ks = kk.astype(BF16)
        hit = lvl == l
        yield
        for h in range(N_HEADS):
            kl = slice(h * C_EXPAND, (h + 1) * C_EXPAND)
            a = _dot_nt(qs[:, kl], ks[:, kl])
            att[h] = jnp.where(hit, a, 0.0 if att[h] is None else att[h])
    c_last = cum[tc - 1:tc, :]
    lane_head = lax.broadcasted_iota(jnp.int32, (tc, BRANCH_W), 1) >> HEAD_SHIFT
    v = cig[:, 0:BRANCH_W]
    v_stack = _stack_heads(v, lane_head).astype(BF16)
    p_cat = jnp.concatenate([a.astype(BF16) for a in att], axis=1)
    q_dec = (q * jnp.exp(cum)).astype(BF16)
    k_tail = (kk * jnp.exp(c_last - cum)).astype(BF16)
    yield
    o_intra = _dot(p_cat, v_stack)
    yield
    st = st_scr[...]
    o = o_intra + _dot_nt(q_dec, st.astype(BF16))
    upd = _dot_tn(v.astype(BF16), k_tail)
    rh = lax.broadcasted_iota(jnp.int32, (BRANCH_W, C_FDIM), 0) >> HEAD_SHIFT
    ch_ = lax.broadcasted_iota(jnp.int32, (BRANCH_W, C_FDIM), 1) >> C_EXPAND_SHIFT
    st_scr[...] = st * jnp.exp(c_last) + jnp.where(rh == ch_, upd, 0.0)
    yield
    ms = _head_sum(o * o, e3_ref[...]) * (1.0 / HEAD_W)
    y = o * lax.rsqrt(ms + NORM_EPS) * ng_ref[0] * jax.nn.silu(cig[:, BRANCH_W:2 * BRANCH_W])
    y_ref[0, r0:r0 + t_blk, :] = y[0:t_blk]


def _hgrn(cols3, s0_t, P, layer):
    b, t, _ = cols3.shape
    e3 = P["e3"]
    tc = 256 if t >= 256 else 16
    assert t % tc == 0 or t < tc
    t_blk = min(t, tc)
    n_sub = HGRN_TILES if (t_blk == tc and (t // tc) % HGRN_TILES == 0) else 1
    n_t = max(1, t // (n_sub * tc))
    sel, lvl, nl = _hgrn_consts(tc)
    body = functools.partial(_hgrn_body, t, tc, n_sub, n_t, nl)
    blk = lambda idx: pl.BlockSpec((1, n_sub * t_blk, C_FDIM), lambda i, j: (i, j, idx))
    pad_shape = (3, tc, C_FDIM) if t_blk != tc else (1, 8, 128)
    return pl.pallas_call(
        body,
        out_shape=(jax.ShapeDtypeStruct((b, t, BRANCH_W), F32),
                   jax.ShapeDtypeStruct((b, N_HEADS, HEAD_W, C_EXPAND), F32)),
        grid=(b, n_t),
        in_specs=[blk(3), blk(4), blk(5),
                  pl.BlockSpec((1, 1, N_HEADS, HEAD_W, C_EXPAND), lambda i, j: (layer, i, 0, 0, 0)),
                  _layer_block(P["c_lb"].shape, layer),
                  _layer_block(P["c_norm_g"].shape, layer),
                  _resident(sel.shape), _resident(lvl.shape), _resident(e3.shape)],
        out_specs=(pl.BlockSpec((1, n_sub * t_blk, BRANCH_W), lambda i, j: (i, j, 0)),
                   pl.BlockSpec((1, N_HEADS, HEAD_W, C_EXPAND), lambda i, j: (i, 0, 0, 0))),
        scratch_shapes=[pltpu.VMEM(pad_shape, F32),
                        pltpu.VMEM((BRANCH_W, C_FDIM), F32)],
        compiler_params=_params(("parallel", "arbitrary")),
        name="hgrn2",
    )(cols3, cols3, cols3, s0_t, P["c_lb"], P["c_norm_g"], sel, lvl, e3)


def _rope_rows(x, cos, sin_signed):
    lane = lax.broadcasted_iota(jnp.int32, x.shape, 1)
    swapped = jnp.where((lane & (HEAD_W - 1)) < HEAD_W // 2,
                        pltpu.roll(x, BRANCH_W - HEAD_W // 2, 1), pltpu.roll(x, HEAD_W // 2, 1))
    return x * cos + swapped * sin_signed


def _pattern_weight(d):
    ok = d >= 0
    w = ((d <= 128).astype(F32) + (((d & 3) == 0) & (d <= 512)).astype(F32)
         + (((d & 15) == 0) & (d <= 2048)).astype(F32))
    return jnp.where(ok, w, 0.0)


def _att_group(n_blk):
    return next(g for g in (4, 2, 1) if n_blk % g == 0)


def _att_weight_table(n_blk):
    grp = _att_group(n_blk)
    r = np.arange(ATT_BLK)
    delta = np.arange(-(grp - 1), n_blk)
    d = delta[:, None, None] * ATT_BLK + r[None, None, :] - r[None, :, None]
    w = ((d <= 128).astype(np.float32) + ((d % 4 == 0) & (d <= 512)) + ((d % 16 == 0) & (d <= 2048)))
    return jnp.asarray(np.where(d >= 0, w, 0.0), F32)


def _attn_prompt_body(n_blk, has_prev, qkv_ref, cos_ref, sin_ref, wt_ref, *rest):
    y_ref, kt_ref, vt_ref, qt_scr, kb_scr, vt_scr, s_scr = rest[2:] if has_prev else rest
    qkv = qkv_ref[0]
    cos = cos_ref[...]
    sin = sin_ref[...]
    q = _rope_rows(qkv[:, 0:BRANCH_W], cos, sin) * (HEAD_W ** -0.5)
    k = _rope_rows(qkv[:, BRANCH_W:2 * BRANCH_W], cos, sin)
    v_t = qkv[:, 2 * BRANCH_W:3 * BRANCH_W].T
    kb_scr[...] = k.astype(BF16)
    qt_scr[...] = q.T.astype(BF16)
    vt_scr[...] = v_t.astype(BF16)
    kt_ref[0, 0] = k.T
    vt_ref[0, 0] = v_t
    n_lane = N_HEADS * ATT_BLK
    row_head = lax.broadcasted_iota(jnp.int32, (BRANCH_W, n_lane), 0) >> HEAD_SHIFT
    col_head = lax.broadcasted_iota(jnp.int32, (BRANCH_W, n_lane), 1) >> 7
    own_head = row_head == col_head
    grp = _att_group(n_blk)

    def q_block(i, _):
        q0 = pl.multiple_of(i * ATT_BLK, ATT_BLK)
        qt = qt_scr[:, pl.ds(q0, ATT_BLK)]
        q_bd = jnp.where(own_head, jnp.concatenate([qt] * N_HEADS, axis=1), jnp.zeros((), BF16))

        def weights(j):
            w1 = wt_ref[i - j + (grp - 1)]
            return jnp.concatenate([w1] * N_HEADS, axis=1)

        n_grp = (i + grp) // grp

        def scores(gi):
            g_max = None
            for n in range(grp):
                j = gi * grp + n
                s = _dot(kb_scr[pl.ds(pl.multiple_of(j * ATT_BLK, ATT_BLK), ATT_BLK), :], q_bd)
                s = jnp.where(weights(j) > 0.0, s, NEG_BIG)
                s_scr[gi & 1, n] = s
                g_max = s if g_max is None else jnp.maximum(g_max, s)
            return jnp.max(g_max, axis=0, keepdims=True)

        def update(gi, g_max, m_run, l_run, acc):
            m_new = jnp.maximum(m_run, g_max)
            alpha = jnp.exp(m_run - m_new)
            l_new = alpha * l_run
            pv = [None] * N_HEADS
            for n in range(grp):
                j = gi * grp + n
                p = weights(j) * jnp.exp(s_scr[gi & 1, n] - m_new)
                l_new = l_new + jnp.sum(p, axis=0, keepdims=True)
                pb = p.astype(BF16)
                for h in range(N_HEADS):
                    vt = vt_scr[h * HEAD_W:(h + 1) * HEAD_W, pl.ds(pl.multiple_of(j * ATT_BLK, ATT_BLK), ATT_BLK)]
                    term = _dot(vt, pb[:, h * ATT_BLK:(h + 1) * ATT_BLK])
                    pv[h] = term if pv[h] is None else pv[h] + term
            heads = [alpha[:, h * ATT_BLK:(h + 1) * ATT_BLK] * acc[h * HEAD_W:(h + 1) * HEAD_W, :] + pv[h]
                     for h in range(N_HEADS)]
            return m_new, l_new, jnp.concatenate(heads, axis=0)

        def step(gi, carry):
            g_max, m_run, l_run, acc = carry
            m_new = jnp.maximum(m_run, g_max)
            alpha = jnp.exp(m_run - m_new)
            l_new = alpha * l_run
            pv = [None] * N_HEADS
            nxt_max = None
            for n in range(grp):
                j = gi * grp + n
                jn = j + grp
                s_nxt = _dot(kb_scr[pl.ds(pl.multiple_of(jn * ATT_BLK, ATT_BLK), ATT_BLK), :], q_bd)
                p = weights(j) * jnp.exp(s_scr[gi & 1, n] - m_new)
                l_new = l_new + jnp.sum(p, axis=0, keepdims=True)
                pb = p.astype(BF16)
                for h in range(N_HEADS):
                    vt = vt_scr[h * HEAD_W:(h + 1) * HEAD_W, pl.ds(pl.multiple_of(j * ATT_BLK, ATT_BLK), ATT_BLK)]
                    term = _dot(vt, pb[:, h * ATT_BLK:(h + 1) * ATT_BLK])
                    pv[h] = term if pv[h] is None else pv[h] + term
                s_nxt = jnp.where(weights(jn) > 0.0, s_nxt, NEG_BIG)
                s_scr[(gi + 1) & 1, n] = s_nxt
                nxt_max = s_nxt if nxt_max is None else jnp.maximum(nxt_max, s_nxt)
            heads = [alpha[:, h * ATT_BLK:(h + 1) * ATT_BLK] * acc[h * HEAD_W:(h + 1) * HEAD_W, :] + pv[h]
                     for h in range(N_HEADS)]
            return (jnp.max(nxt_max, axis=0, keepdims=True), m_new, l_new, jnp.concatenate(heads, axis=0))

        init = (scores(0), jnp.full((1, n_lane), NEG_BIG, F32), jnp.zeros((1, n_lane), F32),
                jnp.zeros((BRANCH_W, ATT_BLK), F32))
        g_max, m_run, l_run, acc = lax.fori_loop(0, n_grp - 1, step, init)
        _, l_run, acc = update(n_grp - 1, g_max, m_run, l_run, acc)
        out_t = jnp.concatenate([acc[h * HEAD_W:(h + 1) * HEAD_W, :] / l_run[:, h * ATT_BLK:(h + 1) * ATT_BLK]
                                 for h in range(N_HEADS)], axis=0)
        y_ref[0, pl.ds(q0, ATT_BLK), :] = out_t.T
        return 0

    lax.fori_loop(0, n_blk, q_block, 0)


def _attn_prompt(cols3, cos, sin, layer, depth, kv_prev):
    b, t, _ = cols3.shape
    n_blk = t // ATT_BLK
    assert n_blk * ATT_BLK == t
    has_prev = kv_prev is not None
    body = functools.partial(_attn_prompt_body, n_blk, has_prev)
    wtab = _att_weight_table(n_blk)
    kv_shape = jax.ShapeDtypeStruct((depth, b, BRANCH_W, t), F32)
    kv_spec = pl.BlockSpec((1, 1, BRANCH_W, t), lambda i: (layer, i, 0, 0))
    prev_in = list(kv_prev) if has_prev else []
    return pl.pallas_call(
        body,
        out_shape=(jax.ShapeDtypeStruct((b, t, BRANCH_W), F32), kv_shape, kv_shape),
        grid=(b,),
        in_specs=[pl.BlockSpec((1, t, 3 * BRANCH_W), lambda i: (i, 0, 4)),
                  _resident((t, BRANCH_W)), _resident((t, BRANCH_W)), _resident(wtab.shape)]
                 + [pl.BlockSpec(memory_space=pl.ANY)] * len(prev_in),
        out_specs=(pl.BlockSpec((1, t, BRANCH_W), lambda i: (i, 0, 0)), kv_spec, kv_spec),
        input_output_aliases={4: 1, 5: 2} if has_prev else {},
        scratch_shapes=[pltpu.VMEM((BRANCH_W, t), BF16),
                        pltpu.VMEM((t, BRANCH_W), BF16),
                        pltpu.VMEM((BRANCH_W, t), BF16),
                        pltpu.VMEM((2, _att_group(n_blk), ATT_BLK, N_HEADS * ATT_BLK), F32)],
        compiler_params=_params(("parallel",)),
        name="attn_prompt",
    )(cols3, cos, sin, wtab, *prev_in)


def _attn_sample_body(t_len, l_cache, qkv_ref, kc_ref, vc_ref, cos_ref, sin_ref, y_ref, k_ref, pad_scr):
    tp = 8
    pad_scr[...] = jnp.zeros(pad_scr.shape, F32)
    pad_scr[pl.ds(0, t_len), :] = qkv_ref[0]
    qkv = pad_scr[...]
    cos = cos_ref[...]
    sin = sin_ref[...]
    q = _rope_rows(qkv[:, 0:BRANCH_W], cos, sin) * (HEAD_W ** -0.5)
    k_new = _rope_rows(qkv[:, BRANCH_W:2 * BRANCH_W], cos, sin)
    v_new = qkv[:, 2 * BRANCH_W:3 * BRANCH_W]
    k_ref[0] = k_new[0:t_len]
    lane_head = lax.broadcasted_iota(jnp.int32, (tp, BRANCH_W), 1) >> HEAD_SHIFT
    q_bd = jnp.concatenate([jnp.where(lane_head == h, q, 0.0) for h in range(N_HEADS)], axis=0).astype(BF16)
    n_row = N_HEADS * tp
    k_t = kc_ref[0, 0].reshape(BRANCH_W, l_cache)
    v_t = vc_ref[0, 0].reshape(BRANCH_W, l_cache)
    s_c = _dot(q_bd, k_t.astype(BF16))
    s_n = _dot_nt(q_bd, k_new.astype(BF16))
    t_row = lax.broadcasted_iota(jnp.int32, (n_row, 1), 0) & (tp - 1)
    col_c = lax.broadcasted_iota(jnp.int32, (n_row, l_cache), 1)
    w_c = _pattern_weight(l_cache + t_row - col_c)
    col_n = lax.broadcasted_iota(jnp.int32, (n_row, tp), 1)
    w_n = jnp.where(col_n < t_len, _pattern_weight(t_row - col_n), 0.0)
    s_c = jnp.where(w_c > 0.0, s_c, NEG_BIG)
    s_n = jnp.where(w_n > 0.0, s_n, NEG_BIG)
    m = jnp.maximum(jnp.max(s_c, axis=-1, keepdims=True), jnp.max(s_n, axis=-1, keepdims=True))
    p_c = w_c * jnp.exp(s_c - m)
    p_n = w_n * jnp.exp(s_n - m)
    l = jnp.sum(p_c, axis=-1, keepdims=True) + jnp.sum(p_n, axis=-1, keepdims=True)
    num = _dot_nt(p_c.astype(BF16), v_t.astype(BF16)) + _dot(p_n.astype(BF16), v_new.astype(BF16))
    ratio = num / l
    out = jnp.zeros((tp, BRANCH_W), F32)
    for h in range(N_HEADS):
        out = jnp.where(lane_head == h, ratio[h * tp:(h + 1) * tp, :], out)
    y_ref[0] = out[0:t_len]


def _attn_sample(cols3, k_cache_t, v_cache_t, layer, cos, sin):
    b, t, _ = cols3.shape
    l_cache = k_cache_t.shape[-1]
    assert t <= 8
    body = functools.partial(_attn_sample_body, t, l_cache)
    cache_spec = pl.BlockSpec((1, 1, N_HEADS, HEAD_W, l_cache), lambda i: (layer, i, 0, 0, 0))
    return pl.pallas_call(
        body,
        out_shape=(jax.ShapeDtypeStruct((b, t, BRANCH_W), F32), jax.ShapeDtypeStruct((b, t, BRANCH_W), F32)),
        grid=(b,),
        in_specs=[pl.BlockSpec((1, t, 3 * BRANCH_W), lambda i: (i, 0, 4)),
                  cache_spec, cache_spec,
                  pl.BlockSpec((8, BRANCH_W), lambda i: (0, 0)),
                  pl.BlockSpec((8, BRANCH_W), lambda i: (0, 0))],
        out_specs=(pl.BlockSpec((1, t, BRANCH_W), lambda i: (i, 0, 0)),
                   pl.BlockSpec((1, t, BRANCH_W), lambda i: (i, 0, 0))),
        scratch_shapes=[pltpu.VMEM((8, 3 * BRANCH_W), F32)],
        compiler_params=_params(("parallel",)),
        name="attn_sample",
    )(cols3, k_cache_t, v_cache_t, cos, sin)


def _merge_body(x_ref, g_ref, win_ref, ya_ref, yb_ref, yc_ref, yd_ref, wb_ref, wo_ref, o_ref):
    x = x_ref[...]
    u = _rms_rows(x, g_ref[0]).astype(BF16)
    merged = None
    for n, y_ref in enumerate((ya_ref, yb_ref, yc_ref, yd_ref)):
        z = _dot(y_ref[...].astype(BF16), wb_ref[0, n])
        gate_cols = slice(N_MIX_COLS + n * D_MODEL, N_MIX_COLS + (n + 1) * D_MODEL)
        gate = jax.nn.sigmoid(_dot(u, win_ref[0, :, gate_cols]))
        merged = gate * z if merged is None else merged + gate * z
    o_ref[...] = x + _dot(merged.astype(BF16), wo_ref[0])


def _merge(x2d, ys, P, layer):
    n, d = x2d.shape
    tm = min(n, 512)
    row = lambda w: pl.BlockSpec((tm, w), lambda i: (i, 0))
    return pl.pallas_call(
        _merge_body,
        out_shape=jax.ShapeDtypeStruct((n, d), F32),
        grid=(n // tm,),
        in_specs=[row(d), _layer_block(P["norm1_g"].shape, layer), _layer_block(P["w_in"].shape, layer),
                  row(BRANCH_W), row(BRANCH_W), row(BRANCH_W), row(BRANCH_W),
                  _layer_block(P["w_branch"].shape, layer), _layer_block(P["w_out"].shape, layer)],
        out_specs=row(d),
        compiler_params=_params(("parallel",)),
        name="merge",
    )(x2d, P["norm1_g"], P["w_in"], *ys, P["w_branch"], P["w_out"])


FFN_COL = 1024


def _ffn_columns(x, v, prev1, prev2, first, second, wg_ref, wu_ref, wd_ref, cw_ref, cb_ref, keep_gate):
    acc = jnp.zeros(x.shape, F32)
    cw = cw_ref[0]
    for c in range(D_FF // FFN_COL):
        sl = slice(c * FFN_COL, (c + 1) * FFN_COL)
        hg = _dot(v, wg_ref[0, :, sl])
        hu = _dot(v, wu_ref[0, :, sl])
        h1 = jnp.where(first, prev1(sl), pltpu.roll(hg, 1, 0))
        h2 = jnp.where(first | second, prev2(sl), pltpu.roll(hg, 2, 0))
        conv = cb_ref[0, :, sl] + cw[2:3, sl] * hg + cw[1:2, sl] * h1 + cw[0:1, sl] * h2
        hmid = (jax.nn.gelu(conv) * hu).astype(BF16)
        acc = acc + _dot(hmid, wd_ref[0, sl, :])
        keep_gate(sl, hg)
    return x + acc


def _ffn_body(tm, n_t, has_final, x_ref, g_ref, wg_ref, wu_ref, wd_ref, cw_ref, cb_ref, st_ref, *rest):
    if has_final:
        fg_ref, o_ref, ns_ref, carry_scr = rest
    else:
        o_ref, ns_ref, carry_scr = rest
    ti = pl.program_id(1)

    @pl.when(ti == 0)
    def _():
        carry_scr[...] = st_ref[0, 0]

    x = x_ref[0]
    v = _rms_rows(x, g_ref[0]).astype(BF16)
    row = lax.broadcasted_iota(jnp.int32, (tm, 1), 0)

    def prev2(sl):
        return jnp.where(row == 0, carry_scr[0:1, sl], carry_scr[1:2, sl])

    def keep_gate(sl, hg):
        carry_scr[:, sl] = hg[tm - 2:tm, :]

    y = _ffn_columns(x, v, lambda sl: carry_scr[1:2, sl], prev2, row == 0, row == 1,
                     wg_ref, wu_ref, wd_ref, cw_ref, cb_ref, keep_gate)
    o_ref[0] = _rms_rows(y, fg_ref[...]) if has_final else y

    @pl.when(ti == n_t - 1)
    def _():
        ns_ref[0] = carry_scr[...]


def _ffn_rows_body(seq, has_final, x_ref, g_ref, wg_ref, wu_ref, wd_ref, cw_ref, cb_ref, p1_ref, p2_ref, *rest):
    if has_final:
        fg_ref, o_ref, hg_ref = rest
    else:
        o_ref, hg_ref = rest
    x = x_ref[...]
    v = _rms_rows(x, g_ref[0]).astype(BF16)
    step = lax.broadcasted_iota(jnp.int32, (x.shape[0], 1), 0) & (seq - 1)

    def keep_gate(sl, hg):
        hg_ref[:, sl] = hg

    y = _ffn_columns(x, v, lambda sl: p1_ref[:, sl], lambda sl: p2_ref[:, sl], step == 0, step == 1,
                     wg_ref, wu_ref, wd_ref, cw_ref, cb_ref, keep_gate)
    o_ref[...] = _rms_rows(y, fg_ref[...]) if has_final else y


def _ffn(x3, state, P, layer, final_g):
    b, t, d = x3.shape
    has_final = final_g is not None
    extra_in = [final_g] if has_final else []
    extra_spec = [_resident((1, d))] if has_final else []
    names = ("norm2_g", "ffn_w_gate", "ffn_w_up", "ffn_w_down", "ffn_conv_w", "ffn_conv_b")
    weights = [P[n] for n in names]
    weight_specs = [_layer_block(w.shape, layer) for w in weights]
    if t >= 8:
        tm = min(t, 512)
        n_t = t // tm
        assert tm * n_t == t and tm % 8 == 0
        body = functools.partial(_ffn_body, tm, n_t, has_final)
        return pl.pallas_call(
            body,
            out_shape=(jax.ShapeDtypeStruct((b, t, d), F32), jax.ShapeDtypeStruct((b, 2, D_FF), F32)),
            grid=(b, n_t),
            in_specs=[pl.BlockSpec((1, tm, d), lambda i, j: (i, j, 0))] + weight_specs
                     + [pl.BlockSpec((1, 1, 2, D_FF), lambda i, j: (layer, i, 0, 0))] + extra_spec,
            out_specs=(pl.BlockSpec((1, tm, d), lambda i, j: (i, j, 0)),
                       pl.BlockSpec((1, 2, D_FF), lambda i, j: (i, 0, 0))),
            scratch_shapes=[pltpu.VMEM((2, D_FF), F32)],
            compiler_params=_params(("parallel", "arbitrary")),
            name="convffn",
        )(x3, *weights, state, *extra_in)
    assert t >= 2 and t & (t - 1) == 0 and (b * t) % 8 == 0
    n = b * t
    st = state[layer]
    zero = jnp.zeros((b, t - 1, D_FF), F32)
    p1 = jnp.concatenate([st[:, 1:2], zero], axis=1).reshape(n, D_FF)
    p2 = jnp.concatenate([st, zero[:, 1:]], axis=1).reshape(n, D_FF)
    body = functools.partial(_ffn_rows_body, t, has_final)
    full = lambda shape: pl.BlockSpec(shape, lambda i: (0,) * len(shape))
    y, hg = pl.pallas_call(
        body,
        out_shape=(jax.ShapeDtypeStruct((n, d), F32), jax.ShapeDtypeStruct((n, D_FF), F32)),
        grid=(1,),
        in_specs=[full((n, d))] + weight_specs + [full((n, D_FF)), full((n, D_FF))] + extra_spec,
        out_specs=(full((n, d)), full((n, D_FF))),
        compiler_params=_params(("arbitrary",)),
        name="convffn_rows",
    )(x3.reshape(n, d), *weights, p1, p2, *extra_in)
    return y.reshape(b, t, d), hg.reshape(b, t, D_FF)[:, t - 2:, :]


def _block_diag(w):
    eye = jnp.asarray(np.eye(N_HEADS, dtype=np.float32))
    return jnp.einsum("lhij,hg->lhigj", w, eye).reshape(w.shape[0], BRANCH_W, BRANCH_W)


def _head_sum_matrix3():
    i = np.arange(BRANCH_W) // HEAD_W
    e = (i[:, None] == i[None, :]).astype(np.float32)
    return jnp.asarray(np.concatenate([e, e, e], axis=0), BF16)


def _rope_tables(pos, rows):
    half = HEAD_W // 2
    inv = ROPE_THETA ** (-jnp.arange(half, dtype=F32) / half)
    ang = pos.astype(F32)[:, None] * inv[None, :]
    cos = jnp.cos(ang)
    sin = jnp.sin(ang)
    cos_t = jnp.tile(jnp.concatenate([cos, cos], axis=1), (1, N_HEADS))
    sin_t = jnp.tile(jnp.concatenate([-sin, sin], axis=1), (1, N_HEADS))
    pad = rows - pos.shape[0]
    if pad:
        cos_t = jnp.pad(cos_t, ((0, pad), (0, 0)))
        sin_t = jnp.pad(sin_t, ((0, pad), (0, 0)))
    return cos_t, sin_t


def _all_layer_params(W):
    row = lambda a: a.reshape(a.shape[0], 1, -1)
    depth = W["w_in"].shape[0]
    zpad = jnp.zeros((depth, 64, BRANCH_W), F32)
    lb_sm = jax.nn.softmax(W["c_lb"].astype(F32), axis=0)
    P = {
        "w_in": W["w_in"].astype(BF16),
        "a_gx_bd": _split_weight(_block_diag(W["a_gx_w"])),
        "a_ga_bd": _split_weight(_block_diag(W["a_ga_w"])),
        "b_w2p": _split_weight(jnp.concatenate([W["b_w2"], zpad], axis=1)),
        "b_a2p": _split_weight(jnp.concatenate([zpad, W["b_a2"]], axis=1)),
        "b_g2": _split_weight(W["b_g2"]),
        "c_lb": row(jnp.cumsum(lb_sm, axis=0) - lb_sm[0]),
        "c_norm_g": row(jnp.tile(W["c_norm_g"], (1, N_HEADS))),
        "w_branch": W["w_branch"].astype(BF16), "w_out": W["w_out"].astype(BF16),
        "ffn_w_gate": W["ffn_w_gate"].astype(BF16), "ffn_w_up": W["ffn_w_up"].astype(BF16),
        "ffn_w_down": W["ffn_w_down"].astype(BF16),
        "a_conv_w": W["a_conv_w"], "ffn_conv_w": W["ffn_conv_w"],
        "e3": _head_sum_matrix3(),
    }
    for n in ("norm1_g", "norm2_g", "a_conv_b", "a_gx_b", "a_ga_b", "a_lambda", "b_mu", "b_w0", "b_a0", "b_k_k",
              "b_k_a", "b_r_k", "b_ln_w", "b_ln_b", "ffn_conv_b"):
        P[n] = row(W[n])
    return P


def _trunk(x, pos0, st, W, P):
    b, t, d = x.shape
    depth = W["w_in"].shape[0]
    has_cache = st["k"] is not None
    if has_cache:
        cos, sin = _rope_tables(pos0 + jnp.arange(t), 8)
        k_cache_t = jnp.transpose(st["k"], (0, 1, 3, 4, 2))
        v_cache_t = jnp.transpose(st["v"], (0, 1, 3, 4, 2))
    else:
        cos, sin = _rope_tables(pos0 + jnp.arange(t), t)
    ha = st["ha"][:, :, None, :]
    sh = st["sh"][:, :, None, :]
    sc_t = jnp.transpose(st["sc"], (0, 1, 2, 4, 3))
    outs = {n: [] for n in ("ha", "ca", "wkv", "sh", "sc", "k", "v", "cf")}
    kv_prev = None
    for l in range(depth):
        cols = _inproj(x.reshape(b * t, d), P, l).reshape(b, t, N_MIX_COLS)
        y_a, h_new, ca_new = _rglru(cols, ha, st["ca"], P, l)
        y_b, sh_new, wkv_new = _rwkv(cols, sh, st["wkv"], P, l)
        y_c, sc_new = _hgrn(cols, sc_t, P, l)
        if has_cache:
            y_d, k_rows = _attn_sample(cols, k_cache_t, v_cache_t, l, cos, sin)
            k_out = k_rows.reshape(b, t, N_HEADS, HEAD_W)
            v_out = cols[:, :, N_MIX_COLS - BRANCH_W:].reshape(b, t, N_HEADS, HEAD_W)
        else:
            y_d, k_t, v_t = _attn_prompt(cols, cos, sin, l, depth, kv_prev)
            kv_prev = (k_t, v_t)
        ys = [y.reshape(b * t, BRANCH_W) for y in (y_a, y_b, y_c, y_d)]
        x1 = _merge(x.reshape(b * t, d), ys, P, l)
        final_g = W["final_norm_g"][None, :] if l == depth - 1 else None
        x, cf_new = _ffn(x1.reshape(b, t, d), st["cf"], P, l, final_g)
        outs["ha"].append(h_new[:, 0, :])
        outs["ca"].append(ca_new)
        outs["wkv"].append(wkv_new)
        outs["sh"].append(sh_new[:, 0, :])
        outs["sc"].append(sc_new)
        outs["cf"].append(cf_new)
        if has_cache:
            outs["k"].append(k_out)
            outs["v"].append(v_out)
    if not has_cache:
        del outs["k"], outs["v"]
    res = {n: jnp.stack(outs[n]) for n in outs}
    res["sc"] = jnp.transpose(res["sc"], (0, 1, 2, 4, 3))
    if not has_cache:
        keep = min(D_WIN_MAX, t)
        for n, buf in zip(("k", "v"), kv_prev):
            res[n] = jnp.transpose(buf.reshape(depth, b, N_HEADS, HEAD_W, t)[..., t - keep:], (0, 1, 4, 2, 3))
    return (x,) + tuple(res[n] for n in ("ha", "ca", "wkv", "sh", "sc", "k", "v", "cf"))


def kernel(x_prompt, x_sample, state_a_h, state_a_conv, state_b_wkv, state_b_shift, state_c_s, cache_d_k, cache_d_v, state_ffn_conv, norm1_g, w_in, a_conv_w, a_conv_b, a_gx_w, a_gx_b, a_ga_w, a_ga_b, a_lambda, b_mu, b_w0, b_w2, b_a0, b_a2, b_g2, b_k_k, b_k_a, b_r_k, b_ln_w, b_ln_b, c_lb, c_norm_g, w_branch, w_out, norm2_g, ffn_w_gate, ffn_w_up, ffn_conv_w, ffn_conv_b, ffn_w_down, final_norm_g):
    W = {"norm1_g": norm1_g, "w_in": w_in, "a_conv_w": a_conv_w, "a_conv_b": a_conv_b, "a_gx_w": a_gx_w,
         "a_gx_b": a_gx_b, "a_ga_w": a_ga_w, "a_ga_b": a_ga_b, "a_lambda": a_lambda, "b_mu": b_mu,
         "b_w0": b_w0, "b_w2": b_w2, "b_a0": b_a0, "b_a2": b_a2, "b_g2": b_g2, "b_k_k": b_k_k,
         "b_k_a": b_k_a, "b_r_k": b_r_k, "b_ln_w": b_ln_w, "b_ln_b": b_ln_b, "c_lb": c_lb,
         "c_norm_g": c_norm_g, "w_branch": w_branch, "w_out": w_out, "norm2_g": norm2_g,
         "ffn_w_gate": ffn_w_gate, "ffn_w_up": ffn_w_up, "ffn_conv_w": ffn_conv_w,
         "ffn_conv_b": ffn_conv_b, "ffn_w_down": ffn_w_down, "final_norm_g": final_norm_g}
    depth = w_in.shape[0]
    P = _all_layer_params(W)
    b_p, t_p = x_prompt.shape[:2]

    def zeros(*s):
        return jnp.zeros((depth, b_p) + s, F32)

    st_p = {"ha": zeros(BRANCH_W), "ca": zeros(3, BRANCH_W), "wkv": zeros(N_HEADS, HEAD_W, HEAD_W),
            "sh": zeros(B_COLS), "sc": zeros(N_HEADS, C_EXPAND, HEAD_W), "k": None, "v": None,
            "cf": zeros(2, D_FF)}
    out_p = _trunk(x_prompt, 0, st_p, W, P)
    st_s = {"ha": state_a_h, "ca": state_a_conv, "wkv": state_b_wkv, "sh": state_b_shift, "sc": state_c_s,
            "k": cache_d_k, "v": cache_d_v, "cf": state_ffn_conv}
    out_s = _trunk(x_sample, PAST_LEN, st_s, W, P)
    return (out_p[0], out_s[0]) + out_p[1:] + out_s[1:]
```

```python
import functools

import numpy as np

import jax
import jax.numpy as jnp
from jax import lax
from jax.experimental import pallas as pl
from jax.experimental.pallas import tpu as pltpu

F32 = jnp.float32
BF16 = jnp.bfloat16
HI = lax.Precision.HIGHEST

D_MODEL = 1024
BRANCH_W = 256
N_HEADS = 4
HEAD_W = 64
HEAD_SHIFT = 6
C_EXPAND = 128
C_EXPAND_SHIFT = 7
C_FDIM = 512
B_COLS = 1024
D_FF = 3072
NORM_EPS = 1e-6
A_C = 8.0
B_GN_EPS = 64e-5
ROPE_THETA = 10000.0
PAST_LEN = 8192
D_WIN_MAX = 2048
NEG_BIG = -1e30
N_MIX_COLS = 3840
ATT_BLK = 128

VMEM_LIMIT = 56 * 1024 * 1024


def _dot(a, b, prec=None):
    return jnp.dot(a, b, preferred_element_type=F32, precision=prec)


def _dot_nt(a, b, prec=None):
    return lax.dot_general(a, b, (((1,), (1,)), ((), ())), preferred_element_type=F32, precision=prec)


def _dot_tn(a, b, prec=None):
    return lax.dot_general(a, b, (((0,), (0,)), ((), ())), preferred_element_type=F32, precision=prec)


def _rms_rows(x, g):
    ms = jnp.mean(x * x, axis=-1, keepdims=True)
    return x * lax.rsqrt(ms + NORM_EPS) * g


def _round_up(n, m):
    return (n + m - 1) // m * m


def _params(sem):
    return pltpu.CompilerParams(dimension_semantics=sem, vmem_limit_bytes=VMEM_LIMIT)


def _resident(shape):
    nd = len(shape)
    return pl.BlockSpec(shape, lambda *_: (0,) * nd, pipeline_mode=pl.Buffered(1))


def _split3(x):
    hi = x.astype(BF16)
    r1 = x - hi.astype(F32)
    mid = r1.astype(BF16)
    lo = (r1 - mid.astype(F32)).astype(BF16)
    return hi, mid, lo


def _head_sum(x, e3):
    return _dot(jnp.concatenate(_split3(x), axis=1), e3)


def _dot_split(x, w3):
    hi = x.astype(BF16)
    lo = (x - hi.astype(F32)).astype(BF16)
    return _dot(jnp.concatenate([hi, lo, hi], axis=1), w3)


def _split_weight(w):
    hi = w.astype(BF16)
    lo = (w - hi.astype(F32)).astype(BF16)
    return jnp.concatenate([hi, hi, lo], axis=1)


def _stack_heads(x, lane_head):
    return jnp.concatenate([jnp.where(lane_head == h, x, jnp.zeros_like(x)) for h in range(N_HEADS)], axis=0)


def _layer_block(shape, layer):
    nd = len(shape)
    return pl.BlockSpec((1,) + tuple(shape[1:]), lambda *_: (layer,) + (0,) * (nd - 1),
                        pipeline_mode=pl.Buffered(1))


def _inproj_body(x_ref, g_ref, w_ref, o_ref):
    o_ref[...] = _dot(_rms_rows(x_ref[...], g_ref[0]).astype(BF16), w_ref[0])


def _inproj(x2d, P, layer):
    n, d = x2d.shape
    tm = min(n, 1024)
    return pl.pallas_call(
        _inproj_body,
        out_shape=jax.ShapeDtypeStruct((n, N_MIX_COLS), F32),
        grid=(n // tm,),
        in_specs=[pl.BlockSpec((tm, d), lambda i: (i, 0)),
                  _layer_block(P["norm1_g"].shape, layer),
                  pl.BlockSpec((1, d, N_MIX_COLS), lambda i: (layer, 0, 0), pipeline_mode=pl.Buffered(1))],
        out_specs=pl.BlockSpec((tm, N_MIX_COLS), lambda i: (i, 0)),
        compiler_params=_params(("parallel",)),
        name="inproj",
    )(x2d, P["norm1_g"], P["w_in"])


def _rglru_body(t_len, tp, axg_ref, h0_ref, cbuf_ref, cw_ref, cb_ref, wgx_ref, bgx_ref, wga_ref, bga_ref,
                lam_ref, y_ref, hl_ref, nb_ref, xs_scr, a_scr, b_scr):
    if tp != t_len:
        xs_scr[...] = jnp.zeros(xs_scr.shape, F32)
    xs_scr[pl.ds(5, 3), :] = jnp.concatenate([cbuf_ref[0, 0], jnp.zeros((3, BRANCH_W), F32)], axis=1)
    xs_scr[pl.ds(8, t_len), :] = axg_ref[0]
    x0 = xs_scr[pl.ds(8, tp), 0:BRANCH_W]
    x1 = xs_scr[pl.ds(7, tp), 0:BRANCH_W]
    x2 = xs_scr[pl.ds(6, tp), 0:BRANCH_W]
    x3 = xs_scr[pl.ds(5, tp), 0:BRANCH_W]
    cw = cw_ref[0]
    xc = cb_ref[0] + cw[3:4] * x0 + cw[2:3] * x1 + cw[1:2] * x2 + cw[0:1] * x3
    gate_x = jax.nn.sigmoid(_dot_split(xc, wgx_ref[0]) + bgx_ref[0])
    gate_a = jax.nn.sigmoid(_dot_split(xc, wga_ref[0]) + bga_ref[0])
    log_a = -A_C * gate_a * jax.nn.softplus(-lam_ref[0])
    a = jnp.exp(log_a)
    th = jnp.tanh(log_a)
    b_in = jnp.sqrt(-2.0 * th / (1.0 - th)) * (gate_x * xc)
    if tp != t_len:
        row = lax.broadcasted_iota(jnp.int32, (tp, 1), 0)
        a = jnp.where(row < t_len, a, 1.0)
        b_in = jnp.where(row < t_len, b_in, 0.0)
    a_scr[...] = a
    b_scr[...] = b_in

    row8 = lax.broadcasted_iota(jnp.int32, (8, 1), 0)

    def group(g, carry):
        r0 = pl.multiple_of(g * 8, 8)
        ag = a_scr[pl.ds(r0, 8), :]
        bg = b_scr[pl.ds(r0, 8), :]
        for s in (1, 2, 4):
            a_sh = jnp.where(row8 >= s, pltpu.roll(ag, s, 0), 1.0)
            b_sh = jnp.where(row8 >= s, pltpu.roll(bg, s, 0), 0.0)
            bg = ag * b_sh + bg
            ag = ag * a_sh
        h = ag * carry + bg
        b_scr[pl.ds(r0, 8), :] = h
        return h[7:8, :]

    h_last = lax.fori_loop(0, tp // 8, group, h0_ref[0, 0])
    h = b_scr[...]
    gate = xs_scr[pl.ds(8, tp), BRANCH_W:2 * BRANCH_W]
    y = h * jax.nn.gelu(gate)
    y_ref[0] = y[0:t_len]
    hl_ref[0] = h_last
    nb_ref[0] = xs_scr[pl.ds(8 + t_len - 3, 3), 0:BRANCH_W]


def _rglru(cols3, h0, cbuf, P, layer):
    b, t, _ = cols3.shape
    tp = _round_up(t, 8)
    body = functools.partial(_rglru_body, t, tp)
    names = ("a_conv_w", "a_conv_b", "a_gx_bd", "a_gx_b", "a_ga_bd", "a_ga_b", "a_lambda")
    return pl.pallas_call(
        body,
        out_shape=(jax.ShapeDtypeStruct((b, t, BRANCH_W), F32),
                   jax.ShapeDtypeStruct((b, 1, BRANCH_W), F32),
                   jax.ShapeDtypeStruct((b, 3, BRANCH_W), F32)),
        grid=(b,),
        in_specs=[pl.BlockSpec((1, t, 2 * BRANCH_W), lambda i: (i, 0, 0)),
                  pl.BlockSpec((1, 1, 1, BRANCH_W), lambda i: (layer, i, 0, 0)),
                  pl.BlockSpec((1, 1, 3, BRANCH_W), lambda i: (layer, i, 0, 0))]
                 + [_layer_block(P[n].shape, layer) for n in names],
        out_specs=(pl.BlockSpec((1, t, BRANCH_W), lambda i: (i, 0, 0)),
                   pl.BlockSpec((1, 1, BRANCH_W), lambda i: (i, 0, 0)),
                   pl.BlockSpec((1, 3, BRANCH_W), lambda i: (i, 0, 0))),
        scratch_shapes=[pltpu.VMEM((tp + 8, 2 * BRANCH_W), F32),
                        pltpu.VMEM((tp, BRANCH_W), F32),
                        pltpu.VMEM((tp, BRANCH_W), F32)],
        compiler_params=_params(("parallel",)),
        name="rglru",
    )(cols3, h0, cbuf, *[P[n] for n in names])


RWKV_GROUP = 4


def _rwkv_consts(tc, ch):
    t = np.arange(tc)
    same = (t[:, None] // ch) == (t[None, :] // ch)
    tril = same & (t[None, :] <= t[:, None])
    return jnp.asarray(np.concatenate([tril, same], axis=0).astype(np.float32), BF16)


def _rwkv_body(t_len, tc, n_t, ch, cb1_ref, cb2_ref, shift_ref, s0_ref, mu_ref, w0_ref, w2_ref, a0_ref, a2_ref,
               g2_ref, kk_ref, ka_ref, rk_ref, lnw_ref, lnb_ref, cs_ref, e3_ref, y_ref, last_ref, sn_ref,
               xs_scr, s_scr, at_scr, bt_scr, kt_scr, bh_scr, kh_scr, v_scr, rt_scr, gc_scr, y_scr, bon_scr,
               g_scr):
    ti = pl.program_id(1)
    t_blk = min(t_len, tc)
    cs = N_HEADS * ch
    half = B_COLS // 2

    @pl.when(ti == 0)
    def _():
        if t_blk != tc:
            xs_scr[...] = jnp.zeros(xs_scr.shape, F32)
        xs_scr[pl.ds(7, 1), :] = shift_ref[0, 0]
        s_scr[...] = jnp.zeros(s_scr.shape, F32)
        for h in range(N_HEADS):
            s_scr[h * HEAD_W:(h + 1) * HEAD_W, h * HEAD_W:(h + 1) * HEAD_W] = s0_ref[0, 0, h]

    xs_scr[pl.ds(8, t_blk), 0:half] = cb1_ref[0]
    xs_scr[pl.ds(8, t_blk), half:B_COLS] = cb2_ref[0]
    cb = xs_scr[pl.ds(8, tc), :]
    shifted = xs_scr[pl.ds(7, tc), :]
    cm = cb + (shifted - cb) * mu_ref[0]
    r = cm[:, 0:256]
    k = cm[:, 256:512]
    v = cm[:, 512:768]
    lora = cm[:, 768:896]
    w = -jax.nn.softplus(-(w0_ref[0] + _dot_split(jnp.tanh(lora), w2_ref[0]))) - 0.5
    log_w = -jnp.exp(w)
    a = jax.nn.sigmoid(a0_ref[0] + _dot_split(lora, a2_ref[0]))
    g_scr[...] = _dot_split(jax.nn.sigmoid(cm[:, 896:1024]), g2_ref[0])
    e3 = e3_ref[...]
    kk = k * kk_ref[0]
    kk = kk / jnp.maximum(jnp.sqrt(_head_sum(kk * kk, e3)), 1e-12)
    k2 = k * (1.0 + (a - 1.0) * ka_ref[0])
    bon_scr[...] = _head_sum(r * k2 * rk_ref[0], e3) * v
    a_s = -kk
    b_s = kk * a
    if t_blk != tc:
        row = lax.broadcasted_iota(jnp.int32, (tc, 1), 0)
        live = row < t_blk
        log_w = jnp.where(live, log_w, 0.0)
        a_s = jnp.where(live, a_s, 0.0)
        b_s = jnp.where(live, b_s, 0.0)
        k2 = jnp.where(live, k2, 0.0)
    parts = jnp.concatenate(_split3(log_w), axis=1)
    sub = cs_ref.shape[1]
    cum, tot = [], []
    for j in range(tc // sub):
        cc = _dot(cs_ref[...], parts[j * sub:(j + 1) * sub, :])
        cc = cc[:, 0:256] + cc[:, 256:512] + cc[:, 512:768]
        cum.append(cc[0:sub])
        tot.append(cc[sub:2 * sub])
    cum = jnp.concatenate(cum, axis=0)
    tot = jnp.concatenate(tot, axis=0)
    inv = jnp.exp(-cum)
    tail = jnp.exp(tot - cum)
    at_scr[...] = (a_s * jnp.exp(cum - log_w)).astype(BF16)
    bt_scr[...] = (b_s * inv).astype(BF16)
    kt_scr[...] = (k2 * inv).astype(BF16)
    bh_scr[...] = (b_s * tail).astype(BF16)
    kh_scr[...] = (k2 * tail).astype(BF16)
    v_scr[...] = v.astype(BF16)
    rt_scr[...] = r * jnp.exp(cum)
    gc_scr[...] = jnp.exp(tot)

    row_t = lax.broadcasted_iota(jnp.int32, (cs, cs), 0) & (ch - 1)
    col_t = lax.broadcasted_iota(jnp.int32, (cs, cs), 1) & (ch - 1)
    strict = col_t < row_t
    row_t2 = lax.broadcasted_iota(jnp.int32, (cs, 2 * cs), 0) & (ch - 1)
    col_t2 = lax.broadcasted_iota(jnp.int32, (cs, 2 * cs), 1) & (ch - 1)
    incl2 = col_t2 <= row_t2
    eye = (lax.broadcasted_iota(jnp.int32, (cs, cs), 0) == lax.broadcasted_iota(jnp.int32, (cs, cs), 1)).astype(F32)
    lane_head = lax.broadcasted_iota(jnp.int32, (ch, BRANCH_W), 1) >> HEAD_SHIFT
    n_dbl = ch.bit_length() - 1

    def prepare(c):
        r0 = c * ch if isinstance(c, int) else pl.multiple_of(c * ch, ch)
        abd, bbd, kbd, bhd, khd, vbd = (_stack_heads(s[pl.ds(r0, ch), :], lane_head)
                                        for s in (at_scr, bt_scr, kt_scr, bh_scr, kh_scr, v_scr))
        rbd = _stack_heads(rt_scr[pl.ds(r0, ch), :], lane_head)
        s1 = _dot_nt(jnp.concatenate([abd, rbd.astype(BF16)], axis=0), jnp.concatenate([bbd, kbd], axis=0))
        l_ab = jnp.where(strict, s1[0:cs, 0:cs], 0.0)
        l_ak = jnp.where(strict, s1[0:cs, cs:2 * cs], 0.0)
        m_r = jnp.where(incl2, s1[cs:2 * cs, :], 0.0).astype(BF16)
        tinv = eye + l_ab
        lb16 = l_ab.astype(BF16)
        yield
        pw = _dot(lb16, lb16)
        for i in range(1, n_dbl):
            pw16 = pw.astype(BF16)
            yield
            if i < n_dbl - 1:
                both = _dot(jnp.concatenate([tinv.astype(BF16), pw16], axis=0), pw16)
                tinv = tinv + both[0:cs]
                pw = both[cs:2 * cs]
            else:
                tinv = tinv + _dot(tinv.astype(BF16), pw16)
        yield
        w1 = _dot(l_ak.astype(BF16), vbd)
        yield
        x = _dot(tinv.astype(BF16), jnp.concatenate([abd, w1.astype(BF16)], axis=1))
        yield
        z = _dot(m_r[:, 0:cs], x.astype(BF16))
        r_hat = rbd + z[:, 0:BRANCH_W]
        yield
        y_hat = z[:, BRANCH_W:2 * BRANCH_W] + _dot(m_r[:, cs:2 * cs], vbd)
        ar = jnp.concatenate([x[:, 0:BRANCH_W], r_hat], axis=0).astype(BF16)
        g_end = gc_scr[pl.ds(r0, ch), :][0:1, :]
        return ar, x[:, BRANCH_W:2 * BRANCH_W], y_hat, vbd, jnp.concatenate([bhd, khd], axis=0), g_end

    def advance(c0, prepared):
        for n, (ar, u_hat, y_hat, vbd, bk, g_end) in enumerate(prepared):
            c = c0 + n
            r0 = c * ch if isinstance(c, int) else pl.multiple_of(c * ch, ch)
            sb = s_scr[...]
            uy = _dot_nt(ar, sb.astype(BF16))
            u = uy[0:cs] + u_hat
            ys = uy[cs:2 * cs] + y_hat
            y_scr[pl.ds(r0, ch), :] = ys[0:ch] + ys[ch:2 * ch] + ys[2 * ch:3 * ch] + ys[3 * ch:4 * ch]
            yield
            s_scr[...] = sb * g_end + _dot_tn(jnp.concatenate([u.astype(BF16), vbd], axis=0), bk)
            yield

    def interleave(gens):
        done = [None] * len(gens)
        live = list(range(len(gens)))
        while live:
            for i in list(live):
                try:
                    next(gens[i])
                except StopIteration as stop:
                    done[i] = stop.value
                    live.remove(i)
        return done

    n_chunk = tc // ch
    grp = RWKV_GROUP if n_chunk % RWKV_GROUP == 0 else 1
    first = tuple(interleave([prepare(c) for c in range(grp)]))
    if n_chunk > grp:
        def group(i, carry):
            c0 = i * grp
            res = interleave([prepare(c0 + grp + n) for n in range(grp)] + [advance(c0, carry)])
            return tuple(res[0:grp])

        first = lax.fori_loop(0, n_chunk // grp - 1, group, first)

    def finish(r0, rows, live_rows):
        y = y_scr[r0:r0 + rows, :]
        yield
        mean = _head_sum(y, e3) * (1.0 / HEAD_W)
        dev = y - mean
        yield
        var = _head_sum(dev * dev, e3) * (1.0 / HEAD_W)
        yn = dev * lax.rsqrt(var + B_GN_EPS) * lnw_ref[0] + lnb_ref[0]
        out = (yn + bon_scr[r0:r0 + rows, :]) * g_scr[r0:r0 + rows, :]
        y_ref[0, r0:r0 + live_rows, :] = out[0:live_rows]

    done_rows = (n_chunk - grp) * ch
    interleave([advance(n_chunk - grp, first)] + ([finish(0, done_rows, done_rows)] if done_rows else []))
    interleave([finish(done_rows, tc - done_rows, t_blk - done_rows)])
    xs_scr[pl.ds(7, 1), :] = xs_scr[pl.ds(8 + t_blk - 1, 1), :]

    @pl.when(ti == n_t - 1)
    def _():
        last_ref[0] = xs_scr[pl.ds(7, 1), :]
        for h in range(N_HEADS):
            sn_ref[0, h] = s_scr[h * HEAD_W:(h + 1) * HEAD_W, h * HEAD_W:(h + 1) * HEAD_W]


def _rwkv(cols3, shift, s0, P, layer):
    b, t, _ = cols3.shape
    if t >= 64:
        ch, tc = 64, min(t, 1024)
        assert t % tc == 0
    else:
        ch = 16
        tc = _round_up(t, ch)
    n_t = max(1, t // tc)
    t_blk = min(t, tc)
    body = functools.partial(_rwkv_body, t, tc, n_t, ch)
    consts = _rwkv_consts(min(tc, 256), ch)
    names = ("b_mu", "b_w0", "b_w2p", "b_a0", "b_a2p", "b_g2", "b_k_k", "b_k_a", "b_r_k", "b_ln_w", "b_ln_b")
    half = B_COLS // 2
    sc16 = lambda: pltpu.VMEM((tc, BRANCH_W), BF16)
    sc32 = lambda: pltpu.VMEM((tc, BRANCH_W), F32)
    return pl.pallas_call(
        body,
        out_shape=(jax.ShapeDtypeStruct((b, t, BRANCH_W), F32),
                   jax.ShapeDtypeStruct((b, 1, B_COLS), F32),
                   jax.ShapeDtypeStruct((b, N_HEADS, HEAD_W, HEAD_W), F32)),
        grid=(b, n_t),
        in_specs=[pl.BlockSpec((1, t_blk, half), lambda i, j: (i, j, 1)),
                  pl.BlockSpec((1, t_blk, half), lambda i, j: (i, j, 2)),
                  pl.BlockSpec((1, 1, 1, B_COLS), lambda i, j: (layer, i, 0, 0)),
                  pl.BlockSpec((1, 1, N_HEADS, HEAD_W, HEAD_W), lambda i, j: (layer, i, 0, 0, 0))]
                 + [_layer_block(P[n].shape, layer) for n in names]
                 + [_resident(consts.shape), _resident(P["e3"].shape)],
        out_specs=(pl.BlockSpec((1, t_blk, BRANCH_W), lambda i, j: (i, j, 0)),
                   pl.BlockSpec((1, 1, B_COLS), lambda i, j: (i, 0, 0)),
                   pl.BlockSpec((1, N_HEADS, HEAD_W, HEAD_W), lambda i, j: (i, 0, 0, 0))),
        scratch_shapes=[pltpu.VMEM((tc + 8, B_COLS), F32),
                        pltpu.VMEM((BRANCH_W, BRANCH_W), F32),
                        sc16(), sc16(), sc16(), sc16(), sc16(), sc16(),
                        sc32(), sc32(), sc32(), sc32(), sc32()],
        compiler_params=_params(("parallel", "arbitrary")),
        name="rwkv7",
    )(cols3, cols3, shift, s0, *[P[n] for n in names], consts, P["e3"])


HGRN_TILES = 2


def _hgrn_consts(tc):
    nl = tc.bit_length() - 1
    t = np.arange(tc)
    tril = (t[None, :] <= t[:, None]).astype(np.float32)
    x = t[:, None] ^ t[None, :]
    lvl = np.floor(np.log2(np.maximum(x, 1))).astype(np.int32)
    lvl = np.where(t[:, None] > t[None, :], lvl, np.where(t[:, None] == t[None, :], nl, -1))
    return jnp.asarray(tril, BF16), jnp.asarray(lvl, jnp.int32), nl


def _hgrn_body(t_len, tc, n_sub, n_t, nl, cq_ref, cf_ref, cig_ref, s0_ref, lb_ref, ng_ref, sel_ref, lvl_ref,
               e3_ref, y_ref, sn_ref, in_scr, st_scr):
    ti = pl.program_id(1)
    t_blk = min(t_len, tc)

    @pl.when(ti == 0)
    def _():
        st_scr[...] = jnp.zeros(st_scr.shape, F32)
        for h in range(N_HEADS):
            st_scr[h * HEAD_W:(h + 1) * HEAD_W, h * C_EXPAND:(h + 1) * C_EXPAND] = s0_ref[0, 0, h]

    if t_blk != tc:
        in_scr[...] = jnp.zeros(in_scr.shape, F32)
        in_scr[0, pl.ds(0, t_blk), :] = cq_ref[0]
        in_scr[1, pl.ds(0, t_blk), :] = cf_ref[0]
        in_scr[2, pl.ds(0, t_blk), :] = cig_ref[0]

    gens = [_hgrn_tile(n * tc, t_blk, tc, nl, cq_ref, cf_ref, cig_ref, lb_ref, ng_ref, sel_ref, lvl_ref, e3_ref,
                       y_ref, in_scr, st_scr) for n in range(n_sub)]
    live = list(gens)
    while live:
        for g in list(live):
            try:
                next(g)
            except StopIteration:
                live.remove(g)

    @pl.when(ti == n_t - 1)
    def _():
        for h in range(N_HEADS):
            sn_ref[0, h] = st_scr[h * HEAD_W:(h + 1) * HEAD_W, h * C_EXPAND:(h + 1) * C_EXPAND]


def _hgrn_tile(r0, t_blk, tc, nl, cq_ref, cf_ref, cig_ref, lb_ref, ng_ref, sel_ref, lvl_ref, e3_ref, y_ref,
               in_scr, st_scr):
    if t_blk != tc:
        q, f, cig = in_scr[0], in_scr[1], in_scr[2]
    else:
        q, f, cig = cq_ref[0, r0:r0 + tc, :], cf_ref[0, r0:r0 + tc, :], cig_ref[0, r0:r0 + tc, :]
    lb = lb_ref[0]
    fg = lb + (1.0 - lb) * jax.nn.sigmoid(f)
    log_f = jnp.log(fg)
    kk = 1.0 - fg
    fg_live = fg
    if t_blk != tc:
        row = lax.broadcasted_iota(jnp.int32, (tc, 1), 0)
        live = row < t_blk
        log_f = jnp.where(live, log_f, 0.0)
        kk = jnp.where(live, kk, 0.0)
        fg_live = jnp.where(live, fg, 1.0)
    hi = log_f.astype(BF16)
    lo = (log_f - hi.astype(F32)).astype(BF16)
    parts = jnp.concatenate([hi, lo], axis=1)
    yield
    d = _dot(sel_ref[...], parts)
    cum = d[:, 0:C_FDIM] + d[:, C_FDIM:2 * C_FDIM]
    odd = (lax.broadcasted_iota(jnp.int32, (tc, 1), 0) & 1) == 1

    lvl = lvl_ref[...]
    att = [None] * N_HEADS
    for l in range(nl + 1):
        if l == 0:
            e = jnp.where(odd, fg_live, 1.0)
        elif l < nl:
            m = 1 << l
            ref_rows = cum.reshape(tc // (2 * m), 2 * m, C_FDIM)[:, m - 1:m, :]
            ref_rows = jnp.broadcast_to(ref_rows, (tc // (2 * m), 2 * m, C_FDIM)).reshape(tc, C_FDIM)
            e = jnp.exp(-jnp.abs(cum - ref_rows))
        if l < nl:
            qs = (q * e).astype(BF16)
            ks = (kk * e).astype(BF16)
        else:
            qs = q.astype(BF16)
            ks = kk.astype(BF16)
        hit = lvl == l
        yield
        for h in range(N_HEADS):
            kl = slice(h * C_EXPAND, (h + 1) * C_EXPAND)
            a = _dot_nt(qs[:, kl], ks[:, kl])
            att[h] = jnp.where(hit, a, 0.0 if att[h] is None else att[h])
    c_last = cum[tc - 1:tc, :]
    lane_head = lax.broadcasted_iota(jnp.int32, (tc, BRANCH_W), 1) >> HEAD_SHIFT
    v = cig[:, 0:BRANCH_W]
    v_stack = _stack_heads(v, lane_head).astype(BF16)
    p_cat = jnp.concatenate([a.astype(BF16) for a in att], axis=1)
    q_dec = (q * jnp.exp(cum)).astype(BF16)
    k_tail = (kk * jnp.exp(c_last - cum)).astype(BF16)
    yield
    o_intra = _dot(p_cat, v_stack)
    yield
    st = st_scr[...]
    o = o_intra + _dot_nt(q_dec, st.astype(BF16))
    upd = _dot_tn(v.astype(BF16), k_tail)
    rh = lax.broadcasted_iota(jnp.int32, (BRANCH_W, C_FDIM), 0) >> HEAD_SHIFT
    ch_ = lax.broadcasted_iota(jnp.int32, (BRANCH_W, C_FDIM), 1) >> C_EXPAND_SHIFT
    st_scr[...] = st * jnp.exp(c_last) + jnp.where(rh == ch_, upd, 0.0)
    yield
    ms = _head_sum(o * o, e3_ref[...]) * (1.0 / HEAD_W)
    y = o * lax.rsqrt(ms + NORM_EPS) * ng_ref[0] * jax.nn.silu(cig[:, BRANCH_W:2 * BRANCH_W])
    y_ref[0, r0:r0 + t_blk, :] = y[0:t_blk]


def _hgrn(cols3, s0_t, P, layer):
    b, t, _ = cols3.shape
    e3 = P["e3"]
    tc = 256 if t >= 256 else 16
    assert t % tc == 0 or t < tc
    t_blk = min(t, tc)
    n_sub = HGRN_TILES if (t_blk == tc and (t // tc) % HGRN_TILES == 0) else 1
    n_t = max(1, t // (n_sub * tc))
    sel, lvl, nl = _hgrn_consts(tc)
    body = functools.partial(_hgrn_body, t, tc, n_sub, n_t, nl)
    blk = lambda idx: pl.BlockSpec((1, n_sub * t_blk, C_FDIM), lambda i, j: (i, j, idx))
    pad_shape = (3, tc, C_FDIM) if t_blk != tc else (1, 8, 128)
    return pl.pallas_call(
        body,
        out_shape=(jax.ShapeDtypeStruct((b, t, BRANCH_W), F32),
                   jax.ShapeDtypeStruct((b, N_HEADS, HEAD_W, C_EXPAND), F32)),
        grid=(b, n_t),
        in_specs=[blk(3), blk(4), blk(5),
                  pl.BlockSpec((1, 1, N_HEADS, HEAD_W, C_EXPAND), lambda i, j: (layer, i, 0, 0, 0)),
                  _layer_block(P["c_lb"].shape, layer),
                  _layer_block(P["c_norm_g"].shape, layer),
                  _resident(sel.shape), _resident(lvl.shape), _resident(e3.shape)],
        out_specs=(pl.BlockSpec((1, n_sub * t_blk, BRANCH_W), lambda i, j: (i, j, 0)),
                   pl.BlockSpec((1, N_HEADS, HEAD_W, C_EXPAND), lambda i, j: (i, 0, 0, 0))),
        scratch_shapes=[pltpu.VMEM(pad_shape, F32),
                        pltpu.VMEM((BRANCH_W, C_FDIM), F32)],
        compiler_params=_params(("parallel", "arbitrary")),
        name="hgrn2",
    )(cols3, cols3, cols3, s0_t, P["c_lb"], P["c_norm_g"], sel, lvl, e3)


def _rope_rows(x, cos, sin_signed):
    lane = lax.broadcasted_iota(jnp.int32, x.shape, 1)
    swapped = jnp.where((lane & (HEAD_W - 1)) < HEAD_W // 2,
                        pltpu.roll(x, BRANCH_W - HEAD_W // 2, 1), pltpu.roll(x, HEAD_W // 2, 1))
    return x * cos + swapped * sin_signed


def _pattern_weight(d):
    ok = d >= 0
    w = ((d <= 128).astype(F32) + (((d & 3) == 0) & (d <= 512)).astype(F32)
         + (((d & 15) == 0) & (d <= 2048)).astype(F32))
    return jnp.where(ok, w, 0.0)


def _att_group(n_blk):
    return next(g for g in (4, 2, 1) if n_blk % g == 0)


def _att_weight_table(n_blk):
    grp = _att_group(n_blk)
    r = np.arange(ATT_BLK)
    delta = np.arange(-(grp - 1), n_blk)
    d = delta[:, None, None] * ATT_BLK + r[None, None, :] - r[None, :, None]
    w = ((d <= 128).astype(np.float32) + ((d % 4 == 0) & (d <= 512)) + ((d % 16 == 0) & (d <= 2048)))
    return jnp.asarray(np.where(d >= 0, w, 0.0), F32)


def _attn_prompt_body(n_blk, has_prev, qkv_ref, cos_ref, sin_ref, wt_ref, *rest):
    y_ref, kt_ref, vt_ref, qt_scr, kb_scr, vt_scr, s_scr = rest[2:] if has_prev else rest
    qkv = qkv_ref[0]
    cos = cos_ref[...]
    sin = sin_ref[...]
    q = _rope_rows(qkv[:, 0:BRANCH_W], cos, sin) * (HEAD_W ** -0.5)
    k = _rope_rows(qkv[:, BRANCH_W:2 * BRANCH_W], cos, sin)
    v_t = qkv[:, 2 * BRANCH_W:3 * BRANCH_W].T
    kb_scr[...] = k.astype(BF16)
    qt_scr[...] = q.T.astype(BF16)
    vt_scr[...] = v_t.astype(BF16)
    kt_ref[0, 0] = k.T
    vt_ref[0, 0] = v_t
    n_lane = N_HEADS * ATT_BLK
    row_head = lax.broadcasted_iota(jnp.int32, (BRANCH_W, n_lane), 0) >> HEAD_SHIFT
    col_head = lax.broadcasted_iota(jnp.int32, (BRANCH_W, n_lane), 1) >> 7
    own_head = row_head == col_head
    grp = _att_group(n_blk)

    def q_block(i, _):
        q0 = pl.multiple_of(i * ATT_BLK, ATT_BLK)
        qt = qt_scr[:, pl.ds(q0, ATT_BLK)]
        q_bd = jnp.where(own_head, jnp.concatenate([qt] * N_HEADS, axis=1), jnp.zeros((), BF16))

        def weights(j):
            w1 = wt_ref[i - j + (grp - 1)]
            return jnp.concatenate([w1] * N_HEADS, axis=1)

        n_grp = (i + grp) // grp

        def scores(gi):
            g_max = None
            for n in range(grp):
                j = gi * grp + n
                s = _dot(kb_scr[pl.ds(pl.multiple_of(j * ATT_BLK, ATT_BLK), ATT_BLK), :], q_bd)
                s = jnp.where(weights(j) > 0.0, s, NEG_BIG)
                s_scr[gi & 1, n] = s
                g_max = s if g_max is None else jnp.maximum(g_max, s)
            return jnp.max(g_max, axis=0, keepdims=True)

        def update(gi, g_max, m_run, l_run, acc):
            m_new = jnp.maximum(m_run, g_max)
            alpha = jnp.exp(m_run - m_new)
            l_new = alpha * l_run
            pv = [None] * N_HEADS
            for n in range(grp):
                j = gi * grp + n
                p = weights(j) * jnp.exp(s_scr[gi & 1, n] - m_new)
                l_new = l_new + jnp.sum(p, axis=0, keepdims=True)
                pb = p.astype(BF16)
                for h in range(N_HEADS):
                    vt = vt_scr[h * HEAD_W:(h + 1) * HEAD_W, pl.ds(pl.multiple_of(j * ATT_BLK, ATT_BLK), ATT_BLK)]
                    term = _dot(vt, pb[:, h * ATT_BLK:(h + 1) * ATT_BLK])
                    pv[h] = term if pv[h] is None else pv[h] + term
            heads = [alpha[:, h * ATT_BLK:(h + 1) * ATT_BLK] * acc[h * HEAD_W:(h + 1) * HEAD_W, :] + pv[h]
                     for h in range(N_HEADS)]
            return m_new, l_new, jnp.concatenate(heads, axis=0)

        def step(gi, carry):
            g_max, m_run, l_run, acc = carry
            m_new = jnp.maximum(m_run, g_max)
            alpha = jnp.exp(m_run - m_new)
            l_new = alpha * l_run
            pv = [None] * N_HEADS
            nxt_max = None
            for n in range(grp):
                j = gi * grp + n
                jn = j + grp
                s_nxt = _dot(kb_scr[pl.ds(pl.multiple_of(jn * ATT_BLK, ATT_BLK), ATT_BLK), :], q_bd)
                p = weights(j) * jnp.exp(s_scr[gi & 1, n] - m_new)
                l_new = l_new + jnp.sum(p, axis=0, keepdims=True)
                pb = p.astype(BF16)
                for h in range(N_HEADS):
                    vt = vt_scr[h * HEAD_W:(h + 1) * HEAD_W, pl.ds(pl.multiple_of(j * ATT_BLK, ATT_BLK), ATT_BLK)]
                    term = _dot(vt, pb[:, h * ATT_BLK:(h + 1) * ATT_BLK])
                    pv[h] = term if pv[h] is None else pv[h] + term
                s_nxt = jnp.where(weights(jn) > 0.0, s_nxt, NEG_BIG)
                s_scr[(gi + 1) & 1, n] = s_nxt
                nxt_max = s_nxt if nxt_max is None else jnp.maximum(nxt_max, s_nxt)
            heads = [alpha[:, h * ATT_BLK:(h + 1) * ATT_BLK] * acc[h * HEAD_W:(h + 1) * HEAD_W, :] + pv[h]
                     for h in range(N_HEADS)]
            return (jnp.max(nxt_max, axis=0, keepdims=True), m_new, l_new, jnp.concatenate(heads, axis=0))

        init = (scores(0), jnp.full((1, n_lane), NEG_BIG, F32), jnp.zeros((1, n_lane), F32),
                jnp.zeros((BRANCH_W, ATT_BLK), F32))
        g_max, m_run, l_run, acc = lax.fori_loop(0, n_grp - 1, step, init)
        _, l_run, acc = update(n_grp - 1, g_max, m_run, l_run, acc)
        out_t = jnp.concatenate([acc[h * HEAD_W:(h + 1) * HEAD_W, :] / l_run[:, h * ATT_BLK:(h + 1) * ATT_BLK]
                                 for h in range(N_HEADS)], axis=0)
        y_ref[0, pl.ds(q0, ATT_BLK), :] = out_t.T
        return 0

    lax.fori_loop(0, n_blk, q_block, 0)


def _attn_prompt(cols3, cos, sin, layer, depth, kv_prev):
    b, t, _ = cols3.shape
    n_blk = t // ATT_BLK
    assert n_blk * ATT_BLK == t
    has_prev = kv_prev is not None
    body = functools.partial(_attn_prompt_body, n_blk, has_prev)
    wtab = _att_weight_table(n_blk)
    kv_shape = jax.ShapeDtypeStruct((depth, b, BRANCH_W, t), F32)
    kv_spec = pl.BlockSpec((1, 1, BRANCH_W, t), lambda i: (layer, i, 0, 0))
    prev_in = list(kv_prev) if has_prev else []
    return pl.pallas_call(
        body,
        out_shape=(jax.ShapeDtypeStruct((b, t, BRANCH_W), F32), kv_shape, kv_shape),
        grid=(b,),
        in_specs=[pl.BlockSpec((1, t, 3 * BRANCH_W), lambda i: (i, 0, 4)),
                  _resident((t, BRANCH_W)), _resident((t, BRANCH_W)), _resident(wtab.shape)]
                 + [pl.BlockSpec(memory_space=pl.ANY)] * len(prev_in),
        out_specs=(pl.BlockSpec((1, t, BRANCH_W), lambda i: (i, 0, 0)), kv_spec, kv_spec),
        input_output_aliases={4: 1, 5: 2} if has_prev else {},
        scratch_shapes=[pltpu.VMEM((BRANCH_W, t), BF16),
                        pltpu.VMEM((t, BRANCH_W), BF16),
                        pltpu.VMEM((BRANCH_W, t), BF16),
                        pltpu.VMEM((2, _att_group(n_blk), ATT_BLK, N_HEADS * ATT_BLK), F32)],
        compiler_params=_params(("parallel",)),
        name="attn_prompt",
    )(cols3, cos, sin, wtab, *prev_in)


def _attn_sample_body(t_len, l_cache, qkv_ref, kc_ref, vc_ref, cos_ref, sin_ref, y_ref, k_ref, pad_scr):
    tp = 8
    pad_scr[...] = jnp.zeros(pad_scr.shape, F32)
    pad_scr[pl.ds(0, t_len), :] = qkv_ref[0]
    qkv = pad_scr[...]
    cos = cos_ref[...]
    sin = sin_ref[...]
    q = _rope_rows(qkv[:, 0:BRANCH_W], cos, sin) * (HEAD_W ** -0.5)
    k_new = _rope_rows(qkv[:, BRANCH_W:2 * BRANCH_W], cos, sin)
    v_new = qkv[:, 2 * BRANCH_W:3 * BRANCH_W]
    k_ref[0] = k_new[0:t_len]
    lane_head = lax.broadcasted_iota(jnp.int32, (tp, BRANCH_W), 1) >> HEAD_SHIFT
    q_bd = jnp.concatenate([jnp.where(lane_head == h, q, 0.0) for h in range(N_HEADS)], axis=0).astype(BF16)
    n_row = N_HEADS * tp
    k_t = kc_ref[0, 0].reshape(BRANCH_W, l_cache)
    v_t = vc_ref[0, 0].reshape(BRANCH_W, l_cache)
    s_c = _dot(q_bd, k_t.astype(BF16))
    s_n = _dot_nt(q_bd, k_new.astype(BF16))
    t_row = lax.broadcasted_iota(jnp.int32, (n_row, 1), 0) & (tp - 1)
    col_c = lax.broadcasted_iota(jnp.int32, (n_row, l_cache), 1)
    w_c = _pattern_weight(l_cache + t_row - col_c)
    col_n = lax.broadcasted_iota(jnp.int32, (n_row, tp), 1)
    w_n = jnp.where(col_n < t_len, _pattern_weight(t_row - col_n), 0.0)
    s_c = jnp.where(w_c > 0.0, s_c, NEG_BIG)
    s_n = jnp.where(w_n > 0.0, s_n, NEG_BIG)
    m = jnp.maximum(jnp.max(s_c, axis=-1, keepdims=True), jnp.max(s_n, axis=-1, keepdims=True))
    p_c = w_c * jnp.exp(s_c - m)
    p_n = w_n * jnp.exp(s_n - m)
    l = jnp.sum(p_c, axis=-1, keepdims=True) + jnp.sum(p_n, axis=-1, keepdims=True)
    num = _dot_nt(p_c.astype(BF16), v_t.astype(BF16)) + _dot(p_n.astype(BF16), v_new.astype(BF16))
    ratio = num / l
    out = jnp.zeros((tp, BRANCH_W), F32)
    for h in range(N_HEADS):
        out = jnp.where(lane_head == h, ratio[h * tp:(h + 1) * tp, :], out)
    y_ref[0] = out[0:t_len]


def _attn_sample(cols3, k_cache_t, v_cache_t, layer, cos, sin):
    b, t, _ = cols3.shape
    l_cache = k_cache_t.shape[-1]
    assert t <= 8
    body = functools.partial(_attn_sample_body, t, l_cache)
    cache_spec = pl.BlockSpec((1, 1, N_HEADS, HEAD_W, l_cache), lambda i: (layer, i, 0, 0, 0))
    return pl.pallas_call(
        body,
        out_shape=(jax.ShapeDtypeStruct((b, t, BRANCH_W), F32), jax.ShapeDtypeStruct((b, t, BRANCH_W), F32)),
        grid=(b,),
        in_specs=[pl.BlockSpec((1, t, 3 * BRANCH_W), lambda i: (i, 0, 4)),
                  cache_spec, cache_spec,
                  pl.BlockSpec((8, BRANCH_W), lambda i: (0, 0)),
                  pl.BlockSpec((8, BRANCH_W), lambda i: (0, 0))],
        out_specs=(pl.BlockSpec((1, t, BRANCH_W), lambda i: (i, 0, 0)),
                   pl.BlockSpec((1, t, BRANCH_W), lambda i: (i, 0, 0))),
        scratch_shapes=[pltpu.VMEM((8, 3 * BRANCH_W), F32)],
        compiler_params=_params(("parallel",)),
        name="attn_sample",
    )(cols3, k_cache_t, v_cache_t, cos, sin)


def _merge_body(x_ref, g_ref, win_ref, ya_ref, yb_ref, yc_ref, yd_ref, wb_ref, wo_ref, o_ref):
    x = x_ref[...]
    u = _rms_rows(x, g_ref[0]).astype(BF16)
    merged = None
    for n, y_ref in enumerate((ya_ref, yb_ref, yc_ref, yd_ref)):
        z = _dot(y_ref[...].astype(BF16), wb_ref[0, n])
        gate_cols = slice(N_MIX_COLS + n * D_MODEL, N_MIX_COLS + (n + 1) * D_MODEL)
        gate = jax.nn.sigmoid(_dot(u, win_ref[0, :, gate_cols]))
        merged = gate * z if merged is None else merged + gate * z
    o_ref[...] = x + _dot(merged.astype(BF16), wo_ref[0])


def _merge(x2d, ys, P, layer):
    n, d = x2d.shape
    tm = min(n, 512)
    row = lambda w: pl.BlockSpec((tm, w), lambda i: (i, 0))
    return pl.pallas_call(
        _merge_body,
        out_shape=jax.ShapeDtypeStruct((n, d), F32),
        grid=(n // tm,),
        in_specs=[row(d), _layer_block(P["norm1_g"].shape, layer), _layer_block(P["w_in"].shape, layer),
                  row(BRANCH_W), row(BRANCH_W), row(BRANCH_W), row(BRANCH_W),
                  _layer_block(P["w_branch"].shape, layer), _layer_block(P["w_out"].shape, layer)],
        out_specs=row(d),
        compiler_params=_params(("parallel",)),
        name="merge",
    )(x2d, P["norm1_g"], P["w_in"], *ys, P["w_branch"], P["w_out"])


FFN_COL = 3072


def _ffn_columns(x, v, prev1, prev2, first, second, wg_ref, wu_ref, wd_ref, cw_ref, cb_ref, keep_gate):
    acc = jnp.zeros(x.shape, F32)
    cw = cw_ref[0]
    for c in range(D_FF // FFN_COL):
        sl = slice(c * FFN_COL, (c + 1) * FFN_COL)
        hg = _dot(v, wg_ref[0, :, sl])
        hu = _dot(v, wu_ref[0, :, sl])
        h1 = jnp.where(first, prev1(sl), pltpu.roll(hg, 1, 0))
        h2 = jnp.where(first | second, prev2(sl), pltpu.roll(hg, 2, 0))
        conv = cb_ref[0, :, sl] + cw[2:3, sl] * hg + cw[1:2, sl] * h1 + cw[0:1, sl] * h2
        hmid = (jax.nn.gelu(conv) * hu).astype(BF16)
        acc = acc + _dot(hmid, wd_ref[0, sl, :])
        keep_gate(sl, hg)
    return x + acc


def _ffn_body(tm, n_t, has_final, x_ref, g_ref, wg_ref, wu_ref, wd_ref, cw_ref, cb_ref, st_ref, *rest):
    if has_final:
        fg_ref, o_ref, ns_ref, carry_scr = rest
    else:
        o_ref, ns_ref, carry_scr = rest
    ti = pl.program_id(1)

    @pl.when(ti == 0)
    def _():
        carry_scr[...] = st_ref[0, 0]

    x = x_ref[0]
    v = _rms_rows(x, g_ref[0]).astype(BF16)
    row = lax.broadcasted_iota(jnp.int32, (tm, 1), 0)

    def prev2(sl):
        return jnp.where(row == 0, carry_scr[0:1, sl], carry_scr[1:2, sl])

    def keep_gate(sl, hg):
        carry_scr[:, sl] = hg[tm - 2:tm, :]

    y = _ffn_columns(x, v, lambda sl: carry_scr[1:2, sl], prev2, row == 0, row == 1,
                     wg_ref, wu_ref, wd_ref, cw_ref, cb_ref, keep_gate)
    o_ref[0] = _rms_rows(y, fg_ref[...]) if has_final else y

    @pl.when(ti == n_t - 1)
    def _():
        ns_ref[0] = carry_scr[...]


def _ffn_rows_body(seq, has_final, x_ref, g_ref, wg_ref, wu_ref, wd_ref, cw_ref, cb_ref, p1_ref, p2_ref, *rest):
    if has_final:
        fg_ref, o_ref, hg_ref = rest
    else:
        o_ref, hg_ref = rest
    x = x_ref[...]
    v = _rms_rows(x, g_ref[0]).astype(BF16)
    step = lax.broadcasted_iota(jnp.int32, (x.shape[0], 1), 0) & (seq - 1)

    def keep_gate(sl, hg):
        hg_ref[:, sl] = hg

    y = _ffn_columns(x, v, lambda sl: p1_ref[:, sl], lambda sl: p2_ref[:, sl], step == 0, step == 1,
                     wg_ref, wu_ref, wd_ref, cw_ref, cb_ref, keep_gate)
    o_ref[...] = _rms_rows(y, fg_ref[...]) if has_final else y


def _ffn(x3, state, P, layer, final_g):
    b, t, d = x3.shape
    has_final = final_g is not None
    extra_in = [final_g] if has_final else []
    extra_spec = [_resident((1, d))] if has_final else []
    names = ("norm2_g", "ffn_w_gate", "ffn_w_up", "ffn_w_down", "ffn_conv_w", "ffn_conv_b")
    weights = [P[n] for n in names]
    weight_specs = [_layer_block(w.shape, layer) for w in weights]
    if t >= 8:
        tm = min(t, 512)
        n_t = t // tm
        assert tm * n_t == t and tm % 8 == 0
        body = functools.partial(_ffn_body, tm, n_t, has_final)
        return pl.pallas_call(
            body,
            out_shape=(jax.ShapeDtypeStruct((b, t, d), F32), jax.ShapeDtypeStruct((b, 2, D_FF), F32)),
            grid=(b, n_t),
            in_specs=[pl.BlockSpec((1, tm, d), lambda i, j: (i, j, 0))] + weight_specs
                     + [pl.BlockSpec((1, 1, 2, D_FF), lambda i, j: (layer, i, 0, 0))] + extra_spec,
            out_specs=(pl.BlockSpec((1, tm, d), lambda i, j: (i, j, 0)),
                       pl.BlockSpec((1, 2, D_FF), lambda i, j: (i, 0, 0))),
            scratch_shapes=[pltpu.VMEM((2, D_FF), F32)],
            compiler_params=_params(("parallel", "arbitrary")),
            name="convffn",
        )(x3, *weights, state, *extra_in)
    assert t >= 2 and t & (t - 1) == 0 and (b * t) % 8 == 0
    n = b * t
    st = state[layer]
    zero = jnp.zeros((b, t - 1, D_FF), F32)
    p1 = jnp.concatenate([st[:, 1:2], zero], axis=1).reshape(n, D_FF)
    p2 = jnp.concatenate([st, zero[:, 1:]], axis=1).reshape(n, D_FF)
    body = functools.partial(_ffn_rows_body, t, has_final)
    full = lambda shape: pl.BlockSpec(shape, lambda i: (0,) * len(shape))
    y, hg = pl.pallas_call(
        body,
        out_shape=(jax.ShapeDtypeStruct((n, d), F32), jax.ShapeDtypeStruct((n, D_FF), F32)),
        grid=(1,),
        in_specs=[full((n, d))] + weight_specs + [full((n, D_FF)), full((n, D_FF))] + extra_spec,
        out_specs=(full((n, d)), full((n, D_FF))),
        compiler_params=_params(("arbitrary",)),
        name="convffn_rows",
    )(x3.reshape(n, d), *weights, p1, p2, *extra_in)
    return y.reshape(b, t, d), hg.reshape(b, t, D_FF)[:, t - 2:, :]


def _block_diag(w):
    eye = jnp.asarray(np.eye(N_HEADS, dtype=np.float32))
    return jnp.einsum("lhij,hg->lhigj", w, eye).reshape(w.shape[0], BRANCH_W, BRANCH_W)


def _head_sum_matrix3():
    i = np.arange(BRANCH_W) // HEAD_W
    e = (i[:, None] == i[None, :]).astype(np.float32)
    return jnp.asarray(np.concatenate([e, e, e], axis=0), BF16)


def _rope_tables(pos, rows):
    half = HEAD_W // 2
    inv = ROPE_THETA ** (-jnp.arange(half, dtype=F32) / half)
    ang = pos.astype(F32)[:, None] * inv[None, :]
    cos = jnp.cos(ang)
    sin = jnp.sin(ang)
    cos_t = jnp.tile(jnp.concatenate([cos, cos], axis=1), (1, N_HEADS))
    sin_t = jnp.tile(jnp.concatenate([-sin, sin], axis=1), (1, N_HEADS))
    pad = rows - pos.shape[0]
    if pad:
        cos_t = jnp.pad(cos_t, ((0, pad), (0, 0)))
        sin_t = jnp.pad(sin_t, ((0, pad), (0, 0)))
    return cos_t, sin_t


def _all_layer_params(W):
    row = lambda a: a.reshape(a.shape[0], 1, -1)
    depth = W["w_in"].shape[0]
    zpad = jnp.zeros((depth, 64, BRANCH_W), F32)
    lb_sm = jax.nn.softmax(W["c_lb"].astype(F32), axis=0)
    P = {
        "w_in": W["w_in"].astype(BF16),
        "a_gx_bd": _split_weight(_block_diag(W["a_gx_w"])),
        "a_ga_bd": _split_weight(_block_diag(W["a_ga_w"])),
        "b_w2p": _split_weight(jnp.concatenate([W["b_w2"], zpad], axis=1)),
        "b_a2p": _split_weight(jnp.concatenate([zpad, W["b_a2"]], axis=1)),
        "b_g2": _split_weight(W["b_g2"]),
        "c_lb": row(jnp.cumsum(lb_sm, axis=0) - lb_sm[0]),
        "c_norm_g": row(jnp.tile(W["c_norm_g"], (1, N_HEADS))),
        "w_branch": W["w_branch"].astype(BF16), "w_out": W["w_out"].astype(BF16),
        "ffn_w_gate": W["ffn_w_gate"].astype(BF16), "ffn_w_up": W["ffn_w_up"].astype(BF16),
        "ffn_w_down": W["ffn_w_down"].astype(BF16),
        "a_conv_w": W["a_conv_w"], "ffn_conv_w": W["ffn_conv_w"],
        "e3": _head_sum_matrix3(),
    }
    for n in ("norm1_g", "norm2_g", "a_conv_b", "a_gx_b", "a_ga_b", "a_lambda", "b_mu", "b_w0", "b_a0", "b_k_k",
              "b_k_a", "b_r_k", "b_ln_w", "b_ln_b", "ffn_conv_b"):
        P[n] = row(W[n])
    return P


def _trunk(x, pos0, st, W, P):
    b, t, d = x.shape
    depth = W["w_in"].shape[0]
    has_cache = st["k"] is not None
    if has_cache:
        cos, sin = _rope_tables(pos0 + jnp.arange(t), 8)
        k_cache_t = jnp.transpose(st["k"], (0, 1, 3, 4, 2))
        v_cache_t = jnp.transpose(st["v"], (0, 1, 3, 4, 2))
    else:
        cos, sin = _rope_tables(pos0 + jnp.arange(t), t)
    ha = st["ha"][:, :, None, :]
    sh = st["sh"][:, :, None, :]
    sc_t = jnp.transpose(st["sc"], (0, 1, 2, 4, 3))
    outs = {n: [] for n in ("ha", "ca", "wkv", "sh", "sc", "k", "v", "cf")}
    kv_prev = None
    for l in range(depth):
        cols = _inproj(x.reshape(b * t, d), P, l).reshape(b, t, N_MIX_COLS)
        y_a, h_new, ca_new = _rglru(cols, ha, st["ca"], P, l)
        y_b, sh_new, wkv_new = _rwkv(cols, sh, st["wkv"], P, l)
        y_c, sc_new = _hgrn(cols, sc_t, P, l)
        if has_cache:
            y_d, k_rows = _attn_sample(cols, k_cache_t, v_cache_t, l, cos, sin)
            k_out = k_rows.reshape(b, t, N_HEADS, HEAD_W)
            v_out = cols[:, :, N_MIX_COLS - BRANCH_W:].reshape(b, t, N_HEADS, HEAD_W)
        else:
            y_d, k_t, v_t = _attn_prompt(cols, cos, sin, l, depth, kv_prev)
            kv_prev = (k_t, v_t)
        ys = [y.reshape(b * t, BRANCH_W) for y in (y_a, y_b, y_c, y_d)]
        x1 = _merge(x.reshape(b * t, d), ys, P, l)
        final_g = W["final_norm_g"][None, :] if l == depth - 1 else None
        x, cf_new = _ffn(x1.reshape(b, t, d), st["cf"], P, l, final_g)
        outs["ha"].append(h_new[:, 0, :])
        outs["ca"].append(ca_new)
        outs["wkv"].append(wkv_new)
        outs["sh"].append(sh_new[:, 0, :])
        outs["sc"].append(sc_new)
        outs["cf"].append(cf_new)
        if has_cache:
            outs["k"].append(k_out)
            outs["v"].append(v_out)
    if not has_cache:
        del outs["k"], outs["v"]
    res = {n: jnp.stack(outs[n]) for n in outs}
    res["sc"] = jnp.transpose(res["sc"], (0, 1, 2, 4, 3))
    if not has_cache:
        keep = min(D_WIN_MAX, t)
        for n, buf in zip(("k", "v"), kv_prev):
            res[n] = jnp.transpose(buf.reshape(depth, b, N_HEADS, HEAD_W, t)[..., t - keep:], (0, 1, 4, 2, 3))
    return (x,) + tuple(res[n] for n in ("ha", "ca", "wkv", "sh", "sc", "k", "v", "cf"))


def kernel(x_prompt, x_sample, state_a_h, state_a_conv, state_b_wkv, state_b_shift, state_c_s, cache_d_k, cache_d_v, state_ffn_conv, norm1_g, w_in, a_conv_w, a_conv_b, a_gx_w, a_gx_b, a_ga_w, a_ga_b, a_lambda, b_mu, b_w0, b_w2, b_a0, b_a2, b_g2, b_k_k, b_k_a, b_r_k, b_ln_w, b_ln_b, c_lb, c_norm_g, w_branch, w_out, norm2_g, ffn_w_gate, ffn_w_up, ffn_conv_w, ffn_conv_b, ffn_w_down, final_norm_g):
    W = {"norm1_g": norm1_g, "w_in": w_in, "a_conv_w": a_conv_w, "a_conv_b": a_conv_b, "a_gx_w": a_gx_w,
         "a_gx_b": a_gx_b, "a_ga_w": a_ga_w, "a_ga_b": a_ga_b, "a_lambda": a_lambda, "b_mu": b_mu,
         "b_w0": b_w0, "b_w2": b_w2, "b_a0": b_a0, "b_a2": b_a2, "b_g2": b_g2, "b_k_k": b_k_k,
         "b_k_a": b_k_a, "b_r_k": b_r_k, "b_ln_w": b_ln_w, "b_ln_b": b_ln_b, "c_lb": c_lb,
         "c_norm_g": c_norm_g, "w_branch": w_branch, "w_out": w_out, "norm2_g": norm2_g,
         "ffn_w_gate": ffn_w_gate, "ffn_w_up": ffn_w_up, "ffn_conv_w": ffn_conv_w,
         "ffn_conv_b": ffn_conv_b, "ffn_w_down": ffn_w_down, "final_norm_g": final_norm_g}
    depth = w_in.shape[0]
    P = _all_layer_params(W)
    b_p, t_p = x_prompt.shape[:2]

    def zeros(*s):
        return jnp.zeros((depth, b_p) + s, F32)

    st_p = {"ha": zeros(BRANCH_W), "ca": zeros(3, BRANCH_W), "wkv": zeros(N_HEADS, HEAD_W, HEAD_W),
            "sh": zeros(B_COLS), "sc": zeros(N_HEADS, C_EXPAND, HEAD_W), "k": None, "v": None,
            "cf": zeros(2, D_FF)}
    out_p = _trunk(x_prompt, 0, st_p, W, P)
    st_s = {"ha": state_a_h, "ca": state_a_conv, "wkv": state_b_wkv, "sh": state_b_shift, "sc": state_c_s,
            "k": cache_d_k, "v": cache_d_v, "cf": state_ffn_conv}
    out_s = _trunk(x_sample, PAST_LEN, st_s, W, P)
    return (out_p[0], out_s[0]) + out_p[1:] + out_s[1:]
```

```python
import functools

import numpy as np

import jax
import jax.numpy as jnp
from jax import lax
from jax.experimental import pallas as pl
from jax.experimental.pallas import tpu as pltpu

F32 = jnp.float32
BF16 = jnp.bfloat16
HI = lax.Precision.HIGHEST

D_MODEL = 1024
BRANCH_W = 256
N_HEADS = 4
HEAD_W = 64
HEAD_SHIFT = 6
C_EXPAND = 128
C_EXPAND_SHIFT = 7
C_FDIM = 512
B_COLS = 1024
D_FF = 3072
NORM_EPS = 1e-6
A_C = 8.0
B_GN_EPS = 64e-5
ROPE_THETA = 10000.0
PAST_LEN = 8192
D_WIN_MAX = 2048
NEG_BIG = -1e30
N_MIX_COLS = 3840
ATT_BLK = 128

VMEM_LIMIT = 56 * 1024 * 1024


def _dot(a, b, prec=None):
    return jnp.dot(a, b, preferred_element_type=F32, precision=prec)


def _dot_nt(a, b, prec=None):
    return lax.dot_general(a, b, (((1,), (1,)), ((), ())), preferred_element_type=F32, precision=prec)


def _dot_tn(a, b, prec=None):
    return lax.dot_general(a, b, (((0,), (0,)), ((), ())), preferred_element_type=F32, precision=prec)


def _rms_rows(x, g):
    ms = jnp.mean(x * x, axis=-1, keepdims=True)
    return x * lax.rsqrt(ms + NORM_EPS) * g


def _round_up(n, m):
    return (n + m - 1) // m * m


def _params(sem):
    return pltpu.CompilerParams(dimension_semantics=sem, vmem_limit_bytes=VMEM_LIMIT)


def _resident(shape):
    nd = len(shape)
    return pl.BlockSpec(shape, lambda *_: (0,) * nd, pipeline_mode=pl.Buffered(1))


def _split3(x):
    hi = x.astype(BF16)
    r1 = x - hi.astype(F32)
    mid = r1.astype(BF16)
    lo = (r1 - mid.astype(F32)).astype(BF16)
    return hi, mid, lo


def _head_sum(x, e3):
    return _dot(jnp.concatenate(_split3(x), axis=1), e3)


def _dot_split(x, w3):
    hi = x.astype(BF16)
    lo = (x - hi.astype(F32)).astype(BF16)
    return _dot(jnp.concatenate([hi, lo, hi], axis=1), w3)


def _split_weight(w):
    hi = w.astype(BF16)
    lo = (w - hi.astype(F32)).astype(BF16)
    return jnp.concatenate([hi, hi, lo], axis=1)


def _stack_heads(x, lane_head):
    return jnp.concatenate([jnp.where(lane_head == h, x, jnp.zeros_like(x)) for h in range(N_HEADS)], axis=0)


def _layer_block(shape, layer):
    nd = len(shape)
    return pl.BlockSpec((1,) + tuple(shape[1:]), lambda *_: (layer,) + (0,) * (nd - 1),
                        pipeline_mode=pl.Buffered(1))


def _inproj_body(x_ref, g_ref, w_ref, o_ref):
    o_ref[...] = _dot(_rms_rows(x_ref[...], g_ref[0]).astype(BF16), w_ref[0])


def _inproj(x2d, P, layer):
    n, d = x2d.shape
    tm = min(n, 1024)
    return pl.pallas_call(
        _inproj_body,
        out_shape=jax.ShapeDtypeStruct((n, N_MIX_COLS), F32),
        grid=(n // tm,),
        in_specs=[pl.BlockSpec((tm, d), lambda i: (i, 0)),
                  _layer_block(P["norm1_g"].shape, layer),
                  pl.BlockSpec((1, d, N_MIX_COLS), lambda i: (layer, 0, 0), pipeline_mode=pl.Buffered(1))],
        out_specs=pl.BlockSpec((tm, N_MIX_COLS), lambda i: (i, 0)),
        compiler_params=_params(("parallel",)),
        name="inproj",
    )(x2d, P["norm1_g"], P["w_in"])


def _rglru_body(t_len, tp, axg_ref, h0_ref, cbuf_ref, cw_ref, cb_ref, wgx_ref, bgx_ref, wga_ref, bga_ref,
                lam_ref, y_ref, hl_ref, nb_ref, xs_scr, a_scr, b_scr):
    if tp != t_len:
        xs_scr[...] = jnp.zeros(xs_scr.shape, F32)
    xs_scr[pl.ds(5, 3), :] = jnp.concatenate([cbuf_ref[0, 0], jnp.zeros((3, BRANCH_W), F32)], axis=1)
    xs_scr[pl.ds(8, t_len), :] = axg_ref[0]
    x0 = xs_scr[pl.ds(8, tp), 0:BRANCH_W]
    x1 = xs_scr[pl.ds(7, tp), 0:BRANCH_W]
    x2 = xs_scr[pl.ds(6, tp), 0:BRANCH_W]
    x3 = xs_scr[pl.ds(5, tp), 0:BRANCH_W]
    cw = cw_ref[0]
    xc = cb_ref[0] + cw[3:4] * x0 + cw[2:3] * x1 + cw[1:2] * x2 + cw[0:1] * x3
    gate_x = jax.nn.sigmoid(_dot_split(xc, wgx_ref[0]) + bgx_ref[0])
    gate_a = jax.nn.sigmoid(_dot_split(xc, wga_ref[0]) + bga_ref[0])
    log_a = -A_C * gate_a * jax.nn.softplus(-lam_ref[0])
    a = jnp.exp(log_a)
    th = jnp.tanh(log_a)
    b_in = jnp.sqrt(-2.0 * th / (1.0 - th)) * (gate_x * xc)
    if tp != t_len:
        row = lax.broadcasted_iota(jnp.int32, (tp, 1), 0)
        a = jnp.where(row < t_len, a, 1.0)
        b_in = jnp.where(row < t_len, b_in, 0.0)
    a_scr[...] = a
    b_scr[...] = b_in

    row8 = lax.broadcasted_iota(jnp.int32, (8, 1), 0)

    def group(g, carry):
        r0 = pl.multiple_of(g * 8, 8)
        ag = a_scr[pl.ds(r0, 8), :]
        bg = b_scr[pl.ds(r0, 8), :]
        for s in (1, 2, 4):
            a_sh = jnp.where(row8 >= s, pltpu.roll(ag, s, 0), 1.0)
            b_sh = jnp.where(row8 >= s, pltpu.roll(bg, s, 0), 0.0)
            bg = ag * b_sh + bg
            ag = ag * a_sh
        h = ag * carry + bg
        b_scr[pl.ds(r0, 8), :] = h
        return h[7:8, :]

    h_last = lax.fori_loop(0, tp // 8, group, h0_ref[0, 0])
    h = b_scr[...]
    gate = xs_scr[pl.ds(8, tp), BRANCH_W:2 * BRANCH_W]
    y = h * jax.nn.gelu(gate)
    y_ref[0] = y[0:t_len]
    hl_ref[0] = h_last
    nb_ref[0] = xs_scr[pl.ds(8 + t_len - 3, 3), 0:BRANCH_W]


def _rglru(cols3, h0, cbuf, P, layer):
    b, t, _ = cols3.shape
    tp = _round_up(t, 8)
    body = functools.partial(_rglru_body, t, tp)
    names = ("a_conv_w", "a_conv_b", "a_gx_bd", "a_gx_b", "a_ga_bd", "a_ga_b", "a_lambda")
    return pl.pallas_call(
        body,
        out_shape=(jax.ShapeDtypeStruct((b, t, BRANCH_W), F32),
                   jax.ShapeDtypeStruct((b, 1, BRANCH_W), F32),
                   jax.ShapeDtypeStruct((b, 3, BRANCH_W), F32)),
        grid=(b,),
        in_specs=[pl.BlockSpec((1, t, 2 * BRANCH_W), lambda i: (i, 0, 0)),
                  pl.BlockSpec((1, 1, 1, BRANCH_W), lambda i: (layer, i, 0, 0)),
                  pl.BlockSpec((1, 1, 3, BRANCH_W), lambda i: (layer, i, 0, 0))]
                 + [_layer_block(P[n].shape, layer) for n in names],
        out_specs=(pl.BlockSpec((1, t, BRANCH_W), lambda i: (i, 0, 0)),
                   pl.BlockSpec((1, 1, BRANCH_W), lambda i: (i, 0, 0)),
                   pl.BlockSpec((1, 3, BRANCH_W), lambda i: (i, 0, 0))),
        scratch_shapes=[pltpu.VMEM((tp + 8, 2 * BRANCH_W), F32),
                        pltpu.VMEM((tp, BRANCH_W), F32),
                        pltpu.VMEM((tp, BRANCH_W), F32)],
        compiler_params=_params(("parallel",)),
        name="rglru",
    )(cols3, h0, cbuf, *[P[n] for n in names])


RWKV_GROUP = 4


def _rwkv_consts(tc, ch):
    t = np.arange(tc)
    same = (t[:, None] // ch) == (t[None, :] // ch)
    tril = same & (t[None, :] <= t[:, None])
    return jnp.asarray(np.concatenate([tril, same], axis=0).astype(np.float32), BF16)


def _rwkv_body(t_len, tc, n_t, ch, cb1_ref, cb2_ref, shift_ref, s0_ref, mu_ref, w0_ref, w2_ref, a0_ref, a2_ref,
               g2_ref, kk_ref, ka_ref, rk_ref, lnw_ref, lnb_ref, cs_ref, e3_ref, y_ref, last_ref, sn_ref,
               xs_scr, s_scr, at_scr, bt_scr, kt_scr, bh_scr, kh_scr, v_scr, rt_scr, gc_scr, y_scr, bon_scr,
               g_scr):
    ti = pl.program_id(1)
    t_blk = min(t_len, tc)
    cs = N_HEADS * ch
    half = B_COLS // 2

    @pl.when(ti == 0)
    def _():
        if t_blk != tc:
            xs_scr[...] = jnp.zeros(xs_scr.shape, F32)
        xs_scr[pl.ds(7, 1), :] = shift_ref[0, 0]
        s_scr[...] = jnp.zeros(s_scr.shape, F32)
        for h in range(N_HEADS):
            s_scr[h * HEAD_W:(h + 1) * HEAD_W, h * HEAD_W:(h + 1) * HEAD_W] = s0_ref[0, 0, h]

    xs_scr[pl.ds(8, t_blk), 0:half] = cb1_ref[0]
    xs_scr[pl.ds(8, t_blk), half:B_COLS] = cb2_ref[0]
    cb = xs_scr[pl.ds(8, tc), :]
    shifted = xs_scr[pl.ds(7, tc), :]
    cm = cb + (shifted - cb) * mu_ref[0]
    r = cm[:, 0:256]
    k = cm[:, 256:512]
    v = cm[:, 512:768]
    lora = cm[:, 768:896]
    w = -jax.nn.softplus(-(w0_ref[0] + _dot_split(jnp.tanh(lora), w2_ref[0]))) - 0.5
    log_w = -jnp.exp(w)
    a = jax.nn.sigmoid(a0_ref[0] + _dot_split(lora, a2_ref[0]))
    g_scr[...] = _dot_split(jax.nn.sigmoid(cm[:, 896:1024]), g2_ref[0])
    e3 = e3_ref[...]
    kk = k * kk_ref[0]
    kk = kk / jnp.maximum(jnp.sqrt(_head_sum(kk * kk, e3)), 1e-12)
    k2 = k * (1.0 + (a - 1.0) * ka_ref[0])
    bon_scr[...] = _head_sum(r * k2 * rk_ref[0], e3) * v
    a_s = -kk
    b_s = kk * a
    if t_blk != tc:
        row = lax.broadcasted_iota(jnp.int32, (tc, 1), 0)
        live = row < t_blk
        log_w = jnp.where(live, log_w, 0.0)
        a_s = jnp.where(live, a_s, 0.0)
        b_s = jnp.where(live, b_s, 0.0)
        k2 = jnp.where(live, k2, 0.0)
    parts = jnp.concatenate(_split3(log_w), axis=1)
    sub = cs_ref.shape[1]
    cum, tot = [], []
    for j in range(tc // sub):
        cc = _dot(cs_ref[...], parts[j * sub:(j + 1) * sub, :])
        cc = cc[:, 0:256] + cc[:, 256:512] + cc[:, 512:768]
        cum.append(cc[0:sub])
        tot.append(cc[sub:2 * sub])
    cum = jnp.concatenate(cum, axis=0)
    tot = jnp.concatenate(tot, axis=0)
    inv = jnp.exp(-cum)
    tail = jnp.exp(tot - cum)
    at_scr[...] = (a_s * jnp.exp(cum - log_w)).astype(BF16)
    bt_scr[...] = (b_s * inv).astype(BF16)
    kt_scr[...] = (k2 * inv).astype(BF16)
    bh_scr[...] = (b_s * tail).astype(BF16)
    kh_scr[...] = (k2 * tail).astype(BF16)
    v_scr[...] = v.astype(BF16)
    rt_scr[...] = r * jnp.exp(cum)
    gc_scr[...] = jnp.exp(tot)

    row_t = lax.broadcasted_iota(jnp.int32, (cs, cs), 0) & (ch - 1)
    col_t = lax.broadcasted_iota(jnp.int32, (cs, cs), 1) & (ch - 1)
    strict = col_t < row_t
    row_t2 = lax.broadcasted_iota(jnp.int32, (cs, 2 * cs), 0) & (ch - 1)
    col_t2 = lax.broadcasted_iota(jnp.int32, (cs, 2 * cs), 1) & (ch - 1)
    incl2 = col_t2 <= row_t2
    eye = (lax.broadcasted_iota(jnp.int32, (cs, cs), 0) == lax.broadcasted_iota(jnp.int32, (cs, cs), 1)).astype(F32)
    lane_head = lax.broadcasted_iota(jnp.int32, (ch, BRANCH_W), 1) >> HEAD_SHIFT
    n_dbl = ch.bit_length() - 1

    def prepare(c):
        r0 = c * ch if isinstance(c, int) else pl.multiple_of(c * ch, ch)
        abd, bbd, kbd, bhd, khd, vbd = (_stack_heads(s[pl.ds(r0, ch), :], lane_head)
                                        for s in (at_scr, bt_scr, kt_scr, bh_scr, kh_scr, v_scr))
        rbd = _stack_heads(rt_scr[pl.ds(r0, ch), :], lane_head)
        s1 = _dot_nt(jnp.concatenate([abd, rbd.astype(BF16)], axis=0), jnp.concatenate([bbd, kbd], axis=0))
        l_ab = jnp.where(strict, s1[0:cs, 0:cs], 0.0)
        l_ak = jnp.where(strict, s1[0:cs, cs:2 * cs], 0.0)
        m_r = jnp.where(incl2, s1[cs:2 * cs, :], 0.0).astype(BF16)
        tinv = eye + l_ab
        lb16 = l_ab.astype(BF16)
        yield
        pw = _dot(lb16, lb16)
        for i in range(1, n_dbl):
            pw16 = pw.astype(BF16)
            yield
            if i < n_dbl - 1:
                both = _dot(jnp.concatenate([tinv.astype(BF16), pw16], axis=0), pw16)
                tinv = tinv + both[0:cs]
                pw = both[cs:2 * cs]
            else:
                tinv = tinv + _dot(tinv.astype(BF16), pw16)
        yield
        w1 = _dot(l_ak.astype(BF16), vbd)
        yield
        x = _dot(tinv.astype(BF16), jnp.concatenate([abd, w1.astype(BF16)], axis=1))
        yield
        z = _dot(m_r[:, 0:cs], x.astype(BF16))
        r_hat = rbd + z[:, 0:BRANCH_W]
        yield
        y_hat = z[:, BRANCH_W:2 * BRANCH_W] + _dot(m_r[:, cs:2 * cs], vbd)
        ar = jnp.concatenate([x[:, 0:BRANCH_W], r_hat], axis=0).astype(BF16)
        g_end = gc_scr[pl.ds(r0, ch), :][0:1, :]
        return ar, x[:, BRANCH_W:2 * BRANCH_W], y_hat, vbd, jnp.concatenate([bhd, khd], axis=0), g_end

    def advance(c0, prepared):
        for n, (ar, u_hat, y_hat, vbd, bk, g_end) in enumerate(prepared):
            c = c0 + n
            r0 = c * ch if isinstance(c, int) else pl.multiple_of(c * ch, ch)
            sb = s_scr[...]
            uy = _dot_nt(ar, sb.astype(BF16))
            u = uy[0:cs] + u_hat
            ys = uy[cs:2 * cs] + y_hat
            y_scr[pl.ds(r0, ch), :] = ys[0:ch] + ys[ch:2 * ch] + ys[2 * ch:3 * ch] + ys[3 * ch:4 * ch]
            yield
            s_scr[...] = sb * g_end + _dot_tn(jnp.concatenate([u.astype(BF16), vbd], axis=0), bk)
            yield

    def interleave(gens):
        done = [None] * len(gens)
        live = list(range(len(gens)))
        while live:
            for i in list(live):
                try:
                    next(gens[i])
                except StopIteration as stop:
                    done[i] = stop.value
                    live.remove(i)
        return done

    n_chunk = tc // ch
    grp = RWKV_GROUP if n_chunk % RWKV_GROUP == 0 else 1
    first = tuple(interleave([prepare(c) for c in range(grp)]))
    if n_chunk > grp:
        def group(i, carry):
            c0 = i * grp
            res = interleave([prepare(c0 + grp + n) for n in range(grp)] + [advance(c0, carry)])
            return tuple(res[0:grp])

        first = lax.fori_loop(0, n_chunk // grp - 1, group, first)

    def finish(r0, rows, live_rows):
        y = y_scr[r0:r0 + rows, :]
        yield
        mean = _head_sum(y, e3) * (1.0 / HEAD_W)
        dev = y - mean
        yield
        var = _head_sum(dev * dev, e3) * (1.0 / HEAD_W)
        yn = dev * lax.rsqrt(var + B_GN_EPS) * lnw_ref[0] + lnb_ref[0]
        out = (yn + bon_scr[r0:r0 + rows, :]) * g_scr[r0:r0 + rows, :]
        y_ref[0, r0:r0 + live_rows, :] = out[0:live_rows]

    done_rows = (n_chunk - grp) * ch
    interleave([advance(n_chunk - grp, first)] + ([finish(0, done_rows, done_rows)] if done_rows else []))
    interleave([finish(done_rows, tc - done_rows, t_blk - done_rows)])
    xs_scr[pl.ds(7, 1), :] = xs_scr[pl.ds(8 + t_blk - 1, 1), :]

    @pl.when(ti == n_t - 1)
    def _():
        last_ref[0] = xs_scr[pl.ds(7, 1), :]
        for h in range(N_HEADS):
            sn_ref[0, h] = s_scr[h * HEAD_W:(h + 1) * HEAD_W, h * HEAD_W:(h + 1) * HEAD_W]


def _rwkv(cols3, shift, s0, P, layer):
    b, t, _ = cols3.shape
    if t >= 64:
        ch, tc = 64, min(t, 1024)
        assert t % tc == 0
    else:
        ch = 16
        tc = _round_up(t, ch)
    n_t = max(1, t // tc)
    t_blk = min(t, tc)
    body = functools.partial(_rwkv_body, t, tc, n_t, ch)
    consts = _rwkv_consts(min(tc, 256), ch)
    names = ("b_mu", "b_w0", "b_w2p", "b_a0", "b_a2p", "b_g2", "b_k_k", "b_k_a", "b_r_k", "b_ln_w", "b_ln_b")
    half = B_COLS // 2
    sc16 = lambda: pltpu.VMEM((tc, BRANCH_W), BF16)
    sc32 = lambda: pltpu.VMEM((tc, BRANCH_W), F32)
    return pl.pallas_call(
        body,
        out_shape=(jax.ShapeDtypeStruct((b, t, BRANCH_W), F32),
                   jax.ShapeDtypeStruct((b, 1, B_COLS), F32),
                   jax.ShapeDtypeStruct((b, N_HEADS, HEAD_W, HEAD_W), F32)),
        grid=(b, n_t),
        in_specs=[pl.BlockSpec((1, t_blk, half), lambda i, j: (i, j, 1)),
                  pl.BlockSpec((1, t_blk, half), lambda i, j: (i, j, 2)),
                  pl.BlockSpec((1, 1, 1, B_COLS), lambda i, j: (layer, i, 0, 0)),
                  pl.BlockSpec((1, 1, N_HEADS, HEAD_W, HEAD_W), lambda i, j: (layer, i, 0, 0, 0))]
                 + [_layer_block(P[n].shape, layer) for n in names]
                 + [_resident(consts.shape), _resident(P["e3"].shape)],
        out_specs=(pl.BlockSpec((1, t_blk, BRANCH_W), lambda i, j: (i, j, 0)),
                   pl.BlockSpec((1, 1, B_COLS), lambda i, j: (i, 0, 0)),
                   pl.BlockSpec((1, N_HEADS, HEAD_W, HEAD_W), lambda i, j: (i, 0, 0, 0))),
        scratch_shapes=[pltpu.VMEM((tc + 8, B_COLS), F32),
                        pltpu.VMEM((BRANCH_W, BRANCH_W), F32),
                        sc16(), sc16(), sc16(), sc16(), sc16(), sc16(),
                        sc32(), sc32(), sc32(), sc32(), sc32()],
        compiler_params=_params(("parallel", "arbitrary")),
        name="rwkv7",
    )(cols3, cols3, shift, s0, *[P[n] for n in names], consts, P["e3"])


HGRN_TILES = 2


def _hgrn_consts(tc):
    nl = tc.bit_length() - 1
    t = np.arange(tc)
    tril = (t[None, :] <= t[:, None]).astype(np.float32)
    x = t[:, None] ^ t[None, :]
    lvl = np.floor(np.log2(np.maximum(x, 1))).astype(np.int32)
    lvl = np.where(t[:, None] > t[None, :], lvl, np.where(t[:, None] == t[None, :], nl, -1))
    return jnp.asarray(tril, BF16), jnp.asarray(lvl, jnp.int32), nl


def _hgrn_body(t_len, tc, n_sub, n_t, nl, cq_ref, cf_ref, cig_ref, s0_ref, lb_ref, ng_ref, sel_ref, lvl_ref,
               e3_ref, y_ref, sn_ref, in_scr, st_scr):
    ti = pl.program_id(1)
    t_blk = min(t_len, tc)

    @pl.when(ti == 0)
    def _():
        st_scr[...] = jnp.zeros(st_scr.shape, F32)
        for h in range(N_HEADS):
            st_scr[h * HEAD_W:(h + 1) * HEAD_W, h * C_EXPAND:(h + 1) * C_EXPAND] = s0_ref[0, 0, h]

    if t_blk != tc:
        in_scr[...] = jnp.zeros(in_scr.shape, F32)
        in_scr[0, pl.ds(0, t_blk), :] = cq_ref[0]
        in_scr[1, pl.ds(0, t_blk), :] = cf_ref[0]
        in_scr[2, pl.ds(0, t_blk), :] = cig_ref[0]

    gens = [_hgrn_tile(n * tc, t_blk, tc, nl, cq_ref, cf_ref, cig_ref, lb_ref, ng_ref, sel_ref, lvl_ref, e3_ref,
                       y_ref, in_scr, st_scr) for n in range(n_sub)]
    live = list(gens)
    while live:
        for g in list(live):
            try:
                next(g)
            except StopIteration:
                live.remove(g)

    @pl.when(ti == n_t - 1)
    def _():
        for h in range(N_HEADS):
            sn_ref[0, h] = st_scr[h * HEAD_W:(h + 1) * HEAD_W, h * C_EXPAND:(h + 1) * C_EXPAND]


def _hgrn_tile(r0, t_blk, tc, nl, cq_ref, cf_ref, cig_ref, lb_ref, ng_ref, sel_ref, lvl_ref, e3_ref, y_ref,
               in_scr, st_scr):
    if t_blk != tc:
        q, f, cig = in_scr[0], in_scr[1], in_scr[2]
    else:
        q, f, cig = cq_ref[0, r0:r0 + tc, :], cf_ref[0, r0:r0 + tc, :], cig_ref[0, r0:r0 + tc, :]
    lb = lb_ref[0]
    fg = lb + (1.0 - lb) * jax.nn.sigmoid(f)
    log_f = jnp.log(fg)
    kk = 1.0 - fg
    fg_live = fg
    if t_blk != tc:
        row = lax.broadcasted_iota(jnp.int32, (tc, 1), 0)
        live = row < t_blk
        log_f = jnp.where(live, log_f, 0.0)
        kk = jnp.where(live, kk, 0.0)
        fg_live = jnp.where(live, fg, 1.0)
    hi = log_f.astype(BF16)
    lo = (log_f - hi.astype(F32)).astype(BF16)
    parts = jnp.concatenate([hi, lo], axis=1)
    yield
    d = _dot(sel_ref[...], parts)
    cum = d[:, 0:C_FDIM] + d[:, C_FDIM:2 * C_FDIM]
    odd = (lax.broadcasted_iota(jnp.int32, (tc, 1), 0) & 1) == 1

    lvl = lvl_ref[...]
    att = [None] * N_HEADS
    for l in range(nl + 1):
        if l == 0:
            e = jnp.where(odd, fg_live, 1.0)
        elif l < nl:
            m = 1 << l
            ref_rows = cum.reshape(tc // (2 * m), 2 * m, C_FDIM)[:, m - 1:m, :]
            ref_rows = jnp.broadcast_to(ref_rows, (tc // (2 * m), 2 * m, C_FDIM)).reshape(tc, C_FDIM)
            e = jnp.exp(-jnp.abs(cum - ref_rows))
        if l < nl:
            qs = (q * e).astype(BF16)
            ks = (kk * e).astype(BF16)
        else:
            qs = q.astype(BF16)
            ks = kk.astype(BF16)
        hit = lvl == l
        yield
        for h in range(N_HEADS):
            kl = slice(h * C_EXPAND, (h + 1) * C_EXPAND)
            a = _dot_nt(qs[:, kl], ks[:, kl])
            att[h] = jnp.where(hit, a, 0.0 if att[h] is None else att[h])
    c_last = cum[tc - 1:tc, :]
    lane_head = lax.broadcasted_iota(jnp.int32, (tc, BRANCH_W), 1) >> HEAD_SHIFT
    v = cig[:, 0:BRANCH_W]
    v_stack = _stack_heads(v, lane_head).astype(BF16)
    p_cat = jnp.concatenate([a.astype(BF16) for a in att], axis=1)
    q_dec = (q * jnp.exp(cum)).astype(BF16)
    k_tail = (kk * jnp.exp(c_last - cum)).astype(BF16)
    yield
    o_intra = _dot(p_cat, v_stack)
    yield
    st = st_scr[...]
    o = o_intra + _dot_nt(q_dec, st.astype(BF16))
    upd = _dot_tn(v.astype(BF16), k_tail)
    rh = lax.broadcasted_iota(jnp.int32, (BRANCH_W, C_FDIM), 0) >> HEAD_SHIFT
    ch_ = lax.broadcasted_iota(jnp.int32, (BRANCH_W, C_FDIM), 1) >> C_EXPAND_SHIFT
    st_scr[...] = st * jnp.exp(c_last) + jnp.where(rh == ch_, upd, 0.0)
    yield
    ms = _head_sum(o * o, e3_ref[...]) * (1.0 / HEAD_W)
    y = o * lax.rsqrt(ms + NORM_EPS) * ng_ref[0] * jax.nn.silu(cig[:, BRANCH_W:2 * BRANCH_W])
    y_ref[0, r0:r0 + t_blk, :] = y[0:t_blk]


def _hgrn(cols3, s0_t, P, layer):
    b, t, _ = cols3.shape
    e3 = P["e3"]
    tc = 256 if t >= 256 else 16
    assert t % tc == 0 or t < tc
    t_blk = min(t, tc)
    n_sub = HGRN_TILES if (t_blk == tc and (t // tc) % HGRN_TILES == 0) else 1
    n_t = max(1, t // (n_sub * tc))
    sel, lvl, nl = _hgrn_consts(tc)
    body = functools.partial(_hgrn_body, t, tc, n_sub, n_t, nl)
    blk = lambda idx: pl.BlockSpec((1, n_sub * t_blk, C_FDIM), lambda i, j: (i, j, idx))
    pad_shape = (3, tc, C_FDIM) if t_blk != tc else (1, 8, 128)
    return pl.pallas_call(
        body,
        out_shape=(jax.ShapeDtypeStruct((b, t, BRANCH_W), F32),
                   jax.ShapeDtypeStruct((b, N_HEADS, HEAD_W, C_EXPAND), F32)),
        grid=(b, n_t),
        in_specs=[blk(3), blk(4), blk(5),
                  pl.BlockSpec((1, 1, N_HEADS, HEAD_W, C_EXPAND), lambda i, j: (layer, i, 0, 0, 0)),
                  _layer_block(P["c_lb"].shape, layer),
                  _layer_block(P["c_norm_g"].shape, layer),
                  _resident(sel.shape), _resident(lvl.shape), _resident(e3.shape)],
        out_specs=(pl.BlockSpec((1, n_sub * t_blk, BRANCH_W), lambda i, j: (i, j, 0)),
                   pl.BlockSpec((1, N_HEADS, HEAD_W, C_EXPAND), lambda i, j: (i, 0, 0, 0))),
        scratch_shapes=[pltpu.VMEM(pad_shape, F32),
                        pltpu.VMEM((BRANCH_W, C_FDIM), F32)],
        compiler_params=_params(("parallel", "arbitrary")),
        name="hgrn2",
    )(cols3, cols3, cols3, s0_t, P["c_lb"], P["c_norm_g"], sel, lvl, e3)


def _rope_rows(x, cos, sin_signed):
    lane = lax.broadcasted_iota(jnp.int32, x.shape, 1)
    swapped = jnp.where((lane & (HEAD_W - 1)) < HEAD_W // 2,
                        pltpu.roll(x, BRANCH_W - HEAD_W // 2, 1), pltpu.roll(x, HEAD_W // 2, 1))
    return x * cos + swapped * sin_signed


def _pattern_weight(d):
    ok = d >= 0
    w = ((d <= 128).astype(F32) + (((d & 3) == 0) & (d <= 512)).astype(F32)
         + (((d & 15) == 0) & (d <= 2048)).astype(F32))
    return jnp.where(ok, w, 0.0)


def _att_group(n_blk):
    return next(g for g in (4, 2, 1) if n_blk % g == 0)


def _att_weight_table(n_blk):
    grp = _att_group(n_blk)
    r = np.arange(ATT_BLK)
    delta = np.arange(-(grp - 1), n_blk)
    d = delta[:, None, None] * ATT_BLK + r[None, None, :] - r[None, :, None]
    w = ((d <= 128).astype(np.float32) + ((d % 4 == 0) & (d <= 512)) + ((d % 16 == 0) & (d <= 2048)))
    return jnp.asarray(np.where(d >= 0, w, 0.0), F32)


def _attn_prompt_body(n_blk, has_prev, qkv_ref, cos_ref, sin_ref, wt_ref, *rest):
    y_ref, kt_ref, vt_ref, qt_scr, kb_scr, vt_scr, s_scr = rest[2:] if has_prev else rest
    qkv = qkv_ref[0]
    cos = cos_ref[...]
    sin = sin_ref[...]
    q = _rope_rows(qkv[:, 0:BRANCH_W], cos, sin) * (HEAD_W ** -0.5)
    k = _rope_rows(qkv[:, BRANCH_W:2 * BRANCH_W], cos, sin)
    v_t = qkv[:, 2 * BRANCH_W:3 * BRANCH_W].T
    kb_scr[...] = k.astype(BF16)
    qt_scr[...] = q.T.astype(BF16)
    vt_scr[...] = v_t.astype(BF16)
    kt_ref[0, 0] = k.T
    vt_ref[0, 0] = v_t
    n_lane = N_HEADS * ATT_BLK
    row_head = lax.broadcasted_iota(jnp.int32, (BRANCH_W, n_lane), 0) >> HEAD_SHIFT
    col_head = lax.broadcasted_iota(jnp.int32, (BRANCH_W, n_lane), 1) >> 7
    own_head = row_head == col_head
    grp = _att_group(n_blk)

    def q_block(i, _):
        q0 = pl.multiple_of(i * ATT_BLK, ATT_BLK)
        qt = qt_scr[:, pl.ds(q0, ATT_BLK)]
        q_bd = jnp.where(own_head, jnp.concatenate([qt] * N_HEADS, axis=1), jnp.zeros((), BF16))

        def weights(j):
            w1 = wt_ref[i - j + (grp - 1)]
            return jnp.concatenate([w1] * N_HEADS, axis=1)

        n_grp = (i + grp) // grp

        def scores(gi):
            g_max = None
            for n in range(grp):
                j = gi * grp + n
                s = _dot(kb_scr[pl.ds(pl.multiple_of(j * ATT_BLK, ATT_BLK), ATT_BLK), :], q_bd)
                s = jnp.where(weights(j) > 0.0, s, NEG_BIG)
                s_scr[gi & 1, n] = s
                g_max = s if g_max is None else jnp.maximum(g_max, s)
            return jnp.max(g_max, axis=0, keepdims=True)

        def update(gi, g_max, m_run, l_run, acc):
            m_new = jnp.maximum(m_run, g_max)
            alpha = jnp.exp(m_run - m_new)
            l_new = alpha * l_run
            pv = [None] * N_HEADS
            for n in range(grp):
                j = gi * grp + n
                p = weights(j) * jnp.exp(s_scr[gi & 1, n] - m_new)
                l_new = l_new + jnp.sum(p, axis=0, keepdims=True)
                pb = p.astype(BF16)
                for h in range(N_HEADS):
                    vt = vt_scr[h * HEAD_W:(h + 1) * HEAD_W, pl.ds(pl.multiple_of(j * ATT_BLK, ATT_BLK), ATT_BLK)]
                    term = _dot(vt, pb[:, h * ATT_BLK:(h + 1) * ATT_BLK])
                    pv[h] = term if pv[h] is None else pv[h] + term
            heads = [alpha[:, h * ATT_BLK:(h + 1) * ATT_BLK] * acc[h * HEAD_W:(h + 1) * HEAD_W, :] + pv[h]
                     for h in range(N_HEADS)]
            return m_new, l_new, jnp.concatenate(heads, axis=0)

        def step(gi, carry):
            g_max, m_run, l_run, acc = carry
            m_new = jnp.maximum(m_run, g_max)
            alpha = jnp.exp(m_run - m_new)
            l_new = alpha * l_run
            pv = [None] * N_HEADS
            nxt_max = None
            for n in range(grp):
                j = gi * grp + n
                jn = j + grp
                s_nxt = _dot(kb_scr[pl.ds(pl.multiple_of(jn * ATT_BLK, ATT_BLK), ATT_BLK), :], q_bd)
                p = weights(j) * jnp.exp(s_scr[gi & 1, n] - m_new)
                l_new = l_new + jnp.sum(p, axis=0, keepdims=True)
                pb = p.astype(BF16)
                for h in range(N_HEADS):
                    vt = vt_scr[h * HEAD_W:(h + 1) * HEAD_W, pl.ds(pl.multiple_of(j * ATT_BLK, ATT_BLK), ATT_BLK)]
                    term = _dot(vt, pb[:, h * ATT_BLK:(h + 1) * ATT_BLK])
                    pv[h] = term if pv[h] is None else pv[h] + term
                s_nxt = jnp.where(weights(jn) > 0.0, s_nxt, NEG_BIG)
                s_scr[(gi + 1) & 1, n] = s_nxt
                nxt_max = s_nxt if nxt_max is None else jnp.maximum(nxt_max, s_nxt)
            heads = [alpha[:, h * ATT_BLK:(h + 1) * ATT_BLK] * acc[h * HEAD_W:(h + 1) * HEAD_W, :] + pv[h]
                     for h in range(N_HEADS)]
            return (jnp.max(nxt_max, axis=0, keepdims=True), m_new, l_new, jnp.concatenate(heads, axis=0))

        init = (scores(0), jnp.full((1, n_lane), NEG_BIG, F32), jnp.zeros((1, n_lane), F32),
                jnp.zeros((BRANCH_W, ATT_BLK), F32))
        g_max, m_run, l_run, acc = lax.fori_loop(0, n_grp - 1, step, init)
        _, l_run, acc = update(n_grp - 1, g_max, m_run, l_run, acc)
        out_t = jnp.concatenate([acc[h * HEAD_W:(h + 1) * HEAD_W, :] / l_run[:, h * ATT_BLK:(h + 1) * ATT_BLK]
                                 for h in range(N_HEADS)], axis=0)
        y_ref[0, pl.ds(q0, ATT_BLK), :] = out_t.T
        return 0

    lax.fori_loop(0, n_blk, q_block, 0)


def _attn_prompt(cols3, cos, sin, layer, depth, kv_prev):
    b, t, _ = cols3.shape
    n_blk = t // ATT_BLK
    assert n_blk * ATT_BLK == t
    if kv_prev is None:
        kv_prev = (jnp.zeros((depth, b, BRANCH_W, t), F32), jnp.zeros((depth, b, BRANCH_W, t), F32))
    has_prev = True
    body = functools.partial(_attn_prompt_body, n_blk, has_prev)
    wtab = _att_weight_table(n_blk)
    kv_shape = jax.ShapeDtypeStruct((depth, b, BRANCH_W, t), F32)
    kv_spec = pl.BlockSpec((1, 1, BRANCH_W, t), lambda i: (layer, i, 0, 0))
    prev_in = list(kv_prev) if has_prev else []
    return pl.pallas_call(
        body,
        out_shape=(jax.ShapeDtypeStruct((b, t, BRANCH_W), F32), kv_shape, kv_shape),
        grid=(b,),
        in_specs=[pl.BlockSpec((1, t, 3 * BRANCH_W), lambda i: (i, 0, 4)),
                  _resident((t, BRANCH_W)), _resident((t, BRANCH_W)), _resident(wtab.shape)]
                 + [pl.BlockSpec(memory_space=pl.ANY)] * len(prev_in),
        out_specs=(pl.BlockSpec((1, t, BRANCH_W), lambda i: (i, 0, 0)), kv_spec, kv_spec),
        input_output_aliases={4: 1, 5: 2} if has_prev else {},
        scratch_shapes=[pltpu.VMEM((BRANCH_W, t), BF16),
                        pltpu.VMEM((t, BRANCH_W), BF16),
                        pltpu.VMEM((BRANCH_W, t), BF16),
                        pltpu.VMEM((2, _att_group(n_blk), ATT_BLK, N_HEADS * ATT_BLK), F32)],
        compiler_params=_params(("parallel",)),
        name="attn_prompt",
    )(cols3, cos, sin, wtab, *prev_in)


def _attn_sample_body(t_len, l_cache, qkv_ref, kc_ref, vc_ref, cos_ref, sin_ref, y_ref, k_ref, pad_scr):
    tp = 8
    pad_scr[...] = jnp.zeros(pad_scr.shape, F32)
    pad_scr[pl.ds(0, t_len), :] = qkv_ref[0]
    qkv = pad_scr[...]
    cos = cos_ref[...]
    sin = sin_ref[...]
    q = _rope_rows(qkv[:, 0:BRANCH_W], cos, sin) * (HEAD_W ** -0.5)
    k_new = _rope_rows(qkv[:, BRANCH_W:2 * BRANCH_W], cos, sin)
    v_new = qkv[:, 2 * BRANCH_W:3 * BRANCH_W]
    k_ref[0] = k_new[0:t_len]
    lane_head = lax.broadcasted_iota(jnp.int32, (tp, BRANCH_W), 1) >> HEAD_SHIFT
    q_bd = jnp.concatenate([jnp.where(lane_head == h, q, 0.0) for h in range(N_HEADS)], axis=0).astype(BF16)
    n_row = N_HEADS * tp
    k_t = kc_ref[0, 0].reshape(BRANCH_W, l_cache)
    v_t = vc_ref[0, 0].reshape(BRANCH_W, l_cache)
    s_c = _dot(q_bd, k_t.astype(BF16))
    s_n = _dot_nt(q_bd, k_new.astype(BF16))
    t_row = lax.broadcasted_iota(jnp.int32, (n_row, 1), 0) & (tp - 1)
    col_c = lax.broadcasted_iota(jnp.int32, (n_row, l_cache), 1)
    w_c = _pattern_weight(l_cache + t_row - col_c)
    col_n = lax.broadcasted_iota(jnp.int32, (n_row, tp), 1)
    w_n = jnp.where(col_n < t_len, _pattern_weight(t_row - col_n), 0.0)
    s_c = jnp.where(w_c > 0.0, s_c, NEG_BIG)
    s_n = jnp.where(w_n > 0.0, s_n, NEG_BIG)
    m = jnp.maximum(jnp.max(s_c, axis=-1, keepdims=True), jnp.max(s_n, axis=-1, keepdims=True))
    p_c = w_c * jnp.exp(s_c - m)
    p_n = w_n * jnp.exp(s_n - m)
    l = jnp.sum(p_c, axis=-1, keepdims=True) + jnp.sum(p_n, axis=-1, keepdims=True)
    num = _dot_nt(p_c.astype(BF16), v_t.astype(BF16)) + _dot(p_n.astype(BF16), v_new.astype(BF16))
    ratio = num / l
    out = jnp.zeros((tp, BRANCH_W), F32)
    for h in range(N_HEADS):
        out = jnp.where(lane_head == h, ratio[h * tp:(h + 1) * tp, :], out)
    y_ref[0] = out[0:t_len]


def _attn_sample(cols3, k_cache_t, v_cache_t, layer, cos, sin):
    b, t, _ = cols3.shape
    l_cache = k_cache_t.shape[-1]
    assert t <= 8
    body = functools.partial(_attn_sample_body, t, l_cache)
    cache_spec = pl.BlockSpec((1, 1, N_HEADS, HEAD_W, l_cache), lambda i: (layer, i, 0, 0, 0))
    return pl.pallas_call(
        body,
        out_shape=(jax.ShapeDtypeStruct((b, t, BRANCH_W), F32), jax.ShapeDtypeStruct((b, t, BRANCH_W), F32)),
        grid=(b,),
        in_specs=[pl.BlockSpec((1, t, 3 * BRANCH_W), lambda i: (i, 0, 4)),
                  cache_spec, cache_spec,
                  pl.BlockSpec((8, BRANCH_W), lambda i: (0, 0)),
                  pl.BlockSpec((8, BRANCH_W), lambda i: (0, 0))],
        out_specs=(pl.BlockSpec((1, t, BRANCH_W), lambda i: (i, 0, 0)),
                   pl.BlockSpec((1, t, BRANCH_W), lambda i: (i, 0, 0))),
        scratch_shapes=[pltpu.VMEM((8, 3 * BRANCH_W), F32)],
        compiler_params=_params(("parallel",)),
        name="attn_sample",
    )(cols3, k_cache_t, v_cache_t, cos, sin)


def _merge_body(x_ref, g_ref, win_ref, ya_ref, yb_ref, yc_ref, yd_ref, wb_ref, wo_ref, o_ref):
    x = x_ref[...]
    u = _rms_rows(x, g_ref[0]).astype(BF16)
    merged = None
    for n, y_ref in enumerate((ya_ref, yb_ref, yc_ref, yd_ref)):
        z = _dot(y_ref[...].astype(BF16), wb_ref[0, n])
        gate_cols = slice(N_MIX_COLS + n * D_MODEL, N_MIX_COLS + (n + 1) * D_MODEL)
        gate = jax.nn.sigmoid(_dot(u, win_ref[0, :, gate_cols]))
        merged = gate * z if merged is None else merged + gate * z
    o_ref[...] = x + _dot(merged.astype(BF16), wo_ref[0])


def _merge(x2d, ys, P, layer):
    n, d = x2d.shape
    tm = min(n, 512)
    row = lambda w: pl.BlockSpec((tm, w), lambda i: (i, 0))
    return pl.pallas_call(
        _merge_body,
        out_shape=jax.ShapeDtypeStruct((n, d), F32),
        grid=(n // tm,),
        in_specs=[row(d), _layer_block(P["norm1_g"].shape, layer), _layer_block(P["w_in"].shape, layer),
                  row(BRANCH_W), row(BRANCH_W), row(BRANCH_W), row(BRANCH_W),
                  _layer_block(P["w_branch"].shape, layer), _layer_block(P["w_out"].shape, layer)],
        out_specs=row(d),
        compiler_params=_params(("parallel",)),
        name="merge",
    )(x2d, P["norm1_g"], P["w_in"], *ys, P["w_branch"], P["w_out"])


FFN_COL = 3072


def _ffn_columns(x, v, prev1, prev2, first, second, wg_ref, wu_ref, wd_ref, cw_ref, cb_ref, keep_gate):
    acc = jnp.zeros(x.shape, F32)
    cw = cw_ref[0]
    for c in range(D_FF // FFN_COL):
        sl = slice(c * FFN_COL, (c + 1) * FFN_COL)
        hg = _dot(v, wg_ref[0, :, sl])
        hu = _dot(v, wu_ref[0, :, sl])
        h1 = jnp.where(first, prev1(sl), pltpu.roll(hg, 1, 0))
        h2 = jnp.where(first | second, prev2(sl), pltpu.roll(hg, 2, 0))
        conv = cb_ref[0, :, sl] + cw[2:3, sl] * hg + cw[1:2, sl] * h1 + cw[0:1, sl] * h2
        hmid = (jax.nn.gelu(conv) * hu).astype(BF16)
        acc = acc + _dot(hmid, wd_ref[0, sl, :])
        keep_gate(sl, hg)
    return x + acc


def _ffn_body(tm, n_t, has_final, x_ref, g_ref, wg_ref, wu_ref, wd_ref, cw_ref, cb_ref, st_ref, *rest):
    if has_final:
        fg_ref, o_ref, ns_ref, carry_scr = rest
    else:
        o_ref, ns_ref, carry_scr = rest
    ti = pl.program_id(1)

    @pl.when(ti == 0)
    def _():
        carry_scr[...] = st_ref[0, 0]

    x = x_ref[0]
    v = _rms_rows(x, g_ref[0]).astype(BF16)
    row = lax.broadcasted_iota(jnp.int32, (tm, 1), 0)

    def prev2(sl):
        return jnp.where(row == 0, carry_scr[0:1, sl], carry_scr[1:2, sl])

    def keep_gate(sl, hg):
        carry_scr[:, sl] = hg[tm - 2:tm, :]

    y = _ffn_columns(x, v, lambda sl: carry_scr[1:2, sl], prev2, row == 0, row == 1,
                     wg_ref, wu_ref, wd_ref, cw_ref, cb_ref, keep_gate)
    o_ref[0] = _rms_rows(y, fg_ref[...]) if has_final else y

    @pl.when(ti == n_t - 1)
    def _():
        ns_ref[0] = carry_scr[...]


def _ffn_rows_body(seq, has_final, x_ref, g_ref, wg_ref, wu_ref, wd_ref, cw_ref, cb_ref, p1_ref, p2_ref, *rest):
    if has_final:
        fg_ref, o_ref, hg_ref = rest
    else:
        o_ref, hg_ref = rest
    x = x_ref[...]
    v = _rms_rows(x, g_ref[0]).astype(BF16)
    step = lax.broadcasted_iota(jnp.int32, (x.shape[0], 1), 0) & (seq - 1)

    def keep_gate(sl, hg):
        hg_ref[:, sl] = hg

    y = _ffn_columns(x, v, lambda sl: p1_ref[:, sl], lambda sl: p2_ref[:, sl], step == 0, step == 1,
                     wg_ref, wu_ref, wd_ref, cw_ref, cb_ref, keep_gate)
    o_ref[...] = _rms_rows(y, fg_ref[...]) if has_final else y


def _ffn(x3, state, P, layer, final_g):
    b, t, d = x3.shape
    has_final = final_g is not None
    extra_in = [final_g] if has_final else []
    extra_spec = [_resident((1, d))] if has_final else []
    names = ("norm2_g", "ffn_w_gate", "ffn_w_up", "ffn_w_down", "ffn_conv_w", "ffn_conv_b")
    weights = [P[n] for n in names]
    weight_specs = [_layer_block(w.shape, layer) for w in weights]
    if t >= 8:
        tm = min(t, 512)
        n_t = t // tm
        assert tm * n_t == t and tm % 8 == 0
        body = functools.partial(_ffn_body, tm, n_t, has_final)
        return pl.pallas_call(
            body,
            out_shape=(jax.ShapeDtypeStruct((b, t, d), F32), jax.ShapeDtypeStruct((b, 2, D_FF), F32)),
            grid=(b, n_t),
            in_specs=[pl.BlockSpec((1, tm, d), lambda i, j: (i, j, 0))] + weight_specs
                     + [pl.BlockSpec((1, 1, 2, D_FF), lambda i, j: (layer, i, 0, 0))] + extra_spec,
            out_specs=(pl.BlockSpec((1, tm, d), lambda i, j: (i, j, 0)),
                       pl.BlockSpec((1, 2, D_FF), lambda i, j: (i, 0, 0))),
            scratch_shapes=[pltpu.VMEM((2, D_FF), F32)],
            compiler_params=_params(("parallel", "arbitrary")),
            name="convffn",
        )(x3, *weights, state, *extra_in)
    assert t >= 2 and t & (t - 1) == 0 and (b * t) % 8 == 0
    n = b * t
    st = state[layer]
    zero = jnp.zeros((b, t - 1, D_FF), F32)
    p1 = jnp.concatenate([st[:, 1:2], zero], axis=1).reshape(n, D_FF)
    p2 = jnp.concatenate([st, zero[:, 1:]], axis=1).reshape(n, D_FF)
    body = functools.partial(_ffn_rows_body, t, has_final)
    full = lambda shape: pl.BlockSpec(shape, lambda i: (0,) * len(shape))
    y, hg = pl.pallas_call(
        body,
        out_shape=(jax.ShapeDtypeStruct((n, d), F32), jax.ShapeDtypeStruct((n, D_FF), F32)),
        grid=(1,),
        in_specs=[full((n, d))] + weight_specs + [full((n, D_FF)), full((n, D_FF))] + extra_spec,
        out_specs=(full((n, d)), full((n, D_FF))),
        compiler_params=_params(("arbitrary",)),
        name="convffn_rows",
    )(x3.reshape(n, d), *weights, p1, p2, *extra_in)
    return y.reshape(b, t, d), hg.reshape(b, t, D_FF)[:, t - 2:, :]


def _block_diag(w):
    eye = jnp.asarray(np.eye(N_HEADS, dtype=np.float32))
    return jnp.einsum("lhij,hg->lhigj", w, eye).reshape(w.shape[0], BRANCH_W, BRANCH_W)


def _head_sum_matrix3():
    i = np.arange(BRANCH_W) // HEAD_W
    e = (i[:, None] == i[None, :]).astype(np.float32)
    return jnp.asarray(np.concatenate([e, e, e], axis=0), BF16)


def _rope_tables(pos, rows):
    half = HEAD_W // 2
    inv = ROPE_THETA ** (-jnp.arange(half, dtype=F32) / half)
    ang = pos.astype(F32)[:, None] * inv[None, :]
    cos = jnp.cos(ang)
    sin = jnp.sin(ang)
    cos_t = jnp.tile(jnp.concatenate([cos, cos], axis=1), (1, N_HEADS))
    sin_t = jnp.tile(jnp.concatenate([-sin, sin], axis=1), (1, N_HEADS))
    pad = rows - pos.shape[0]
    if pad:
        cos_t = jnp.pad(cos_t, ((0, pad), (0, 0)))
        sin_t = jnp.pad(sin_t, ((0, pad), (0, 0)))
    return cos_t, sin_t


def _all_layer_params(W):
    row = lambda a: a.reshape(a.shape[0], 1, -1)
    depth = W["w_in"].shape[0]
    zpad = jnp.zeros((depth, 64, BRANCH_W), F32)
    lb_sm = jax.nn.softmax(W["c_lb"].astype(F32), axis=0)
    P = {
        "w_in": W["w_in"].astype(BF16),
        "a_gx_bd": _split_weight(_block_diag(W["a_gx_w"])),
        "a_ga_bd": _split_weight(_block_diag(W["a_ga_w"])),
        "b_w2p": _split_weight(jnp.concatenate([W["b_w2"], zpad], axis=1)),
        "b_a2p": _split_weight(jnp.concatenate([zpad, W["b_a2"]], axis=1)),
        "b_g2": _split_weight(W["b_g2"]),
        "c_lb": row(jnp.cumsum(lb_sm, axis=0) - lb_sm[0]),
        "c_norm_g": row(jnp.tile(W["c_norm_g"], (1, N_HEADS))),
        "w_branch": W["w_branch"].astype(BF16), "w_out": W["w_out"].astype(BF16),
        "ffn_w_gate": W["ffn_w_gate"].astype(BF16), "ffn_w_up": W["ffn_w_up"].astype(BF16),
        "ffn_w_down": W["ffn_w_down"].astype(BF16),
        "a_conv_w": W["a_conv_w"], "ffn_conv_w": W["ffn_conv_w"],
        "e3": _head_sum_matrix3(),
    }
    for n in ("norm1_g", "norm2_g", "a_conv_b", "a_gx_b", "a_ga_b", "a_lambda", "b_mu", "b_w0", "b_a0", "b_k_k",
              "b_k_a", "b_r_k", "b_ln_w", "b_ln_b", "ffn_conv_b"):
        P[n] = row(W[n])
    return P


def _trunk(x, pos0, st, W, P):
    b, t, d = x.shape
    depth = W["w_in"].shape[0]
    has_cache = st["k"] is not None
    if has_cache:
        cos, sin = _rope_tables(pos0 + jnp.arange(t), 8)
        k_cache_t = jnp.transpose(st["k"], (0, 1, 3, 4, 2))
        v_cache_t = jnp.transpose(st["v"], (0, 1, 3, 4, 2))
    else:
        cos, sin = _rope_tables(pos0 + jnp.arange(t), t)
    ha = st["ha"][:, :, None, :]
    sh = st["sh"][:, :, None, :]
    sc_t = jnp.transpose(st["sc"], (0, 1, 2, 4, 3))
    outs = {n: [] for n in ("ha", "ca", "wkv", "sh", "sc", "k", "v", "cf")}
    kv_prev = None
    for l in range(depth):
        cols = _inproj(x.reshape(b * t, d), P, l).reshape(b, t, N_MIX_COLS)
        y_a, h_new, ca_new = _rglru(cols, ha, st["ca"], P, l)
        y_b, sh_new, wkv_new = _rwkv(cols, sh, st["wkv"], P, l)
        y_c, sc_new = _hgrn(cols, sc_t, P, l)
        if has_cache:
            y_d, k_rows = _attn_sample(cols, k_cache_t, v_cache_t, l, cos, sin)
            k_out = k_rows.reshape(b, t, N_HEADS, HEAD_W)
            v_out = cols[:, :, N_MIX_COLS - BRANCH_W:].reshape(b, t, N_HEADS, HEAD_W)
        else:
            y_d, k_t, v_t = _attn_prompt(cols, cos, sin, l, depth, kv_prev)
            kv_prev = (k_t, v_t)
        ys = [y.reshape(b * t, BRANCH_W) for y in (y_a, y_b, y_c, y_d)]
        x1 = _merge(x.reshape(b * t, d), ys, P, l)
        final_g = W["final_norm_g"][None, :] if l == depth - 1 else None
        x, cf_new = _ffn(x1.reshape(b, t, d), st["cf"], P, l, final_g)
        outs["ha"].append(h_new[:, 0, :])
        outs["ca"].append(ca_new)
        outs["wkv"].append(wkv_new)
        outs["sh"].append(sh_new[:, 0, :])
        outs["sc"].append(sc_new)
        outs["cf"].append(cf_new)
        if has_cache:
            outs["k"].append(k_out)
            outs["v"].append(v_out)
    if not has_cache:
        del outs["k"], outs["v"]
    res = {n: jnp.stack(outs[n]) for n in outs}
    res["sc"] = jnp.transpose(res["sc"], (0, 1, 2, 4, 3))
    if not has_cache:
        keep = min(D_WIN_MAX, t)
        for n, buf in zip(("k", "v"), kv_prev):
            res[n] = jnp.transpose(buf.reshape(depth, b, N_HEADS, HEAD_W, t)[..., t - keep:], (0, 1, 4, 2, 3))
    return (x,) + tuple(res[n] for n in ("ha", "ca", "wkv", "sh", "sc", "k", "v", "cf"))


def kernel(x_prompt, x_sample, state_a_h, state_a_conv, state_b_wkv, state_b_shift, state_c_s, cache_d_k, cache_d_v, state_ffn_conv, norm1_g, w_in, a_conv_w, a_conv_b, a_gx_w, a_gx_b, a_ga_w, a_ga_b, a_lambda, b_mu, b_w0, b_w2, b_a0, b_a2, b_g2, b_k_k, b_k_a, b_r_k, b_ln_w, b_ln_b, c_lb, c_norm_g, w_branch, w_out, norm2_g, ffn_w_gate, ffn_w_up, ffn_conv_w, ffn_conv_b, ffn_w_down, final_norm_g):
    W = {"norm1_g": norm1_g, "w_in": w_in, "a_conv_w": a_conv_w, "a_conv_b": a_conv_b, "a_gx_w": a_gx_w,
         "a_gx_b": a_gx_b, "a_ga_w": a_ga_w, "a_ga_b": a_ga_b, "a_lambda": a_lambda, "b_mu": b_mu,
         "b_w0": b_w0, "b_w2": b_w2, "b_a0": b_a0, "b_a2": b_a2, "b_g2": b_g2, "b_k_k": b_k_k,
         "b_k_a": b_k_a, "b_r_k": b_r_k, "b_ln_w": b_ln_w, "b_ln_b": b_ln_b, "c_lb": c_lb,
         "c_norm_g": c_norm_g, "w_branch": w_branch, "w_out": w_out, "norm2_g": norm2_g,
         "ffn_w_gate": ffn_w_gate, "ffn_w_up": ffn_w_up, "ffn_conv_w": ffn_conv_w,
         "ffn_conv_b": ffn_conv_b, "ffn_w_down": ffn_w_down, "final_norm_g": final_norm_g}
    depth = w_in.shape[0]
    P = _all_layer_params(W)
    b_p, t_p = x_prompt.shape[:2]

    def zeros(*s):
        return jnp.zeros((depth, b_p) + s, F32)

    st_p = {"ha": zeros(BRANCH_W), "ca": zeros(3, BRANCH_W), "wkv": zeros(N_HEADS, HEAD_W, HEAD_W),
            "sh": zeros(B_COLS), "sc": zeros(N_HEADS, C_EXPAND, HEAD_W), "k": None, "v": None,
            "cf": zeros(2, D_FF)}
    out_p = _trunk(x_prompt, 0, st_p, W, P)
    st_s = {"ha": state_a_h, "ca": state_a_conv, "wkv": state_b_wkv, "sh": state_b_shift, "sc": state_c_s,
            "k": cache_d_k, "v": cache_d_v, "cf": state_ffn_conv}
    out_s = _trunk(x_sample, PAST_LEN, st_s, W, P)
    return (out_p[0], out_s[0]) + out_p[1:] + out_s[1:]
```
